```python
import jax, jax.numpy as jnp
from jax import lax
import numpy as np

D_MODEL = 1024
BATCH = 4
SEQ = 4096
DEPTH = 4

HEAD_DIM = 64
D_ATTN = D_MODEL // 2
N_Q_HEADS = D_ATTN // HEAD_DIM
N_KV_HEADS = 2
Q_PER_KV = N_Q_HEADS // N_KV_HEADS
D_KV = N_KV_HEADS * HEAD_DIM
D_FOURIER = D_MODEL - D_ATTN
N_FOURIER_GROUPS = 8
FOURIER_GROUP = D_FOURIER // N_FOURIER_GROUPS
D_IN = D_ATTN + 2 * D_KV + D_FOURIER
D_MIX = D_ATTN + D_FOURIER
WINDOW = 128
BLOCK = 128
ROPE_THETA = 500000.0
ROT_DIM = HEAD_DIM // 4
D_FF = 3584
N_EXPERTS = 8
TOP_K = 2
EPS = 1e-6
N_DENSE = (DEPTH + 1) // 2
N_MOE = DEPTH // 2

kernel_name = "hybrid_swa_fnet_moe_encoder"


def rmsnorm(x, g):
    xf = x.astype(jnp.float32)
    y = xf * lax.rsqrt(jnp.mean(xf * xf, axis=-1, keepdims=True) + EPS)
    return (y * g.astype(jnp.float32)).astype(x.dtype)


def rope_partial(x, cos, sin):
    half = ROT_DIM // 2
    x1 = x[..., :half]
    x2 = x[..., half:ROT_DIM]
    rot = jnp.concatenate([x1 * cos - x2 * sin, x2 * cos + x1 * sin], axis=-1)
    return jnp.concatenate([rot, x[..., ROT_DIM:]], axis=-1)


def banded_sink_attention(q, k, v, sink):
    B, S = q.shape[0], q.shape[1]
    nb = S // BLOCK
    qb = q.reshape(B, nb, BLOCK, N_KV_HEADS, Q_PER_KV, HEAD_DIM)

    def band(t):
        tp = jnp.pad(t, ((0, 0), (BLOCK, BLOCK), (0, 0), (0, 0)))
        tp = tp.reshape(B, nb + 2, BLOCK, N_KV_HEADS, HEAD_DIM)
        return jnp.concatenate([tp[:, :-2], tp[:, 1:-1], tp[:, 2:]], axis=2)

    kb, vb = band(k), band(v)
    scores = jnp.einsum('bnqhgd,bnkhd->bnhgqk', qb, kb).astype(jnp.float32) * (HEAD_DIM ** -0.5)
    qi = jnp.arange(BLOCK)[:, None]
    kj = jnp.arange(3 * BLOCK)[None, :]
    blk = jnp.arange(nb)[:, None, None]
    k_abs = blk * BLOCK - BLOCK + kj
    mask = (jnp.abs(kj - BLOCK - qi) <= WINDOW) & (k_abs >= 0) & (k_abs < S)
    scores = jnp.where(mask[None, :, None, None], scores, -jnp.inf)
    sink_b = sink.astype(jnp.float32).reshape(N_KV_HEADS, Q_PER_KV)[None, None, :, :, None, None]
    m = jnp.maximum(jnp.max(scores, axis=-1, keepdims=True), sink_b)
    p = jnp.exp(scores - m)
    probs = p / (jnp.sum(p, axis=-1, keepdims=True) + jnp.exp(sink_b - m))
    out = jnp.einsum('bnhgqk,bnkhd->bnqhgd', probs.astype(v.dtype), vb)
    return out.reshape(B, S, N_Q_HEADS * HEAD_DIM)


def fourier_mix(f, w_fmix):
    B, S = f.shape[0], f.shape[1]
    fg = f.astype(jnp.float32).reshape(B, S, N_FOURIER_GROUPS, FOURIER_GROUP)
    spec = jnp.fft.fft2(fg, axes=(1, 3), norm="ortho").real.astype(f.dtype)
    return jnp.einsum('bsgc,gcd->bsgd', spec, w_fmix).reshape(B, S, D_FOURIER)


def swiglu(h, w_gate, w_up, w_down):
    return (jax.nn.silu(h @ w_gate) * (h @ w_up)) @ w_down


def moe_swiglu(h, w_router, e_gate, e_up, e_down):
    B, S, D = h.shape
    t = h.reshape(B * S, D)
    logits = (t @ w_router).astype(jnp.float32)
    top_vals, top_idx = lax.top_k(logits, TOP_K)
    top_w = jax.nn.softmax(top_vals, axis=-1)
    gates = jnp.sum(jax.nn.one_hot(top_idx, N_EXPERTS, dtype=jnp.float32) * top_w[..., None], axis=1)
    out = jnp.zeros_like(t)
    for e in range(N_EXPERTS):
        out = out + gates[:, e:e + 1].astype(t.dtype) * swiglu(t, e_gate[e], e_up[e], e_down[e])
    return out.reshape(B, S, D)


def setup_inputs(seed: int = 0) -> dict:
    key = jax.random.key(seed)
    ks = jax.random.split(key, 20)
    f32 = jnp.float32

    def nrm(k, shape, scale):
        return jax.random.normal(k, shape, f32) * scale

    def gain(k, shape):
        return 1.0 + 0.1 * jax.random.normal(k, shape, f32)

    offsets = jax.random.randint(ks[1], (BATCH, 1), 0, 1024, dtype=jnp.int32)
    positions = jnp.arange(SEQ, dtype=jnp.int32)[None, :] + offsets
    return {
        "x": jax.random.normal(ks[0], (BATCH, SEQ, D_MODEL), f32),
        "positions": positions,
        "norm_mix": gain(ks[2], (DEPTH, D_MODEL)),
        "w_in": nrm(ks[3], (DEPTH, D_MODEL, D_IN), D_MODEL ** -0.5),
        "q_norm": gain(ks[4], (DEPTH, HEAD_DIM)),
        "k_norm": gain(ks[5], (DEPTH, HEAD_DIM)),
        "sink": nrm(ks[6], (DEPTH, N_Q_HEADS), 0.5),
        "w_fmix": nrm(ks[7], (DEPTH, N_FOURIER_GROUPS, FOURIER_GROUP, FOURIER_GROUP), FOURIER_GROUP ** -0.5),
        "g_attn_out": gain(ks[8], (DEPTH, D_ATTN)),
        "g_fourier_out": gain(ks[9], (DEPTH, D_FOURIER)),
        "w_out": nrm(ks[10], (DEPTH, D_MIX, D_MODEL), D_MIX ** -0.5),
        "norm_ffn": gain(ks[11], (DEPTH, D_MODEL)),
        "ffn_gate": nrm(ks[12], (N_DENSE, D_MODEL, D_FF), D_MODEL ** -0.5),
        "ffn_up": nrm(ks[13], (N_DENSE, D_MODEL, D_FF), D_MODEL ** -0.5),
        "ffn_down": nrm(ks[14], (N_DENSE, D_FF, D_MODEL), D_FF ** -0.5),
        "w_router": nrm(ks[15], (N_MOE, D_MODEL, N_EXPERTS), D_MODEL ** -0.5),
        "e_gate": nrm(ks[16], (N_MOE, N_EXPERTS, D_MODEL, D_FF), D_MODEL ** -0.5),
        "e_up": nrm(ks[17], (N_MOE, N_EXPERTS, D_MODEL, D_FF), D_MODEL ** -0.5),
        "e_down": nrm(ks[18], (N_MOE, N_EXPERTS, D_FF, D_MODEL), D_FF ** -0.5),
    }


def reference(x, positions, norm_mix, w_in, q_norm, k_norm, sink, w_fmix, g_attn_out, g_fourier_out,
              w_out, norm_ffn, ffn_gate, ffn_up, ffn_down, w_router, e_gate, e_up, e_down):
    B, S = x.shape[0], x.shape[1]
    inv_freq = ROPE_THETA ** (-jnp.arange(0, ROT_DIM, 2, dtype=jnp.float32) / ROT_DIM)
    ang = positions.astype(jnp.float32)[..., None] * inv_freq
    cos = jnp.cos(ang)[:, :, None, :].astype(x.dtype)
    sin = jnp.sin(ang)[:, :, None, :].astype(x.dtype)

    for l in range(DEPTH):
        h = rmsnorm(x, norm_mix[l])
        z = h @ w_in[l]
        q = z[..., :D_ATTN].reshape(B, S, N_Q_HEADS, HEAD_DIM)
        k = z[..., D_ATTN:D_ATTN + D_KV].reshape(B, S, N_KV_HEADS, HEAD_DIM)
        v = z[..., D_ATTN + D_KV:D_ATTN + 2 * D_KV].reshape(B, S, N_KV_HEADS, HEAD_DIM)
        f = z[..., D_ATTN + 2 * D_KV:]
        q = rope_partial(rmsnorm(q, q_norm[l]), cos, sin)
        k = rope_partial(rmsnorm(k, k_norm[l]), cos, sin)
        a = banded_sink_attention(q, k, v, sink[l])
        fo = fourier_mix(f, w_fmix[l])
        mixed = jnp.concatenate([rmsnorm(a, g_attn_out[l]), rmsnorm(fo, g_fourier_out[l])], axis=-1)
        x = x + mixed @ w_out[l]
        h = rmsnorm(x, norm_ffn[l])
        i = l // 2
        if l % 2 == 0:
            x = x + swiglu(h, ffn_gate[i], ffn_up[i], ffn_down[i])
        else:
            x = x + moe_swiglu(h, w_router[i], e_gate[i], e_up[i], e_down[i])
    return x
```

```python
import functools

import jax
import jax.numpy as jnp
import numpy as np
from jax import lax
from jax.experimental import pallas as pl
from jax.experimental.pallas import tpu as pltpu

F32 = jnp.float32
BF16 = jnp.bfloat16

D_MODEL = 1024
HEAD_DIM = 64
D_ATTN = 512
N_Q_HEADS = 8
N_KV_HEADS = 2
Q_PER_KV = 4
D_KV = 128
D_FOURIER = 512
N_FOURIER_GROUPS = 8
FOURIER_GROUP = 64
D_QK = D_ATTN + D_KV
D_IN = D_ATTN + 2 * D_KV + D_FOURIER
WINDOW = 128
ROPE_THETA = 500000.0
ROT_DIM = 16
D_FF = 3584
N_EXPERTS = 8
EPS = 1e-6

V7X_LANES = 128
V7X_VMEM_LIMIT = 56 * 1024 * 1024

TM = 512
TQ = 256
KV_WIN = TQ + 2 * WINDOW
FFT_N1 = 4
FFT_N2 = 1024
FFT_CHUNK = 128
TM_FFN = 1024
TF_FFN = 512
TM_MOE = 512
TF_UP = 896
TN_DOWN = 512
TM_PLAN = 1024
TM_DMA = 1024


def _cparams(sem, vmem=V7X_VMEM_LIMIT):
    return pltpu.CompilerParams(dimension_semantics=sem, vmem_limit_bytes=vmem)


def _rms(xf, g):
    ms = jnp.mean(xf * xf, axis=-1, keepdims=True)
    return xf * lax.rsqrt(ms + EPS) * g


def _split_bf16(xf):
    hi = xf.astype(BF16)
    lo = (xf - hi.astype(F32)).astype(BF16)
    return hi, lo


def _in_proj_kernel(x_ref, g_ref, w_ref, bd_ref, gain_ref, cos_ref, sa_ref, sb_ref,
                    q_ref, k_ref, v_ref, f_ref, wbf_ref):
    @pl.when(pl.program_id(0) == 0)
    def _():
        wbf_ref[...] = w_ref[...].astype(BF16)

    h = _rms(x_ref[...], g_ref[...]).astype(BF16)
    z = jnp.dot(h, wbf_ref[...], preferred_element_type=F32)
    qk = z[:, :D_QK]
    hi, lo = _split_bf16(qk * qk)
    bd = bd_ref[...]
    ms = (jnp.dot(hi, bd, preferred_element_type=F32) + jnp.dot(lo, bd, preferred_element_type=F32))
    qkn = qk * lax.rsqrt(ms + EPS) * gain_ref[...]
    cos_t, sin_a, sin_b = cos_ref[...], sa_ref[...], sb_ref[...]
    for c in range(D_QK // V7X_LANES):
        blk = qkn[:, c * V7X_LANES:(c + 1) * V7X_LANES]
        rot = (blk * cos_t + pltpu.roll(blk, ROT_DIM // 2, 1) * sin_a
               + pltpu.roll(blk, V7X_LANES - ROT_DIM // 2, 1) * sin_b).astype(BF16)
        if c < D_ATTN // V7X_LANES:
            q_ref[:, c * V7X_LANES:(c + 1) * V7X_LANES] = rot
        else:
            k_ref[...] = rot
    v_ref[...] = z[:, D_QK:D_QK + D_KV].astype(BF16)
    f_ref[...] = z[:, D_QK + D_KV:].astype(BF16)


def _in_proj(x, g, w, bd, gain, cos_t, sin_a, sin_b):
    t = x.shape[0]
    row = lambda i: (i, 0)
    fixed = lambda i: (0, 0)
    return pl.pallas_call(
        _in_proj_kernel,
        grid=(t // TM,),
        in_specs=[
            pl.BlockSpec((TM, D_MODEL), row),
            pl.BlockSpec((1, D_MODEL), fixed),
            pl.BlockSpec((D_MODEL, D_IN), fixed),
            pl.BlockSpec((D_QK, D_QK), fixed),
            pl.BlockSpec((1, D_QK), fixed),
            pl.BlockSpec((TM, V7X_LANES), row),
            pl.BlockSpec((TM, V7X_LANES), row),
            pl.BlockSpec((TM, V7X_LANES), row),
        ],
        out_specs=[
            pl.BlockSpec((TM, D_ATTN), row),
            pl.BlockSpec((TM, D_KV), row),
            pl.BlockSpec((TM, D_KV), row),
            pl.BlockSpec((TM, D_FOURIER), row),
        ],
        out_shape=[
            jax.ShapeDtypeStruct((t, D_ATTN), BF16),
            jax.ShapeDtypeStruct((t, D_KV), BF16),
            jax.ShapeDtypeStruct((t, D_KV), BF16),
            jax.ShapeDtypeStruct((t, D_FOURIER), BF16),
        ],
        scratch_shapes=[pltpu.VMEM((D_MODEL, D_IN), BF16)],
        compiler_params=_cparams(("arbitrary",)),
        name="in_proj",
    )(x, g, w, bd, gain, cos_t, sin_a, sin_b)


def _attn_kernel(sink_ref, q_ref, k_ref, v_ref, g_ref, o_ref, *, seq):
    qs = pl.program_id(1) * TQ
    ws = pl.multiple_of(jnp.clip(qs - WINDOW, 0, seq - KV_WIN), WINDOW)
    kw = k_ref[pl.ds(ws, KV_WIN), :]
    vw = v_ref[pl.ds(ws, KV_WIN), :]
    q_abs = qs + lax.broadcasted_iota(jnp.int32, (TQ, KV_WIN), 0)
    k_abs = ws + lax.broadcasted_iota(jnp.int32, (TQ, KV_WIN), 1)
    mask = jnp.abs(q_abs - k_abs) <= WINDOW
    q = q_ref[...]
    outs = []
    for h in range(N_Q_HEADS):
        kvh = h // Q_PER_KV
        qh = q[:, h * HEAD_DIM:(h + 1) * HEAD_DIM]
        kh = kw[:, kvh * HEAD_DIM:(kvh + 1) * HEAD_DIM]
        vh = vw[:, kvh * HEAD_DIM:(kvh + 1) * HEAD_DIM]
        s = lax.dot_general(qh, kh, (((1,), (1,)), ((), ())), preferred_element_type=F32)
        s = jnp.where(mask, s, -jnp.inf)
        sink = sink_ref[h]
        m = jnp.maximum(jnp.max(s, axis=-1, keepdims=True), sink)
        p = jnp.exp(s - m)
        denom = jnp.sum(p, axis=-1, keepdims=True) + jnp.exp(sink - m)
        o = jnp.dot(p.astype(BF16), vh, preferred_element_type=F32)
        outs.append(o / denom)
    a = jnp.concatenate(outs, axis=-1)
    o_ref[...] = _rms(a, g_ref[...]).astype(BF16)


def _attention(q, k, v, sink, g):
    b, s, _ = q.shape
    return pl.pallas_call(
        functools.partial(_attn_kernel, seq=s),
        grid=(b, s // TQ),
        in_specs=[
            pl.BlockSpec(memory_space=pltpu.SMEM),
            pl.BlockSpec((None, TQ, D_ATTN), lambda i, j: (i, j, 0)),
            pl.BlockSpec((None, s, D_KV), lambda i, j: (i, 0, 0)),
            pl.BlockSpec((None, s, D_KV), lambda i, j: (i, 0, 0)),
            pl.BlockSpec((1, D_ATTN), lambda i, j: (0, 0)),
        ],
        out_specs=pl.BlockSpec((None, TQ, D_ATTN), lambda i, j: (i, j, 0)),
        out_shape=jax.ShapeDtypeStruct((b, s, D_ATTN), BF16),
        compiler_params=_cparams(("arbitrary", "arbitrary")),
        name="band_attn",
    )(sink, q, k, v, g)


def _fmix_prep_kernel(w_ref, cc_ref, sc_ref, o_ref):
    o_ref[...] = jnp.zeros(o_ref.shape, o_ref.dtype)
    scale = 1.0 / 512.0
    for g in range(N_FOURIER_GROUPS):
        w = w_ref[g]
        mr = jnp.dot(cc_ref[...], w, preferred_element_type=F32, precision=lax.Precision.HIGHEST) * scale
        mi = jnp.dot(sc_ref[...], w, preferred_element_type=F32, precision=lax.Precision.HIGHEST) * (-scale)
        c = (g * FOURIER_GROUP) // FFT_CHUNK
        off = (g * FOURIER_GROUP) % FFT_CHUNK
        rows = slice(g * FOURIER_GROUP, (g + 1) * FOURIER_GROUP)
        o_ref[c, rows, off:off + FOURIER_GROUP] = mr.astype(BF16)
        o_ref[c, rows, FFT_CHUNK + off:FFT_CHUNK + off + FOURIER_GROUP] = mi.astype(BF16)


def _fmix_prep(w_fmix, cc, sc):
    depth = w_fmix.shape[0]
    n_chunk = D_FOURIER // FFT_CHUNK
    return pl.pallas_call(
        _fmix_prep_kernel,
        grid=(depth,),
        in_specs=[
            pl.BlockSpec((None, N_FOURIER_GROUPS, FOURIER_GROUP, FOURIER_GROUP), lambda l: (l, 0, 0, 0)),
            pl.BlockSpec((FOURIER_GROUP, FOURIER_GROUP), lambda l: (0, 0)),
            pl.BlockSpec((FOURIER_GROUP, FOURIER_GROUP), lambda l: (0, 0)),
        ],
        out_specs=pl.BlockSpec((None, n_chunk, D_FOURIER, 2 * FFT_CHUNK), lambda l: (l, 0, 0, 0)),
        out_shape=jax.ShapeDtypeStruct((depth, n_chunk, D_FOURIER, 2 * FFT_CHUNK), BF16),
        compiler_params=_cparams(("arbitrary",)),
        name="fmix_prep",
    )(w_fmix, cc, sc)


def _fourier_kernel(f_ref, m_ref, c0_ref, s0_ref, cb_ref, sb_ref, g_ref, o_ref, v_ref):
    k1 = pl.program_id(1)

    @pl.when(k1 == 0)
    def _():
        for c in range(D_FOURIER // FFT_CHUNK):
            mc = m_ref[c]
            z = [jnp.dot(f_ref[n1 * FFT_N2:(n1 + 1) * FFT_N2, :], mc, preferred_element_type=F32)
                 for n1 in range(FFT_N1)]
            zr = [t[:, :FFT_CHUNK] for t in z]
            zi = [t[:, FFT_CHUNK:] for t in z]
            ar, ai = zr[0] + zr[2], zi[0] + zi[2]
            br, bi = zr[1] + zr[3], zi[1] + zi[3]
            cr, ci = zr[0] - zr[2], zi[0] - zi[2]
            dr, di = zr[1] - zr[3], zi[1] - zi[3]
            u = [(ar + br, ai + bi), (cr + di, ci - dr), (ar - br, ai - bi), (cr - di, ci + dr)]
            for kk in range(FFT_N1):
                ur, ui = u[kk]
                cb, sb = cb_ref[kk], sb_ref[kk]
                v_ref[kk, :, c * FFT_CHUNK:(c + 1) * FFT_CHUNK] = (ur * cb + ui * sb).astype(BF16)
                v_ref[kk, :, D_FOURIER + c * FFT_CHUNK:D_FOURIER + (c + 1) * FFT_CHUNK] = (
                    ui * cb - ur * sb).astype(BF16)

    vr = v_ref[k1, :, :D_FOURIER]
    vi = v_ref[k1, :, D_FOURIER:]
    out = (jnp.dot(c0_ref[...], vr, preferred_element_type=F32)
           + jnp.dot(s0_ref[...], vi, preferred_element_type=F32))
    o_ref[...] = _rms(out, g_ref[...]).astype(BF16)


def _fourier(f, mst, c0, s0, cb, sb, g):
    b, s, _ = f.shape
    n_chunk = D_FOURIER // FFT_CHUNK
    out = pl.pallas_call(
        _fourier_kernel,
        grid=(b, FFT_N1),
        in_specs=[
            pl.BlockSpec((None, s, D_FOURIER), lambda i, j: (i, 0, 0)),
            pl.BlockSpec((n_chunk, D_FOURIER, 2 * FFT_CHUNK), lambda i, j: (0, 0, 0)),
            pl.BlockSpec((FFT_N2, FFT_N2), lambda i, j: (0, 0)),
            pl.BlockSpec((FFT_N2, FFT_N2), lambda i, j: (0, 0)),
            pl.BlockSpec((FFT_N1, FFT_N2, V7X_LANES), lambda i, j: (0, 0, 0)),
            pl.BlockSpec((FFT_N1, FFT_N2, V7X_LANES), lambda i, j: (0, 0, 0)),
            pl.BlockSpec((1, D_FOURIER), lambda i, j: (0, 0)),
        ],
        out_specs=pl.BlockSpec((None, FFT_N2, D_FOURIER), lambda i, j: (i, 0, j)),
        out_shape=jax.ShapeDtypeStruct((b, FFT_N2, FFT_N1 * D_FOURIER), BF16),
        scratch_shapes=[pltpu.VMEM((FFT_N1, FFT_N2, 2 * D_FOURIER), BF16)],
        compiler_params=_cparams(("arbitrary", "arbitrary")),
        name="fourier_mix",
    )(f, mst, c0, s0, cb, sb, g)
    return out.reshape(b, s, D_FOURIER)


def _out_proj_body(a_ref, f_ref, x_ref, w_ref, g_ref, wbf_ref):
    @pl.when(pl.program_id(0) == 0)
    def _():
        wbf_ref[...] = w_ref[...].astype(BF16)

    y = (jnp.dot(a_ref[...], wbf_ref[:D_ATTN, :], preferred_element_type=F32)
         + jnp.dot(f_ref[...], wbf_ref[D_ATTN:, :], preferred_element_type=F32))
    xn = x_ref[...] + y
    return xn, _rms(xn, g_ref[...])


def _out_proj_dense_kernel(a_ref, f_ref, x_ref, w_ref, g_ref, xo_ref, h_ref, wbf_ref):
    xn, h = _out_proj_body(a_ref, f_ref, x_ref, w_ref, g_ref, wbf_ref)
    xo_ref[...] = xn
    h_ref[...] = h.astype(BF16)


def _out_proj_moe_kernel(a_ref, f_ref, x_ref, w_ref, g_ref, wr_ref, lt_ref,
                         xo_ref, hp_ref, gate_ref, rank_ref, cnt_ref, wbf_ref, wrh_ref, wrl_ref, carry_ref):
    @pl.when(pl.program_id(0) == 0)
    def _():
        hi, lo = _split_bf16(wr_ref[...])
        wrh_ref[...] = hi
        wrl_ref[...] = lo
        carry_ref[...] = jnp.zeros(carry_ref.shape, F32)

    xn, h = _out_proj_body(a_ref, f_ref, x_ref, w_ref, g_ref, wbf_ref)
    xo_ref[...] = xn
    half = D_MODEL // 2
    lo_bits = lax.shift_right_logical(
        lax.bitcast_convert_type(h[:, :half].astype(BF16).astype(F32), jnp.uint32), jnp.uint32(16))
    hi_bits = lax.bitcast_convert_type(h[:, half:].astype(BF16).astype(F32), jnp.uint32) & jnp.uint32(0xFFFF0000)
    hp_ref[...] = lo_bits | hi_bits
    h_hi, h_lo = _split_bf16(h)
    logits = (jnp.dot(h_hi, wrh_ref[...], preferred_element_type=F32)
              + jnp.dot(h_lo, wrh_ref[...], preferred_element_type=F32)
              + jnp.dot(h_hi, wrl_ref[...], preferred_element_type=F32))
    lane = lax.broadcasted_iota(jnp.int32, logits.shape, 1).astype(F32)
    logits = jnp.where(lane < N_EXPERTS, logits, -jnp.inf)
    m1 = jnp.max(logits, axis=-1, keepdims=True)
    i1 = jnp.min(jnp.where(logits == m1, lane, float(V7X_LANES)), axis=-1, keepdims=True)
    rest = jnp.where(lane == i1, -jnp.inf, logits)
    m2 = jnp.max(rest, axis=-1, keepdims=True)
    i2 = jnp.min(jnp.where(rest == m2, lane, float(V7X_LANES)), axis=-1, keepdims=True)
    e2 = jnp.exp(m2 - m1)
    den = 1.0 + e2
    sel1, sel2 = lane == i1, lane == i2
    sel = sel1 | sel2
    gate_ref[...] = jnp.where(sel1, 1.0 / den, jnp.where(sel2, e2 / den, 0.0))
    self32 = sel.astype(F32)
    rank = jnp.dot(lt_ref[...], self32.astype(BF16), preferred_element_type=F32) + carry_ref[...]
    rank_ref[...] = jnp.where(sel, rank, -1.0)
    carry_ref[...] = carry_ref[...] + jnp.sum(self32, axis=0, keepdims=True)
    cnt_ref[...] = carry_ref[...]


def _out_proj(a, f, x, w, g, moe=None):
    t = x.shape[0]
    row = lambda i: (i, 0)
    fixed = lambda i: (0, 0)
    in_specs = [
        pl.BlockSpec((TM, D_ATTN), row),
        pl.BlockSpec((TM, D_FOURIER), row),
        pl.BlockSpec((TM, D_MODEL), row),
        pl.BlockSpec((D_MODEL, D_MODEL), fixed),
        pl.BlockSpec((1, D_MODEL), fixed),
    ]
    if moe is None:
        return pl.pallas_call(
            _out_proj_dense_kernel,
            grid=(t // TM,),
            in_specs=in_specs,
            out_specs=[pl.BlockSpec((TM, D_MODEL), row), pl.BlockSpec((TM, D_MODEL), row)],
            out_shape=[jax.ShapeDtypeStruct((t, D_MODEL), F32), jax.ShapeDtypeStruct((t, D_MODEL), BF16)],
            scratch_shapes=[pltpu.VMEM((D_MODEL, D_MODEL), BF16)],
            compiler_params=_cparams(("arbitrary",)),
            name="out_proj",
        )(a, f, x, w, g)
    w_router_pad, ltri = moe
    return pl.pallas_call(
        _out_proj_moe_kernel,
        grid=(t // TM,),
        in_specs=in_specs + [pl.BlockSpec((D_MODEL, V7X_LANES), fixed), pl.BlockSpec((TM, TM), fixed)],
        out_specs=[
            pl.BlockSpec((TM, D_MODEL), row),
            pl.BlockSpec((TM, D_MODEL // 2), row),
            pl.BlockSpec((TM, V7X_LANES), row),
            pl.BlockSpec((TM, V7X_LANES), row),
            pl.BlockSpec((1, V7X_LANES), fixed),
        ],
        out_shape=[
            jax.ShapeDtypeStruct((t, D_MODEL), F32),
            jax.ShapeDtypeStruct((t, D_MODEL // 2), jnp.uint32),
            jax.ShapeDtypeStruct((t, V7X_LANES), F32),
            jax.ShapeDtypeStruct((t, V7X_LANES), F32),
            jax.ShapeDtypeStruct((1, V7X_LANES), F32),
        ],
        scratch_shapes=[
            pltpu.VMEM((D_MODEL, D_MODEL), BF16),
            pltpu.VMEM((D_MODEL, V7X_LANES), BF16),
            pltpu.VMEM((D_MODEL, V7X_LANES), BF16),
            pltpu.VMEM((1, V7X_LANES), F32),
        ],
        compiler_params=_cparams(("arbitrary",)),
        name="out_proj_route",
    )(a, f, x, w, g, w_router_pad, ltri)


def _ffn_kernel(x_ref, h_ref, wg_ref, wu_ref, wd_ref, o_ref):
    @pl.when(pl.program_id(1) == 0)
    def _():
        o_ref[...] = x_ref[...]

    h = h_ref[...]
    g = jnp.dot(h, wg_ref[...].astype(BF16), preferred_element_type=F32)
    u = jnp.dot(h, wu_ref[...].astype(BF16), preferred_element_type=F32)
    act = (g * jax.nn.sigmoid(g) * u).astype(BF16)
    o_ref[...] += jnp.dot(act, wd_ref[...].astype(BF16), preferred_element_type=F32)


def _ffn(x, h, wg, wu, wd):
    t = x.shape[0]
    return pl.pallas_call(
        _ffn_kernel,
        grid=(t // TM_FFN, D_FF // TF_FFN),
        in_specs=[
            pl.BlockSpec((TM_FFN, D_MODEL), lambda i, j: (i, 0)),
            pl.BlockSpec((TM_FFN, D_MODEL), lambda i, j: (i, 0)),
            pl.BlockSpec((D_MODEL, TF_FFN), lambda i, j: (0, j)),
            pl.BlockSpec((D_MODEL, TF_FFN), lambda i, j: (0, j)),
            pl.BlockSpec((TF_FFN, D_MODEL), lambda i, j: (j, 0)),
        ],
        out_specs=pl.BlockSpec((TM_FFN, D_MODEL), lambda i, j: (i, 0)),
        out_shape=jax.ShapeDtypeStruct((t, D_MODEL), F32),
        compiler_params=_cparams(("arbitrary", "arbitrary")),
        name="ffn_dense",
    )(x, h, wg, wu, wd)


def _plan_kernel(gate_ref, rank_ref, cnt_ref, pos_ref, wgt_ref, meta_ref, *, n_tiles_max):
    lane = lax.broadcasted_iota(jnp.int32, (1, V7X_LANES), 1)
    cnt = cnt_ref[...]
    padded = jnp.floor((cnt + (TM_MOE - 1)) * (1.0 / TM_MOE)) * TM_MOE
    base = jnp.zeros_like(padded)
    for j in range(1, N_EXPERTS):
        base = base + jnp.where(lane >= j, pltpu.roll(padded, j, 1), 0.0)
    rank = rank_ref[...]
    pos_t = jnp.where(rank >= 0.0, base + rank, -1.0).T
    gate_t = gate_ref[...].T
    seen = jnp.zeros((1, TM_PLAN), F32)
    pos_lo = jnp.zeros((1, TM_PLAN), F32)
    pos_hi = jnp.zeros((1, TM_PLAN), F32)
    w_lo = jnp.zeros((1, TM_PLAN), F32)
    w_hi = jnp.zeros((1, TM_PLAN), F32)
    for e in range(N_EXPERTS):
        p = pos_t[e:e + 1, :]
        gt = gate_t[e:e + 1, :]
        chosen = p >= 0.0
        first = chosen & (seen == 0.0)
        second = chosen & (seen == 1.0)
        pos_lo = jnp.where(first, p, pos_lo)
        pos_hi = jnp.where(second, p, pos_hi)
        w_lo = jnp.where(first, gt, w_lo)
        w_hi = jnp.where(second, gt, w_hi)
        seen = seen + chosen.astype(F32)
    pos_ref[0:1, :] = pos_lo.astype(jnp.int32)
    pos_ref[1:2, :] = pos_hi.astype(jnp.int32)
    wgt_ref[0:1, :] = w_lo
    wgt_ref[1:2, :] = w_hi
    ends = base + padded
    tile_start = (lane * TM_MOE).astype(F32)
    tile_e = jnp.zeros((1, V7X_LANES), F32)
    for e in range(N_EXPERTS):
        tile_e = tile_e + (ends[:, e:e + 1] <= tile_start).astype(F32)
    tile_e = jnp.minimum(tile_e, float(N_EXPERTS - 1))
    n_used = ends[:, N_EXPERTS - 1:N_EXPERTS] * (1.0 / TM_MOE) + jnp.zeros((1, V7X_LANES), F32)
    meta_ref[0:1, :] = tile_e.astype(jnp.int32)
    meta_ref[1:2, :] = n_used.astype(jnp.int32)
    meta_ref[2:3, :] = cnt.astype(jnp.int32)
    meta_ref[3:4, :] = base.astype(jnp.int32)
    meta_ref[4:8, :] = jnp.zeros((4, V7X_LANES), jnp.int32)


def _plan(gates, rank, counts, n_tiles_max):
    t = gates.shape[0]
    return pl.pallas_call(
        functools.partial(_plan_kernel, n_tiles_max=n_tiles_max),
        grid=(t // TM_PLAN,),
        in_specs=[
            pl.BlockSpec((TM_PLAN, V7X_LANES), lambda i: (i, 0)),
            pl.BlockSpec((TM_PLAN, V7X_LANES), lambda i: (i, 0)),
            pl.BlockSpec((1, V7X_LANES), lambda i: (0, 0)),
        ],
        out_specs=[
            pl.BlockSpec((2, TM_PLAN), lambda i: (0, i)),
            pl.BlockSpec((2, TM_PLAN), lambda i: (0, i)),
            pl.BlockSpec((8, V7X_LANES), lambda i: (0, 0)),
        ],
        out_shape=[
            jax.ShapeDtypeStruct((2, t), jnp.int32),
            jax.ShapeDtypeStruct((2, t), F32),
            jax.ShapeDtypeStruct((8, V7X_LANES), jnp.int32),
        ],
        compiler_params=_cparams(("arbitrary",)),
        name="moe_plan",
    )(gates, rank, counts)


def _row_copy(src, src_row, dst, dst_row, sem):
    return pltpu.make_async_copy(src.at[pl.ds(src_row, 1)], dst.at[pl.ds(dst_row, 1)], sem)


def _dispatch_kernel(plo_ref, phi_ref, cnt_ref, base_ref, nt_ref, h_ref, z_ref, o_ref, sem):
    i = pl.program_id(0)
    t0 = i * TM_DMA

    def issue(r, carry):
        _row_copy(h_ref, t0 + r, o_ref, plo_ref[t0 + r], sem).start()
        _row_copy(h_ref, t0 + r, o_ref, phi_ref[t0 + r], sem).start()
        return carry

    lax.fori_loop(0, TM_DMA, issue, 0)

    def drain(r, carry):
        _row_copy(h_ref, 0, o_ref, 0, sem).wait()
        return carry

    lax.fori_loop(0, 2 * TM_DMA, drain, 0)

    @pl.when(i == pl.num_programs(0) - 1)
    def _():
        for e in range(N_EXPERTS):
            n = cnt_ref[e]
            n_pad = ((n + (TM_MOE - 1)) // TM_MOE) * TM_MOE - n
            first = base_ref[e] + n

            def pad_issue(r, carry, first=first):
                _row_copy(z_ref, 0, o_ref, first + r, sem).start()
                return carry

            def pad_drain(r, carry):
                _row_copy(z_ref, 0, o_ref, 0, sem).wait()
                return carry

            lax.fori_loop(0, n_pad, pad_issue, 0)
            lax.fori_loop(0, n_pad, pad_drain, 0)

        def tail(tile, carry):
            cp = pltpu.make_async_copy(z_ref, o_ref.at[pl.ds(pl.multiple_of(tile * TM_MOE, TM_MOE), TM_MOE)], sem)
            cp.start()
            cp.wait()
            return carry

        lax.fori_loop(nt_ref[0], o_ref.shape[0] // TM_MOE, tail, 0)


def _dispatch(pos_lo, pos_hi, cnt, base, n_used, hpk, n_rows):
    t = hpk.shape[0]
    zrow = jnp.zeros((TM_MOE, D_MODEL // 2), jnp.uint32)
    return pl.pallas_call(
        _dispatch_kernel,
        grid_spec=pltpu.PrefetchScalarGridSpec(
            num_scalar_prefetch=5,
            grid=(t // TM_DMA,),
            in_specs=[pl.BlockSpec(memory_space=pl.ANY), pl.BlockSpec(memory_space=pl.ANY)],
            out_specs=pl.BlockSpec(memory_space=pl.ANY),
            scratch_shapes=[pltpu.SemaphoreType.DMA(())],
        ),
        out_shape=jax.ShapeDtypeStruct((n_rows, D_MODEL // 2), jnp.uint32),
        compiler_params=_cparams(("arbitrary",)),
        name="moe_dispatch",
    )(pos_lo, pos_hi, cnt, base, n_used, hpk, zrow)


def _tile_of(i, nt_ref):
    return jnp.minimum(i, nt_ref[0] - 1)


def _expert_changed(i, te_ref, nt_ref):
    ii = _tile_of(i, nt_ref)
    return (i == 0) | (te_ref[ii] != te_ref[jnp.maximum(ii - 1, 0)])


def _moe_up_kernel(te_ref, nt_ref, hs_ref, wg_ref, wu_ref, o_ref, wgb_ref, wub_ref):
    i = pl.program_id(1)

    @pl.when(i < nt_ref[0])
    def _():
        @pl.when(_expert_changed(i, te_ref, nt_ref))
        def _():
            wgb_ref[...] = wg_ref[...].astype(BF16)
            wub_ref[...] = wu_ref[...].astype(BF16)

        half = D_MODEL // 2
        w = hs_ref[...]
        lo = lax.bitcast_convert_type(lax.shift_left(w, jnp.uint32(16)), F32).astype(BF16)
        hi = lax.bitcast_convert_type(w & jnp.uint32(0xFFFF0000), F32).astype(BF16)
        g = (jnp.dot(lo, wgb_ref[:half, :], preferred_element_type=F32)
             + jnp.dot(hi, wgb_ref[half:, :], preferred_element_type=F32))
        u = (jnp.dot(lo, wub_ref[:half, :], preferred_element_type=F32)
             + jnp.dot(hi, wub_ref[half:, :], preferred_element_type=F32))
        o_ref[...] = (g * jax.nn.sigmoid(g) * u).astype(BF16)

    @pl.when(i >= nt_ref[0])
    def _():
        o_ref[...] = jnp.zeros(o_ref.shape, o_ref.dtype)


def _moe_up(tile_e, n_used, hs, wg, wu):
    n_rows = hs.shape[0]
    nt = n_rows // TM_MOE
    return pl.pallas_call(
        _moe_up_kernel,
        grid_spec=pltpu.PrefetchScalarGridSpec(
            num_scalar_prefetch=2,
            grid=(D_FF // TF_UP, nt),
            in_specs=[
                pl.BlockSpec((TM_MOE, D_MODEL // 2), lambda j, i, te, n: (_tile_of(i, n), 0)),
                pl.BlockSpec((None, D_MODEL, TF_UP), lambda j, i, te, n: (te[_tile_of(i, n)], 0, j)),
                pl.BlockSpec((None, D_MODEL, TF_UP), lambda j, i, te, n: (te[_tile_of(i, n)], 0, j)),
            ],
            out_specs=pl.BlockSpec((TM_MOE, TF_UP), lambda j, i, te, n: (i, j)),
            scratch_shapes=[pltpu.VMEM((D_MODEL, TF_UP), BF16), pltpu.VMEM((D_MODEL, TF_UP), BF16)],
        ),
        out_shape=jax.ShapeDtypeStruct((n_rows, D_FF), BF16),
        compiler_params=_cparams(("arbitrary", "arbitrary")),
        name="moe_up",
    )(tile_e, n_used, hs, wg, wu)


def _moe_down_kernel(te_ref, nt_ref, a_ref, wd_ref, o_ref, wdb_ref):
    i = pl.program_id(1)

    @pl.when(i < nt_ref[0])
    def _():
        @pl.when(_expert_changed(i, te_ref, nt_ref))
        def _():
            wdb_ref[...] = wd_ref[...].astype(BF16)

        o_ref[...] = jnp.dot(a_ref[...], wdb_ref[...], preferred_element_type=F32)

    @pl.when(i >= nt_ref[0])
    def _():
        o_ref[...] = jnp.zeros(o_ref.shape, o_ref.dtype)


def _moe_down(tile_e, n_used, act, wd):
    n_rows = act.shape[0]
    nt = n_rows // TM_MOE
    return pl.pallas_call(
        _moe_down_kernel,
        grid_spec=pltpu.PrefetchScalarGridSpec(
            num_scalar_prefetch=2,
            grid=(D_MODEL // TN_DOWN, nt),
            in_specs=[
                pl.BlockSpec((TM_MOE, D_FF), lambda j, i, te, n: (_tile_of(i, n), 0)),
                pl.BlockSpec((None, D_FF, TN_DOWN), lambda j, i, te, n: (te[_tile_of(i, n)], 0, j)),
            ],
            out_specs=pl.BlockSpec((TM_MOE, TN_DOWN), lambda j, i, te, n: (i, j)),
            scratch_shapes=[pltpu.VMEM((D_FF, TN_DOWN), BF16)],
        ),
        out_shape=jax.ShapeDtypeStruct((n_rows, D_MODEL), F32),
        compiler_params=_cparams(("arbitrary", "arbitrary")),
        name="moe_down",
    )(tile_e, n_used, act, wd)


def _combine_kernel(plo_ref, phi_ref, x_ref, wlo_ref, whi_ref, y_ref, o_ref, buf_ref, sem):
    t0 = pl.program_id(0) * TM

    def issue(r, carry):
        pltpu.make_async_copy(y_ref.at[pl.ds(plo_ref[t0 + r], 1)], buf_ref.at[0, pl.ds(r, 1)], sem).start()
        pltpu.make_async_copy(y_ref.at[pl.ds(phi_ref[t0 + r], 1)], buf_ref.at[1, pl.ds(r, 1)], sem).start()
        return carry

    lax.fori_loop(0, TM, issue, 0)

    def drain(r, carry):
        pltpu.make_async_copy(y_ref.at[pl.ds(0, 1)], buf_ref.at[0, pl.ds(0, 1)], sem).wait()
        return carry

    lax.fori_loop(0, 2 * TM, drain, 0)
    o_ref[...] = x_ref[...] + (wlo_ref[...] * buf_ref[0] + whi_ref[...] * buf_ref[1])


def _combine(pos_lo, pos_hi, x, w_lo, w_hi, ys):
    t = x.shape[0]
    return pl.pallas_call(
        _combine_kernel,
        grid_spec=pltpu.PrefetchScalarGridSpec(
            num_scalar_prefetch=2,
            grid=(t // TM,),
            in_specs=[
                pl.BlockSpec((TM, D_MODEL), lambda i, a, b: (i, 0)),
                pl.BlockSpec((TM, 1), lambda i, a, b: (i, 0)),
                pl.BlockSpec((TM, 1), lambda i, a, b: (i, 0)),
                pl.BlockSpec(memory_space=pl.ANY),
            ],
            out_specs=pl.BlockSpec((TM, D_MODEL), lambda i, a, b: (i, 0)),
            scratch_shapes=[pltpu.VMEM((2, TM, D_MODEL), F32), pltpu.SemaphoreType.DMA(())],
        ),
        out_shape=jax.ShapeDtypeStruct((t, D_MODEL), F32),
        compiler_params=_cparams(("arbitrary",)),
        name="moe_combine",
    )(pos_lo, pos_hi, x, w_lo, w_hi, ys)


def _moe(x, hpk, gates, rank, counts, e_gate, e_up, e_down):
    t = x.shape[0]
    n_tiles_max = (2 * t) // TM_MOE + N_EXPERTS
    n_rows = n_tiles_max * TM_MOE
    pos, wgt, meta = _plan(gates, rank, counts, n_tiles_max)
    pos_lo, pos_hi = pos[0], pos[1]
    tile_e, n_used = meta[0], meta[1, :1]
    cnt, base = meta[2, :N_EXPERTS], meta[3, :N_EXPERTS]
    hs = _dispatch(pos_lo, pos_hi, cnt, base, n_used, hpk, n_rows)
    act = _moe_up(tile_e, n_used, hs, e_gate, e_up)
    ys = _moe_down(tile_e, n_used, act, e_down)
    return _combine(pos_lo, pos_hi, x, wgt[0].reshape(t, 1), wgt[1].reshape(t, 1), ys)


def _rope_tables(positions):
    inv_freq = ROPE_THETA ** (-jnp.arange(0, ROT_DIM, 2, dtype=F32) / ROT_DIM)
    ang = positions.astype(F32).reshape(-1, 1) * inv_freq
    cos, sin = jnp.cos(ang), jnp.sin(ang)
    t = cos.shape[0]
    half = ROT_DIM // 2
    ones = jnp.ones((t, HEAD_DIM - ROT_DIM), F32)
    zeros = jnp.zeros((t, HEAD_DIM - ROT_DIM), F32)
    zh = jnp.zeros((t, half), F32)
    reps = V7X_LANES // HEAD_DIM
    cos_t = jnp.tile(jnp.concatenate([cos, cos, ones], axis=1), (1, reps))
    sin_a = jnp.tile(jnp.concatenate([zh, sin, zeros], axis=1), (1, reps))
    sin_b = jnp.tile(jnp.concatenate([-sin, zh, zeros], axis=1), (1, reps))
    return cos_t, sin_a, sin_b


def _dft_tables(seq):
    n2 = jnp.arange(FFT_N2, dtype=jnp.int32)
    m = (n2[:, None] * n2[None, :]) % FFT_N2
    ang = m.astype(F32) * (2.0 * np.pi / FFT_N2)
    c0, s0 = jnp.cos(ang).astype(BF16), jnp.sin(ang).astype(BF16)
    k1 = jnp.arange(FFT_N1, dtype=jnp.int32)
    beta = (k1[:, None] * n2[None, :]).astype(F32) * (2.0 * np.pi / seq)
    cb = jnp.broadcast_to(jnp.cos(beta)[:, :, None], (FFT_N1, FFT_N2, V7X_LANES))
    sb = jnp.broadcast_to(jnp.sin(beta)[:, :, None], (FFT_N1, FFT_N2, V7X_LANES))
    c = jnp.arange(FOURIER_GROUP, dtype=jnp.int32)
    angc = ((c[:, None] * c[None, :]) % FOURIER_GROUP).astype(F32) * (2.0 * np.pi / FOURIER_GROUP)
    return c0, s0, cb, sb, jnp.cos(angc), jnp.sin(angc)


def kernel(x, positions, norm_mix, w_in, q_norm, k_norm, sink, w_fmix, g_attn_out, g_fourier_out, w_out, norm_ffn,
           ffn_gate, ffn_up, ffn_down, w_router, e_gate, e_up, e_down):
    b, s, d = x.shape
    depth = w_in.shape[0]
    assert (d, s) == (D_MODEL, FFT_N1 * FFT_N2) and (b * s) % TM_DMA == 0
    t = b * s
    cos_t, sin_a, sin_b = _rope_tables(positions)
    c0, s0, cb, sb, cc, sc = _dft_tables(s)
    mst = _fmix_prep(w_fmix, cc, sc)
    head = jnp.arange(D_QK) // HEAD_DIM
    bd = (head[:, None] == head[None, :]).astype(BF16) * (1.0 / HEAD_DIM)
    row_i = jnp.arange(TM)
    ltri = (row_i[None, :] < row_i[:, None]).astype(BF16)
    xt = x.reshape(t, d)
    for l in range(depth):
        gain = jnp.concatenate([jnp.tile(q_norm[l], N_Q_HEADS) * (HEAD_DIM ** -0.5),
                                jnp.tile(k_norm[l], N_KV_HEADS)]).reshape(1, D_QK)
        q, k, v, f = _in_proj(xt, norm_mix[l].reshape(1, d), w_in[l], bd, gain, cos_t, sin_a, sin_b)
        a = _attention(q.reshape(b, s, D_ATTN), k.reshape(b, s, D_KV), v.reshape(b, s, D_KV), sink[l],
                       g_attn_out[l].reshape(1, D_ATTN))
        fo = _fourier(f.reshape(b, s, D_FOURIER), mst[l], c0, s0, cb, sb, g_fourier_out[l].reshape(1, D_FOURIER))
        a2, f2 = a.reshape(t, D_ATTN), fo.reshape(t, D_FOURIER)
        i = l // 2
        if l % 2 == 0:
            xt, h = _out_proj(a2, f2, xt, w_out[l], norm_ffn[l].reshape(1, d))
            xt = _ffn(xt, h, ffn_gate[i], ffn_up[i], ffn_down[i])
        else:
            wr = jnp.pad(w_router[i], ((0, 0), (0, V7X_LANES - N_EXPERTS)))
            xt, hpk, gates, rank, counts = _out_proj(a2, f2, xt, w_out[l], norm_ffn[l].reshape(1, d),
                                                     moe=(wr, ltri))
            xt = _moe(xt, hpk, gates, rank, counts, e_gate[i], e_up[i], e_down[i])
    return xt.reshape(b, s, d)
```

```python
import functools

import jax
import jax.numpy as jnp
import numpy as np
from jax import lax
from jax.experimental import pallas as pl
from jax.experimental.pallas import tpu as pltpu

F32 = jnp.float32
BF16 = jnp.bfloat16

D_MODEL = 1024
HEAD_DIM = 64
D_ATTN = 512
N_Q_HEADS = 8
N_KV_HEADS = 2
Q_PER_KV = 4
D_KV = 128
D_FOURIER = 512
N_FOURIER_GROUPS = 8
FOURIER_GROUP = 64
D_QK = D_ATTN + D_KV
D_IN = D_ATTN + 2 * D_KV + D_FOURIER
WINDOW = 128
ROPE_THETA = 500000.0
ROT_DIM = 16
D_FF = 3584
N_EXPERTS = 8
EPS = 1e-6

V7X_LANES = 128
V7X_SUBLANES = 8
V7X_MXU = 256
V7X_VMEM_LIMIT = 56 * 1024 * 1024
RT = D_MODEL // V7X_LANES
assert RT == V7X_SUBLANES

TM = 512
TQ = 256
KV_WIN = TQ + 2 * WINDOW
FFT_N1 = 4
FFT_N2 = 1024
FFT_CHUNK = 128
TM_FFN = 1024
TF_FFN = 512
TM_MOE = 512
TF_UP = D_FF // 2
TM_PLAN = 1024
DMA_UNROLL = 8


def _cparams(sem, vmem=V7X_VMEM_LIMIT):
    return pltpu.CompilerParams(dimension_semantics=sem, vmem_limit_bytes=vmem)


def _rms(xf, g):
    ms = jnp.mean(xf * xf, axis=-1, keepdims=True)
    return xf * lax.rsqrt(ms + EPS) * g


def _split_bf16(xf):
    hi = xf.astype(BF16)
    lo = (xf - hi.astype(F32)).astype(BF16)
    return hi, lo


def _layer_vec(n, l):
    return pl.BlockSpec((None, 1, n), lambda *_: (l, 0, 0))


def _in_proj_kernel(x_ref, g_ref, w_ref, bd_ref, gain_ref, cos_ref, sa_ref, sb_ref,
                    q_ref, k_ref, v_ref, f_ref, wbf_ref):
    @pl.when(pl.program_id(0) == 0)
    def _():
        wbf_ref[...] = w_ref[...].astype(BF16)

    h = _rms(x_ref[...], g_ref[...]).astype(BF16)
    z = jnp.dot(h, wbf_ref[...], preferred_element_type=F32)
    qk = z[:, :D_QK]
    hi, lo = _split_bf16(qk * qk)
    bd = bd_ref[...]
    ms = (jnp.dot(hi, bd, preferred_element_type=F32) + jnp.dot(lo, bd, preferred_element_type=F32))
    qkn = qk * lax.rsqrt(ms + EPS) * gain_ref[...]
    cos_t, sin_a, sin_b = cos_ref[...], sa_ref[...], sb_ref[...]
    for c in range(D_QK // V7X_LANES):
        blk = qkn[:, c * V7X_LANES:(c + 1) * V7X_LANES]
        rot = (blk * cos_t + pltpu.roll(blk, ROT_DIM // 2, 1) * sin_a
               + pltpu.roll(blk, V7X_LANES - ROT_DIM // 2, 1) * sin_b).astype(BF16)
        if c < D_ATTN // V7X_LANES:
            q_ref[:, c * V7X_LANES:(c + 1) * V7X_LANES] = rot
        else:
            k_ref[...] = rot
    v_ref[...] = z[:, D_QK:D_QK + D_KV].astype(BF16)
    f_ref[...] = z[:, D_QK + D_KV:].astype(BF16)


def _in_proj(l, x, g, w, bd, gain, cos_t, sin_a, sin_b):
    t = x.shape[0]
    row = lambda i: (i, 0)
    fixed = lambda i: (0, 0)
    return pl.pallas_call(
        _in_proj_kernel,
        grid=(t // TM,),
        in_specs=[
            pl.BlockSpec((TM, D_MODEL), row),
            _layer_vec(D_MODEL, l),
            pl.BlockSpec((None, D_MODEL, D_IN), lambda i: (l, 0, 0)),
            pl.BlockSpec((D_QK, D_QK), fixed),
            _layer_vec(D_QK, l),
            pl.BlockSpec((TM, V7X_LANES), row),
            pl.BlockSpec((TM, V7X_LANES), row),
            pl.BlockSpec((TM, V7X_LANES), row),
        ],
        out_specs=[
            pl.BlockSpec((TM, D_ATTN), row),
            pl.BlockSpec((TM, D_KV), row),
            pl.BlockSpec((TM, D_KV), row),
            pl.BlockSpec((TM, D_FOURIER), row),
        ],
        out_shape=[
            jax.ShapeDtypeStruct((t, D_ATTN), BF16),
            jax.ShapeDtypeStruct((t, D_KV), BF16),
            jax.ShapeDtypeStruct((t, D_KV), BF16),
            jax.ShapeDtypeStruct((t, D_FOURIER), BF16),
        ],
        scratch_shapes=[pltpu.VMEM((D_MODEL, D_IN), BF16)],
        compiler_params=_cparams(("arbitrary",)),
        name="in_proj",
    )(x, g, w, bd, gain, cos_t, sin_a, sin_b)


def _attn_kernel(sink_ref, q_ref, k_ref, v_ref, g_ref, o_ref, *, seq, layer):
    qs = pl.program_id(1) * TQ
    ws = pl.multiple_of(jnp.clip(qs - WINDOW, 0, seq - KV_WIN), WINDOW)
    kw = k_ref[pl.ds(ws, KV_WIN), :]
    vw = v_ref[pl.ds(ws, KV_WIN), :]
    q_abs = qs + lax.broadcasted_iota(jnp.int32, (TQ, KV_WIN), 0)
    k_abs = ws + lax.broadcasted_iota(jnp.int32, (TQ, KV_WIN), 1)
    mask = jnp.abs(q_abs - k_abs) <= WINDOW
    q = q_ref[...]
    outs = []
    for h in range(N_Q_HEADS):
        kvh = h // Q_PER_KV
        qh = q[:, h * HEAD_DIM:(h + 1) * HEAD_DIM]
        kh = kw[:, kvh * HEAD_DIM:(kvh + 1) * HEAD_DIM]
        vh = vw[:, kvh * HEAD_DIM:(kvh + 1) * HEAD_DIM]
        s = lax.dot_general(qh, kh, (((1,), (1,)), ((), ())), preferred_element_type=F32)
        s = jnp.where(mask, s, -jnp.inf)
        sink = sink_ref[layer, h]
        m = jnp.maximum(jnp.max(s, axis=-1, keepdims=True), sink)
        p = jnp.exp(s - m)
        denom = jnp.sum(p, axis=-1, keepdims=True) + jnp.exp(sink - m)
        o = jnp.dot(p.astype(BF16), vh, preferred_element_type=F32)
        outs.append(o / denom)
    a = jnp.concatenate(outs, axis=-1)
    o_ref[...] = _rms(a, g_ref[...]).astype(BF16)


def _attention(l, q, k, v, sink, g):
    b, s, _ = q.shape
    return pl.pallas_call(
        functools.partial(_attn_kernel, seq=s, layer=l),
        grid=(b, s // TQ),
        in_specs=[
            pl.BlockSpec(memory_space=pltpu.SMEM),
            pl.BlockSpec((None, TQ, D_ATTN), lambda i, j: (i, j, 0)),
            pl.BlockSpec((None, s, D_KV), lambda i, j: (i, 0, 0)),
            pl.BlockSpec((None, s, D_KV), lambda i, j: (i, 0, 0)),
            _layer_vec(D_ATTN, l),
        ],
        out_specs=pl.BlockSpec((None, TQ, D_ATTN), lambda i, j: (i, j, 0)),
        out_shape=jax.ShapeDtypeStruct((b, s, D_ATTN), BF16),
        compiler_params=_cparams(("arbitrary", "arbitrary")),
        name="band_attn",
    )(sink, q, k, v, g)


def _fmix_prep_kernel(w_ref, cc_ref, sc_ref, o_ref):
    o_ref[...] = jnp.zeros(o_ref.shape, o_ref.dtype)
    scale = 1.0 / 512.0
    for g in range(N_FOURIER_GROUPS):
        w = w_ref[g]
        mr = jnp.dot(cc_ref[...], w, preferred_element_type=F32, precision=lax.Precision.HIGHEST) * scale
        mi = jnp.dot(sc_ref[...], w, preferred_element_type=F32, precision=lax.Precision.HIGHEST) * (-scale)
        c = (g * FOURIER_GROUP) // FFT_CHUNK
        off = (g * FOURIER_GROUP) % FFT_CHUNK
        rows = slice(g * FOURIER_GROUP, (g + 1) * FOURIER_GROUP)
        o_ref[c, rows, off:off + FOURIER_GROUP] = mr.astype(BF16)
        o_ref[c, rows, FFT_CHUNK + off:FFT_CHUNK + off + FOURIER_GROUP] = mi.astype(BF16)


def _fmix_prep(w_fmix, cc, sc):
    depth = w_fmix.shape[0]
    n_chunk = D_FOURIER // FFT_CHUNK
    return pl.pallas_call(
        _fmix_prep_kernel,
        grid=(depth,),
        in_specs=[
            pl.BlockSpec((None, N_FOURIER_GROUPS, FOURIER_GROUP, FOURIER_GROUP), lambda l: (l, 0, 0, 0)),
            pl.BlockSpec((FOURIER_GROUP, FOURIER_GROUP), lambda l: (0, 0)),
            pl.BlockSpec((FOURIER_GROUP, FOURIER_GROUP), lambda l: (0, 0)),
        ],
        out_specs=pl.BlockSpec((None, n_chunk, D_FOURIER, 2 * FFT_CHUNK), lambda l: (l, 0, 0, 0)),
        out_shape=jax.ShapeDtypeStruct((depth, n_chunk, D_FOURIER, 2 * FFT_CHUNK), BF16),
        compiler_params=_cparams(("arbitrary",)),
        name="fmix_prep",
    )(w_fmix, cc, sc)


def _fourier_kernel(f_ref, m_ref, c0_ref, s0_ref, cb_ref, sb_ref, g_ref, o_ref, v_ref):
    k1 = pl.program_id(1)

    @pl.when(k1 == 0)
    def _():
        for c in range(D_FOURIER // FFT_CHUNK):
            mc = m_ref[c]
            z = [jnp.dot(f_ref[n1 * FFT_N2:(n1 + 1) * FFT_N2, :], mc, preferred_element_type=F32)
                 for n1 in range(FFT_N1)]
            zr = [t[:, :FFT_CHUNK] for t in z]
            zi = [t[:, FFT_CHUNK:] for t in z]
            ar, ai = zr[0] + zr[2], zi[0] + zi[2]
            br, bi = zr[1] + zr[3], zi[1] + zi[3]
            cr, ci = zr[0] - zr[2], zi[0] - zi[2]
            dr, di = zr[1] - zr[3], zi[1] - zi[3]
            u = [(ar + br, ai + bi), (cr + di, ci - dr), (ar - br, ai - bi), (cr - di, ci + dr)]
            for kk in range(FFT_N1):
                ur, ui = u[kk]
                cb, sb = cb_ref[kk], sb_ref[kk]
                v_ref[kk, :, c * FFT_CHUNK:(c + 1) * FFT_CHUNK] = (ur * cb + ui * sb).astype(BF16)
                v_ref[kk, :, D_FOURIER + c * FFT_CHUNK:D_FOURIER + (c + 1) * FFT_CHUNK] = (
                    ui * cb - ur * sb).astype(BF16)

    vr = v_ref[k1, :, :D_FOURIER]
    vi = v_ref[k1, :, D_FOURIER:]
    out = (jnp.dot(c0_ref[...], vr, preferred_element_type=F32)
           + jnp.dot(s0_ref[...], vi, preferred_element_type=F32))
    o_ref[...] = _rms(out, g_ref[...]).astype(BF16)


def _fourier(l, f, mst, c0, s0, cb, sb, g):
    b, s, _ = f.shape
    n_chunk = D_FOURIER // FFT_CHUNK
    out = pl.pallas_call(
        _fourier_kernel,
        grid=(b, FFT_N1),
        in_specs=[
            pl.BlockSpec((None, s, D_FOURIER), lambda i, j: (i, 0, 0)),
            pl.BlockSpec((None, n_chunk, D_FOURIER, 2 * FFT_CHUNK), lambda i, j: (l, 0, 0, 0)),
            pl.BlockSpec((FFT_N2, FFT_N2), lambda i, j: (0, 0)),
            pl.BlockSpec((FFT_N2, FFT_N2), lambda i, j: (0, 0)),
            pl.BlockSpec((FFT_N1, FFT_N2, V7X_LANES), lambda i, j: (0, 0, 0)),
            pl.BlockSpec((FFT_N1, FFT_N2, V7X_LANES), lambda i, j: (0, 0, 0)),
            _layer_vec(D_FOURIER, l),
        ],
        out_specs=pl.BlockSpec((None, FFT_N2, D_FOURIER), lambda i, j: (i, 0, j)),
        out_shape=jax.ShapeDtypeStruct((b, FFT_N2, FFT_N1 * D_FOURIER), BF16),
        scratch_shapes=[pltpu.VMEM((FFT_N1, FFT_N2, 2 * D_FOURIER), BF16)],
        compiler_params=_cparams(("arbitrary", "arbitrary")),
        name="fourier_mix",
    )(f, mst, c0, s0, cb, sb, g)
    return out.reshape(b, s, D_FOURIER)


def _out_proj_body(a_ref, f_ref, x_ref, w_ref, g_ref, wbf_ref):
    @pl.when(pl.program_id(0) == 0)
    def _():
        wbf_ref[...] = w_ref[...].astype(BF16)

    y = (jnp.dot(a_ref[...], wbf_ref[:D_ATTN, :], preferred_element_type=F32)
         + jnp.dot(f_ref[...], wbf_ref[D_ATTN:, :], preferred_element_type=F32))
    xn = x_ref[...] + y
    return xn, _rms(xn, g_ref[...])


def _out_proj_dense_kernel(a_ref, f_ref, x_ref, w_ref, g_ref, xo_ref, h_ref, wbf_ref):
    xn, h = _out_proj_body(a_ref, f_ref, x_ref, w_ref, g_ref, wbf_ref)
    xo_ref[...] = xn
    h_ref[...] = h.astype(BF16)


def _out_proj_moe_kernel(a_ref, f_ref, x_ref, w_ref, g_ref, wr_ref, lt_ref,
                         xo_ref, hrt_ref, gate_ref, rank_ref, cnt_ref, wbf_ref, wrh_ref, wrl_ref, carry_ref):
    @pl.when(pl.program_id(0) == 0)
    def _():
        hi, lo = _split_bf16(wr_ref[...])
        wrh_ref[...] = hi
        wrl_ref[...] = lo
        carry_ref[...] = jnp.zeros(carry_ref.shape, F32)

    xn, h = _out_proj_body(a_ref, f_ref, x_ref, w_ref, g_ref, wbf_ref)
    xo_ref[...] = xn
    for s in range(RT):
        hrt_ref[pl.ds(s, TM, stride=RT), :] = h[:, s * V7X_LANES:(s + 1) * V7X_LANES]
    h_hi, h_lo = _split_bf16(h)
    logits = (jnp.dot(h_hi, wrh_ref[...], preferred_element_type=F32)
              + jnp.dot(h_lo, wrh_ref[...], preferred_element_type=F32)
              + jnp.dot(h_hi, wrl_ref[...], preferred_element_type=F32))
    lane = lax.broadcasted_iota(jnp.int32, logits.shape, 1).astype(F32)
    logits = jnp.where(lane < N_EXPERTS, logits, -jnp.inf)
    m1 = jnp.max(logits, axis=-1, keepdims=True)
    i1 = jnp.min(jnp.where(logits == m1, lane, float(V7X_LANES)), axis=-1, keepdims=True)
    rest = jnp.where(lane == i1, -jnp.inf, logits)
    m2 = jnp.max(rest, axis=-1, keepdims=True)
    i2 = jnp.min(jnp.where(rest == m2, lane, float(V7X_LANES)), axis=-1, keepdims=True)
    e2 = jnp.exp(m2 - m1)
    den = 1.0 + e2
    sel1, sel2 = lane == i1, lane == i2
    sel = sel1 | sel2
    gate_ref[...] = jnp.where(sel1, 1.0 / den, jnp.where(sel2, e2 / den, 0.0))
    self32 = sel.astype(F32)
    rank = jnp.dot(lt_ref[...], self32.astype(BF16), preferred_element_type=F32) + carry_ref[...]
    rank_ref[...] = jnp.where(sel, rank, -1.0)
    carry_ref[...] = carry_ref[...] + jnp.sum(self32, axis=0, keepdims=True)
    cnt_ref[...] = carry_ref[...]


def _out_proj(l, a, f, x, w, g, moe=None):
    t = x.shape[0]
    row = lambda i: (i, 0)
    fixed = lambda i: (0, 0)
    in_specs = [
        pl.BlockSpec((TM, D_ATTN), row),
        pl.BlockSpec((TM, D_FOURIER), row),
        pl.BlockSpec((TM, D_MODEL), row),
        pl.BlockSpec((None, D_MODEL, D_MODEL), lambda i: (l, 0, 0)),
        _layer_vec(D_MODEL, l),
    ]
    if moe is None:
        return pl.pallas_call(
            _out_proj_dense_kernel,
            grid=(t // TM,),
            in_specs=in_specs,
            out_specs=[pl.BlockSpec((TM, D_MODEL), row), pl.BlockSpec((TM, D_MODEL), row)],
            out_shape=[jax.ShapeDtypeStruct((t, D_MODEL), F32), jax.ShapeDtypeStruct((t, D_MODEL), BF16)],
            scratch_shapes=[pltpu.VMEM((D_MODEL, D_MODEL), BF16)],
            compiler_params=_cparams(("arbitrary",)),
            name="out_proj",
        )(a, f, x, w, g)
    li, w_router_pad, ltri = moe
    return pl.pallas_call(
        _out_proj_moe_kernel,
        grid=(t // TM,),
        in_specs=in_specs + [pl.BlockSpec((None, D_MODEL, V7X_LANES), lambda i: (li, 0, 0)),
                             pl.BlockSpec((TM, TM), fixed)],
        out_specs=[
            pl.BlockSpec((TM, D_MODEL), row),
            pl.BlockSpec((TM * RT, V7X_LANES), row),
            pl.BlockSpec((TM, V7X_LANES), row),
            pl.BlockSpec((TM, V7X_LANES), row),
            pl.BlockSpec((1, V7X_LANES), fixed),
        ],
        out_shape=[
            jax.ShapeDtypeStruct((t, D_MODEL), F32),
            jax.ShapeDtypeStruct((t * RT, V7X_LANES), F32),
            jax.ShapeDtypeStruct((t, V7X_LANES), F32),
            jax.ShapeDtypeStruct((t, V7X_LANES), F32),
            jax.ShapeDtypeStruct((1, V7X_LANES), F32),
        ],
        scratch_shapes=[
            pltpu.VMEM((D_MODEL, D_MODEL), BF16),
            pltpu.VMEM((D_MODEL, V7X_LANES), BF16),
            pltpu.VMEM((D_MODEL, V7X_LANES), BF16),
            pltpu.VMEM((1, V7X_LANES), F32),
        ],
        compiler_params=_cparams(("arbitrary",)),
        name="out_proj_route",
    )(a, f, x, w, g, w_router_pad, ltri)


def _ffn_kernel(x_ref, h_ref, wg_ref, wu_ref, wd_ref, o_ref):
    @pl.when(pl.program_id(1) == 0)
    def _():
        o_ref[...] = x_ref[...]

    h = h_ref[...]
    g = jnp.dot(h, wg_ref[...].astype(BF16), preferred_element_type=F32)
    u = jnp.dot(h, wu_ref[...].astype(BF16), preferred_element_type=F32)
    act = (g * jax.nn.sigmoid(g) * u).astype(BF16)
    o_ref[...] += jnp.dot(act, wd_ref[...].astype(BF16), preferred_element_type=F32)


def _ffn(li, x, h, wg, wu, wd):
    t = x.shape[0]
    return pl.pallas_call(
        _ffn_kernel,
        grid=(t // TM_FFN, D_FF // TF_FFN),
        in_specs=[
            pl.BlockSpec((TM_FFN, D_MODEL), lambda i, j: (i, 0)),
            pl.BlockSpec((TM_FFN, D_MODEL), lambda i, j: (i, 0)),
            pl.BlockSpec((None, D_MODEL, TF_FFN), lambda i, j: (li, 0, j)),
            pl.BlockSpec((None, D_MODEL, TF_FFN), lambda i, j: (li, 0, j)),
            pl.BlockSpec((None, TF_FFN, D_MODEL), lambda i, j: (li, j, 0)),
        ],
        out_specs=pl.BlockSpec((TM_FFN, D_MODEL), lambda i, j: (i, 0)),
        out_shape=jax.ShapeDtypeStruct((t, D_MODEL), F32),
        compiler_params=_cparams(("arbitrary", "arbitrary")),
        name="ffn_dense",
    )(x, h, wg, wu, wd)


def _plan_kernel(gate_ref, rank_ref, cnt_ref, pos_ref, wgt_ref, meta_ref):
    lane = lax.broadcasted_iota(jnp.int32, (1, V7X_LANES), 1)
    cnt = cnt_ref[...]
    padded = jnp.floor((cnt + (TM_MOE - 1)) * (1.0 / TM_MOE)) * TM_MOE
    base = jnp.zeros_like(padded)
    for j in range(1, N_EXPERTS):
        base = base + jnp.where(lane >= j, pltpu.roll(padded, j, 1), 0.0)
    rank = rank_ref[...]
    pos_t = jnp.where(rank >= 0.0, base + rank, -1.0).T
    gate_t = gate_ref[...].T
    seen = jnp.zeros((1, TM_PLAN), F32)
    pos_lo = jnp.zeros((1, TM_PLAN), F32)
    pos_hi = jnp.zeros((1, TM_PLAN), F32)
    w_lo = jnp.zeros((1, TM_PLAN), F32)
    w_hi = jnp.zeros((1, TM_PLAN), F32)
    for e in range(N_EXPERTS):
        p = pos_t[e:e + 1, :]
        gt = gate_t[e:e + 1, :]
        chosen = p >= 0.0
        first = chosen & (seen == 0.0)
        second = chosen & (seen == 1.0)
        pos_lo = jnp.where(first, p, pos_lo)
        pos_hi = jnp.where(second, p, pos_hi)
        w_lo = jnp.where(first, gt, w_lo)
        w_hi = jnp.where(second, gt, w_hi)
        seen = seen + chosen.astype(F32)
    pos_ref[0:1, :] = pos_lo.astype(jnp.int32)
    pos_ref[1:2, :] = pos_hi.astype(jnp.int32)
    wgt_ref[0:1, :] = w_lo
    wgt_ref[1:2, :] = w_hi
    ends = base + padded
    tile_start = (lane * TM_MOE).astype(F32)
    tile_e = jnp.zeros((1, V7X_LANES), F32)
    for e in range(N_EXPERTS):
        tile_e = tile_e + (ends[:, e:e + 1] <= tile_start).astype(F32)
    tile_e = jnp.minimum(tile_e, float(N_EXPERTS - 1))
    n_used = ends[:, N_EXPERTS - 1:N_EXPERTS] * (1.0 / TM_MOE) + jnp.zeros((1, V7X_LANES), F32)
    meta_ref[0:1, :] = tile_e.astype(jnp.int32)
    meta_ref[1:2, :] = n_used.astype(jnp.int32)
    meta_ref[2:3, :] = cnt.astype(jnp.int32)
    meta_ref[3:4, :] = base.astype(jnp.int32)
    meta_ref[4:8, :] = jnp.zeros((4, V7X_LANES), jnp.int32)


def _plan(gates, rank, counts):
    t = gates.shape[0]
    return pl.pallas_call(
        _plan_kernel,
        grid=(t // TM_PLAN,),
        in_specs=[
            pl.BlockSpec((TM_PLAN, V7X_LANES), lambda i: (i, 0)),
            pl.BlockSpec((TM_PLAN, V7X_LANES), lambda i: (i, 0)),
            pl.BlockSpec((1, V7X_LANES), lambda i: (0, 0)),
        ],
        out_specs=[
            pl.BlockSpec((2, TM_PLAN), lambda i: (0, i)),
            pl.BlockSpec((2, TM_PLAN), lambda i: (0, i)),
            pl.BlockSpec((8, V7X_LANES), lambda i: (0, 0)),
        ],
        out_shape=[
            jax.ShapeDtypeStruct((2, t), jnp.int32),
            jax.ShapeDtypeStruct((2, t), F32),
            jax.ShapeDtypeStruct((8, V7X_LANES), jnp.int32),
        ],
        compiler_params=_cparams(("arbitrary",)),
        name="moe_plan",
    )(gates, rank, counts)


def _rt_rows(ref, row, n=1):
    start = row * RT if isinstance(row, int) else pl.multiple_of(row * RT, RT)
    return ref.at[pl.ds(start, n * RT)]


def _dispatch_kernel(plo_ref, phi_ref, cnt_ref, base_ref, nt_ref, h_ref, z_ref, o_ref, sem):
    i = pl.program_id(0)
    t0 = i * TM

    def issue(r, carry):
        pltpu.make_async_copy(_rt_rows(h_ref, r), _rt_rows(o_ref, plo_ref[t0 + r]), sem).start()
        pltpu.make_async_copy(_rt_rows(h_ref, r), _rt_rows(o_ref, phi_ref[t0 + r]), sem).start()
        return carry

    lax.fori_loop(0, TM, issue, 0, unroll=DMA_UNROLL)
    for _ in range(2):
        pltpu.make_async_copy(h_ref, _rt_rows(o_ref, 0, TM), sem).wait()

    @pl.when(i == pl.num_programs(0) - 1)
    def _():
        for e in range(N_EXPERTS):
            n = cnt_ref[e]
            n_pad = ((n + (TM_MOE - 1)) // TM_MOE) * TM_MOE - n
            first = base_ref[e] + n

            def pad_issue(r, carry, first=first):
                pltpu.make_async_copy(_rt_rows(z_ref, 0), _rt_rows(o_ref, first + r), sem).start()
                return carry

            def pad_drain(r, carry):
                pltpu.make_async_copy(_rt_rows(z_ref, 0), _rt_rows(o_ref, 0), sem).wait()
                return carry

            lax.fori_loop(0, n_pad, pad_issue, 0)
            lax.fori_loop(0, n_pad, pad_drain, 0)

        def tail(tile, carry):
            cp = pltpu.make_async_copy(z_ref, _rt_rows(o_ref, tile * TM_MOE, TM_MOE), sem)
            cp.start()
            cp.wait()
            return carry

        lax.fori_loop(nt_ref[0], o_ref.shape[0] // (TM_MOE * RT), tail, 0)


def _dispatch(pos_lo, pos_hi, cnt, base, n_used, h_rt, n_rows):
    t = h_rt.shape[0] // RT
    zeros = jnp.zeros((TM_MOE * RT, V7X_LANES), F32)
    return pl.pallas_call(
        _dispatch_kernel,
        grid_spec=pltpu.PrefetchScalarGridSpec(
            num_scalar_prefetch=5,
            grid=(t // TM,),
            in_specs=[pl.BlockSpec((TM * RT, V7X_LANES), lambda i, *_: (i, 0)),
                      pl.BlockSpec(memory_space=pl.ANY)],
            out_specs=pl.BlockSpec(memory_space=pl.ANY),
            scratch_shapes=[pltpu.SemaphoreType.DMA(())],
        ),
        out_shape=jax.ShapeDtypeStruct((n_rows * RT, V7X_LANES), F32),
        compiler_params=_cparams(("arbitrary",)),
        name="moe_dispatch",
    )(pos_lo, pos_hi, cnt, base, n_used, h_rt, zeros)


def _tile_of(i, nt_ref):
    return jnp.minimum(i, nt_ref[0] - 1)


def _moe_up_kernel(te_ref, nt_ref, hs_ref, wg_ref, wu_ref, o_ref):
    i = pl.program_id(1)

    @pl.when(i < nt_ref[0])
    def _():
        x = jnp.concatenate([hs_ref[pl.ds(s, TM_MOE, stride=RT), :].astype(BF16) for s in range(RT)], axis=1)
        for c in range(TF_UP // V7X_MXU):
            cols = slice(c * V7X_MXU, (c + 1) * V7X_MXU)
            g = jnp.dot(x, wg_ref[:, cols].astype(BF16), preferred_element_type=F32)
            u = jnp.dot(x, wu_ref[:, cols].astype(BF16), preferred_element_type=F32)
            o_ref[:, cols] = (g * jax.nn.sigmoid(g) * u).astype(BF16)

    @pl.when(i >= nt_ref[0])
    def _():
        o_ref[...] = jnp.zeros(o_ref.shape, o_ref.dtype)


def _moe_up(li, tile_e, n_used, hs, wg, wu):
    n_rows = hs.shape[0] // RT
    nt = n_rows // TM_MOE
    wspec = pl.BlockSpec((None, None, D_MODEL, TF_UP), lambda j, i, te, n: (li, te[_tile_of(i, n)], 0, j))
    return pl.pallas_call(
        _moe_up_kernel,
        grid_spec=pltpu.PrefetchScalarGridSpec(
            num_scalar_prefetch=2,
            grid=(D_FF // TF_UP, nt),
            in_specs=[pl.BlockSpec((TM_MOE * RT, V7X_LANES), lambda j, i, te, n: (_tile_of(i, n), 0)), wspec, wspec],
            out_specs=pl.BlockSpec((TM_MOE, TF_UP), lambda j, i, te, n: (i, j)),
        ),
        out_shape=jax.ShapeDtypeStruct((n_rows, D_FF), BF16),
        compiler_params=_cparams(("arbitrary", "arbitrary")),
        name="moe_up",
    )(tile_e, n_used, hs, wg, wu)


def _moe_down_kernel(te_ref, nt_ref, a_ref, wd_ref, o_ref):
    i = pl.program_id(0)

    @pl.when(i < nt_ref[0])
    def _():
        a = a_ref[...]
        for c in range(D_MODEL // V7X_MXU):
            y = jnp.dot(a, wd_ref[:, c * V7X_MXU:(c + 1) * V7X_MXU].astype(BF16), preferred_element_type=F32)
            for k in range(V7X_MXU // V7X_LANES):
                s = c * (V7X_MXU // V7X_LANES) + k
                o_ref[pl.ds(s, TM_MOE, stride=RT), :] = y[:, k * V7X_LANES:(k + 1) * V7X_LANES]

    @pl.when(i >= nt_ref[0])
    def _():
        o_ref[...] = jnp.zeros(o_ref.shape, o_ref.dtype)


def _moe_down(li, tile_e, n_used, act, wd):
    n_rows = act.shape[0]
    nt = n_rows // TM_MOE
    return pl.pallas_call(
        _moe_down_kernel,
        grid_spec=pltpu.PrefetchScalarGridSpec(
            num_scalar_prefetch=2,
            grid=(nt,),
            in_specs=[
                pl.BlockSpec((TM_MOE, D_FF), lambda i, te, n: (_tile_of(i, n), 0)),
                pl.BlockSpec((None, None, D_FF, D_MODEL), lambda i, te, n: (li, te[_tile_of(i, n)], 0, 0)),
            ],
            out_specs=pl.BlockSpec((TM_MOE * RT, V7X_LANES), lambda i, te, n: (i, 0)),
        ),
        out_shape=jax.ShapeDtypeStruct((n_rows * RT, V7X_LANES), F32),
        compiler_params=_cparams(("arbitrary",)),
        name="moe_down",
    )(tile_e, n_used, act, wd)


def _combine_kernel(plo_ref, phi_ref, x_ref, wlo_ref, whi_ref, y_ref, o_ref, lo_ref, hi_ref, sem):
    t0 = pl.program_id(0) * TM

    def issue(r, carry):
        pltpu.make_async_copy(_rt_rows(y_ref, plo_ref[t0 + r]), _rt_rows(lo_ref, r), sem).start()
        pltpu.make_async_copy(_rt_rows(y_ref, phi_ref[t0 + r]), _rt_rows(hi_ref, r), sem).start()
        return carry

    lax.fori_loop(0, TM, issue, 0, unroll=DMA_UNROLL)
    pltpu.make_async_copy(_rt_rows(y_ref, 0, TM), lo_ref, sem).wait()
    pltpu.make_async_copy(_rt_rows(y_ref, 0, TM), hi_ref, sem).wait()
    w_lo, w_hi = wlo_ref[...], whi_ref[...]
    for s in range(RT):
        cols = slice(s * V7X_LANES, (s + 1) * V7X_LANES)
        o_ref[:, cols] = x_ref[:, cols] + (w_lo * lo_ref[pl.ds(s, TM, stride=RT), :]
                                           + w_hi * hi_ref[pl.ds(s, TM, stride=RT), :])


def _combine(pos_lo, pos_hi, x, w_lo, w_hi, ys):
    t = x.shape[0]
    return pl.pallas_call(
        _combine_kernel,
        grid_spec=pltpu.PrefetchScalarGridSpec(
            num_scalar_prefetch=2,
            grid=(t // TM,),
            in_specs=[
                pl.BlockSpec((TM, D_MODEL), lambda i, a, b: (i, 0)),
                pl.BlockSpec((TM, 1), lambda i, a, b: (i, 0)),
                pl.BlockSpec((TM, 1), lambda i, a, b: (i, 0)),
                pl.BlockSpec(memory_space=pl.ANY),
            ],
            out_specs=pl.BlockSpec((TM, D_MODEL), lambda i, a, b: (i, 0)),
            scratch_shapes=[pltpu.VMEM((TM * RT, V7X_LANES), F32), pltpu.VMEM((TM * RT, V7X_LANES), F32),
                            pltpu.SemaphoreType.DMA(())],
        ),
        out_shape=jax.ShapeDtypeStruct((t, D_MODEL), F32),
        compiler_params=_cparams(("arbitrary",)),
        name="moe_combine",
    )(pos_lo, pos_hi, x, w_lo, w_hi, ys)


def _moe(li, x, h_rt, gates, rank, counts, e_gate, e_up, e_down):
    t = x.shape[0]
    n_rows = ((2 * t) // TM_MOE + N_EXPERTS) * TM_MOE
    pos, wgt, meta = _plan(gates, rank, counts)
    pos_lo, pos_hi = pos[0], pos[1]
    tile_e, n_used = meta[0], meta[1, :1]
    cnt, base = meta[2, :N_EXPERTS], meta[3, :N_EXPERTS]
    hs = _dispatch(pos_lo, pos_hi, cnt, base, n_used, h_rt, n_rows)
    act = _moe_up(li, tile_e, n_used, hs, e_gate, e_up)
    ys = _moe_down(li, tile_e, n_used, act, e_down)
    return _combine(pos_lo, pos_hi, x, wgt[0].reshape(t, 1), wgt[1].reshape(t, 1), ys)


def _rope_tables(positions):
    inv_freq = ROPE_THETA ** (-jnp.arange(0, ROT_DIM, 2, dtype=F32) / ROT_DIM)
    ang = positions.astype(F32).reshape(-1, 1) * inv_freq
    cos, sin = jnp.cos(ang), jnp.sin(ang)
    t = cos.shape[0]
    half = ROT_DIM // 2
    ones = jnp.ones((t, HEAD_DIM - ROT_DIM), F32)
    zeros = jnp.zeros((t, HEAD_DIM - ROT_DIM), F32)
    zh = jnp.zeros((t, half), F32)
    reps = V7X_LANES // HEAD_DIM
    cos_t = jnp.tile(jnp.concatenate([cos, cos, ones], axis=1), (1, reps))
    sin_a = jnp.tile(jnp.concatenate([zh, sin, zeros], axis=1), (1, reps))
    sin_b = jnp.tile(jnp.concatenate([-sin, zh, zeros], axis=1), (1, reps))
    return cos_t, sin_a, sin_b


def _dft_tables(seq):
    n2 = jnp.arange(FFT_N2, dtype=jnp.int32)
    m = (n2[:, None] * n2[None, :]) % FFT_N2
    ang = m.astype(F32) * (2.0 * np.pi / FFT_N2)
    c0, s0 = jnp.cos(ang).astype(BF16), jnp.sin(ang).astype(BF16)
    k1 = jnp.arange(FFT_N1, dtype=jnp.int32)
    beta = (k1[:, None] * n2[None, :]).astype(F32) * (2.0 * np.pi / seq)
    cb = jnp.broadcast_to(jnp.cos(beta)[:, :, None], (FFT_N1, FFT_N2, V7X_LANES))
    sb = jnp.broadcast_to(jnp.sin(beta)[:, :, None], (FFT_N1, FFT_N2, V7X_LANES))
    c = jnp.arange(FOURIER_GROUP, dtype=jnp.int32)
    angc = ((c[:, None] * c[None, :]) % FOURIER_GROUP).astype(F32) * (2.0 * np.pi / FOURIER_GROUP)
    return c0, s0, cb, sb, jnp.cos(angc), jnp.sin(angc)


def kernel(x, positions, norm_mix, w_in, q_norm, k_norm, sink, w_fmix, g_attn_out, g_fourier_out, w_out, norm_ffn,
           ffn_gate, ffn_up, ffn_down, w_router, e_gate, e_up, e_down):
    b, s, d = x.shape
    depth = w_in.shape[0]
    assert (d, s) == (D_MODEL, FFT_N1 * FFT_N2) and (b * s) % TM_PLAN == 0
    t = b * s
    cos_t, sin_a, sin_b = _rope_tables(positions)
    c0, s0, cb, sb, cc, sc = _dft_tables(s)
    mst = _fmix_prep(w_fmix, cc, sc)
    head = jnp.arange(D_QK) // HEAD_DIM
    bd = (head[:, None] == head[None, :]).astype(BF16) * (1.0 / HEAD_DIM)
    row_i = jnp.arange(TM)
    ltri = (row_i[None, :] < row_i[:, None]).astype(BF16)
    gain = jnp.concatenate([jnp.tile(q_norm, (1, N_Q_HEADS)) * (HEAD_DIM ** -0.5),
                            jnp.tile(k_norm, (1, N_KV_HEADS))], axis=1).reshape(depth, 1, D_QK)
    vec = lambda p: p.reshape(depth, 1, p.shape[-1])
    norm_mix, norm_ffn, g_attn_out, g_fourier_out = vec(norm_mix), vec(norm_ffn), vec(g_attn_out), vec(g_fourier_out)
    wr = jnp.pad(w_router, ((0, 0), (0, 0), (0, V7X_LANES - N_EXPERTS)))
    xt = x.reshape(t, d)
    for l in range(depth):
        q, k, v, f = _in_proj(l, xt, norm_mix, w_in, bd, gain, cos_t, sin_a, sin_b)
        a = _attention(l, q.reshape(b, s, D_ATTN), k.reshape(b, s, D_KV), v.reshape(b, s, D_KV), sink, g_attn_out)
        fo = _fourier(l, f.reshape(b, s, D_FOURIER), mst, c0, s0, cb, sb, g_fourier_out)
        a2, f2 = a.reshape(t, D_ATTN), fo.reshape(t, D_FOURIER)
        li = l // 2
        if l % 2 == 0:
            xt, h = _out_proj(l, a2, f2, xt, w_out, norm_ffn)
            xt = _ffn(li, xt, h, ffn_gate, ffn_up, ffn_down)
        else:
            xt, h_rt, gates, rank, counts = _out_proj(l, a2, f2, xt, w_out, norm_ffn, moe=(li, wr, ltri))
            xt = _moe(li, xt, h_rt, gates, rank, counts, e_gate, e_up, e_down)
    return xt.reshape(b, s, d)
```

```python
import functools

import jax
import jax.numpy as jnp
import numpy as np
from jax import lax
from jax.experimental import pallas as pl
from jax.experimental.pallas import tpu as pltpu

F32 = jnp.float32
BF16 = jnp.bfloat16

D_MODEL = 1024
HEAD_DIM = 64
D_ATTN = 512
N_Q_HEADS = 8
N_KV_HEADS = 2
Q_PER_KV = 4
D_KV = 128
D_FOURIER = 512
N_FOURIER_GROUPS = 8
FOURIER_GROUP = 64
D_QK = D_ATTN + D_KV
D_IN = D_ATTN + 2 * D_KV + D_FOURIER
WINDOW = 128
ROPE_THETA = 500000.0
ROT_DIM = 16
D_FF = 3584
N_EXPERTS = 8
EPS = 1e-6

V7X_LANES = 128
V7X_SUBLANES = 8
V7X_MXU = 256
V7X_VMEM_LIMIT = 56 * 1024 * 1024
RT = D_MODEL // V7X_LANES
assert RT == V7X_SUBLANES

TM = 512
TQ = 512
QB = 128
KV_WIN = QB + 2 * WINDOW
FFT_N1 = 4
FFT_N2 = 1024
FFT_CHUNK = 128
TM_FFN = 1024
TF_FFN = 512
TM_MOE = 512
TF_UP = D_FF // 2
TM_PLAN = 1024
DMA_UNROLL = 8


def _cparams(sem, vmem=V7X_VMEM_LIMIT):
    return pltpu.CompilerParams(dimension_semantics=sem, vmem_limit_bytes=vmem)


def _rms(xf, g):
    ms = jnp.mean(xf * xf, axis=-1, keepdims=True)
    return xf * lax.rsqrt(ms + EPS) * g


def _split_bf16(xf):
    hi = xf.astype(BF16)
    lo = (xf - hi.astype(F32)).astype(BF16)
    return hi, lo


def _layer_vec(n, l):
    return pl.BlockSpec((None, 1, n), lambda *_: (l, 0, 0))


def _in_proj_kernel(x_ref, g_ref, w_ref, hs_ref, he_ref, gain_ref, cos_ref, sa_ref, sb_ref,
                    q_ref, k_ref, v_ref, f_ref, wbf_ref):
    @pl.when(pl.program_id(0) == 0)
    def _():
        wbf_ref[...] = w_ref[...].astype(BF16)

    h = _rms(x_ref[...], g_ref[...]).astype(BF16)
    z = jnp.dot(h, wbf_ref[...], preferred_element_type=F32)
    qk = z[:, :D_QK]
    hi, lo = _split_bf16(qk * qk)
    ms = (jnp.dot(hi, hs_ref[...], preferred_element_type=F32)
          + jnp.dot(lo, hs_ref[...], preferred_element_type=F32))
    r_hi, r_lo = _split_bf16(lax.rsqrt(ms + EPS))
    inv_rms = (jnp.dot(r_hi, he_ref[...], preferred_element_type=F32)
               + jnp.dot(r_lo, he_ref[...], preferred_element_type=F32))
    qkn = qk * inv_rms * gain_ref[...]
    cos_t, sin_a, sin_b = cos_ref[...], sa_ref[...], sb_ref[...]
    for c in range(D_QK // V7X_LANES):
        blk = qkn[:, c * V7X_LANES:(c + 1) * V7X_LANES]
        rot = (blk * cos_t + pltpu.roll(blk, ROT_DIM // 2, 1) * sin_a
               + pltpu.roll(blk, V7X_LANES - ROT_DIM // 2, 1) * sin_b).astype(BF16)
        if c < D_ATTN // V7X_LANES:
            q_ref[:, c * V7X_LANES:(c + 1) * V7X_LANES] = rot
        else:
            k_ref[...] = rot
    v_ref[...] = z[:, D_QK:D_QK + D_KV].astype(BF16)
    f_ref[...] = z[:, D_QK + D_KV:].astype(BF16)


def _in_proj(l, x, g, w, head_sum, head_expand, gain, cos_t, sin_a, sin_b):
    t = x.shape[0]
    row = lambda i: (i, 0)
    fixed = lambda i: (0, 0)
    return pl.pallas_call(
        _in_proj_kernel,
        grid=(t // TM,),
        in_specs=[
            pl.BlockSpec((TM, D_MODEL), row),
            _layer_vec(D_MODEL, l),
            pl.BlockSpec((None, D_MODEL, D_IN), lambda i: (l, 0, 0)),
            pl.BlockSpec((D_QK, V7X_LANES), fixed),
            pl.BlockSpec((V7X_LANES, D_QK), fixed),
            _layer_vec(D_QK, l),
            pl.BlockSpec((TM, V7X_LANES), row),
            pl.BlockSpec((TM, V7X_LANES), row),
            pl.BlockSpec((TM, V7X_LANES), row),
        ],
        out_specs=[
            pl.BlockSpec((TM, D_ATTN), row),
            pl.BlockSpec((TM, D_KV), row),
            pl.BlockSpec((TM, D_KV), row),
            pl.BlockSpec((TM, D_FOURIER), row),
        ],
        out_shape=[
            jax.ShapeDtypeStruct((t, D_ATTN), BF16),
            jax.ShapeDtypeStruct((t, D_KV), BF16),
            jax.ShapeDtypeStruct((t, D_KV), BF16),
            jax.ShapeDtypeStruct((t, D_FOURIER), BF16),
        ],
        scratch_shapes=[pltpu.VMEM((D_MODEL, D_IN), BF16)],
        compiler_params=_cparams(("arbitrary",)),
        name="in_proj",
    )(x, g, w, head_sum, head_expand, gain, cos_t, sin_a, sin_b)


def _attn_kernel(sink_ref, q_ref, k_ref, v_ref, g_ref, o_ref, s_ref, p_ref, *, seq, layer):
    q0 = pl.program_id(1) * TQ
    gain = g_ref[...]
    heads = range(N_Q_HEADS)

    def block(j, carry):
        r0 = pl.multiple_of(j * QB, QB)
        qs = q0 + r0
        ws = pl.multiple_of(jnp.clip(qs - WINDOW, 0, seq - KV_WIN), QB)
        kw = k_ref[pl.ds(ws, KV_WIN), :]
        vw = v_ref[pl.ds(ws, KV_WIN), :]
        q_abs = qs + lax.broadcasted_iota(jnp.int32, (QB, KV_WIN), 0)
        k_abs = ws + lax.broadcasted_iota(jnp.int32, (QB, KV_WIN), 1)
        mask = jnp.abs(q_abs - k_abs) <= WINDOW
        q = q_ref[pl.ds(r0, QB), :]
        for h in heads:
            kvh = h // Q_PER_KV
            qh = q[:, h * HEAD_DIM:(h + 1) * HEAD_DIM]
            kh = kw[:, kvh * HEAD_DIM:(kvh + 1) * HEAD_DIM]
            s = lax.dot_general(qh, kh, (((1,), (1,)), ((), ())), preferred_element_type=F32)
            s_ref[h] = jnp.where(mask, s, -jnp.inf)
        m = [jnp.maximum(jnp.max(s_ref[h], axis=-1, keepdims=True), sink_ref[layer, h]) for h in heads]
        inv = []
        for h in heads:
            p = jnp.exp(s_ref[h] - m[h])
            inv.append(1.0 / (jnp.sum(p, axis=-1, keepdims=True) + jnp.exp(sink_ref[layer, h] - m[h])))
            p_ref[h] = p.astype(BF16)
        outs = []
        for h in heads:
            kvh = h // Q_PER_KV
            vh = vw[:, kvh * HEAD_DIM:(kvh + 1) * HEAD_DIM]
            outs.append(jnp.dot(p_ref[h], vh, preferred_element_type=F32) * inv[h])
        a = jnp.concatenate(outs, axis=-1)
        o_ref[pl.ds(r0, QB), :] = _rms(a, gain).astype(BF16)
        return carry

    lax.fori_loop(0, TQ // QB, block, 0)


def _attention(l, q, k, v, sink, g):
    b, s, _ = q.shape
    return pl.pallas_call(
        functools.partial(_attn_kernel, seq=s, layer=l),
        grid=(b, s // TQ),
        in_specs=[
            pl.BlockSpec(memory_space=pltpu.SMEM),
            pl.BlockSpec((None, TQ, D_ATTN), lambda i, j: (i, j, 0)),
            pl.BlockSpec((None, s, D_KV), lambda i, j: (i, 0, 0)),
            pl.BlockSpec((None, s, D_KV), lambda i, j: (i, 0, 0)),
            _layer_vec(D_ATTN, l),
        ],
        out_specs=pl.BlockSpec((None, TQ, D_ATTN), lambda i, j: (i, j, 0)),
        out_shape=jax.ShapeDtypeStruct((b, s, D_ATTN), BF16),
        scratch_shapes=[pltpu.VMEM((N_Q_HEADS, QB, KV_WIN), F32), pltpu.VMEM((N_Q_HEADS, QB, KV_WIN), BF16)],
        compiler_params=_cparams(("arbitrary", "arbitrary")),
        name="band_attn",
    )(sink, q, k, v, g)


def _fmix_prep_kernel(w_ref, cc_ref, sc_ref, o_ref):
    o_ref[...] = jnp.zeros(o_ref.shape, o_ref.dtype)
    scale = 1.0 / 512.0
    for g in range(N_FOURIER_GROUPS):
        w = w_ref[g]
        mr = jnp.dot(cc_ref[...], w, preferred_element_type=F32, precision=lax.Precision.HIGHEST) * scale
        mi = jnp.dot(sc_ref[...], w, preferred_element_type=F32, precision=lax.Precision.HIGHEST) * (-scale)
        c = (g * FOURIER_GROUP) // FFT_CHUNK
        off = (g * FOURIER_GROUP) % FFT_CHUNK
        rows = slice(g * FOURIER_GROUP, (g + 1) * FOURIER_GROUP)
        o_ref[c, rows, off:off + FOURIER_GROUP] = mr.astype(BF16)
        o_ref[c, rows, FFT_CHUNK + off:FFT_CHUNK + off + FOURIER_GROUP] = mi.astype(BF16)


def _fmix_prep(w_fmix, cc, sc):
    depth = w_fmix.shape[0]
    n_chunk = D_FOURIER // FFT_CHUNK
    return pl.pallas_call(
        _fmix_prep_kernel,
        grid=(depth,),
        in_specs=[
            pl.BlockSpec((None, N_FOURIER_GROUPS, FOURIER_GROUP, FOURIER_GROUP), lambda l: (l, 0, 0, 0)),
            pl.BlockSpec((FOURIER_GROUP, FOURIER_GROUP), lambda l: (0, 0)),
            pl.BlockSpec((FOURIER_GROUP, FOURIER_GROUP), lambda l: (0, 0)),
        ],
        out_specs=pl.BlockSpec((None, n_chunk, D_FOURIER, 2 * FFT_CHUNK), lambda l: (l, 0, 0, 0)),
        out_shape=jax.ShapeDtypeStruct((depth, n_chunk, D_FOURIER, 2 * FFT_CHUNK), BF16),
        compiler_params=_cparams(("arbitrary",)),
        name="fmix_prep",
    )(w_fmix, cc, sc)


def _fourier_kernel(f_ref, m_ref, c0_ref, s0_ref, cb_ref, sb_ref, g_ref, o_ref, v_ref):
    k1 = pl.program_id(1)

    @pl.when(k1 == 0)
    def _():
        for c in range(D_FOURIER // FFT_CHUNK):
            mc = m_ref[c]
            z = [jnp.dot(f_ref[n1 * FFT_N2:(n1 + 1) * FFT_N2, :], mc, preferred_element_type=F32)
                 for n1 in range(FFT_N1)]
            zr = [t[:, :FFT_CHUNK] for t in z]
            zi = [t[:, FFT_CHUNK:] for t in z]
            ar, ai = zr[0] + zr[2], zi[0] + zi[2]
            br, bi = zr[1] + zr[3], zi[1] + zi[3]
            cr, ci = zr[0] - zr[2], zi[0] - zi[2]
            dr, di = zr[1] - zr[3], zi[1] - zi[3]
            u = [(ar + br, ai + bi), (cr + di, ci - dr), (ar - br, ai - bi), (cr - di, ci + dr)]
            for kk in range(FFT_N1):
                ur, ui = u[kk]
                cb, sb = cb_ref[kk], sb_ref[kk]
                v_ref[kk, :, c * FFT_CHUNK:(c + 1) * FFT_CHUNK] = (ur * cb + ui * sb).astype(BF16)
                v_ref[kk, :, D_FOURIER + c * FFT_CHUNK:D_FOURIER + (c + 1) * FFT_CHUNK] = (
                    ui * cb - ur * sb).astype(BF16)

    vr = v_ref[k1, :, :D_FOURIER]
    vi = v_ref[k1, :, D_FOURIER:]
    out = (jnp.dot(c0_ref[...], vr, preferred_element_type=F32)
           + jnp.dot(s0_ref[...], vi, preferred_element_type=F32))
    o_ref[...] = _rms(out, g_ref[...]).astype(BF16)


def _fourier(l, f, mst, c0, s0, cb, sb, g):
    b, s, _ = f.shape
    n_chunk = D_FOURIER // FFT_CHUNK
    out = pl.pallas_call(
        _fourier_kernel,
        grid=(b, FFT_N1),
        in_specs=[
            pl.BlockSpec((None, s, D_FOURIER), lambda i, j: (i, 0, 0)),
            pl.BlockSpec((None, n_chunk, D_FOURIER, 2 * FFT_CHUNK), lambda i, j: (l, 0, 0, 0)),
            pl.BlockSpec((FFT_N2, FFT_N2), lambda i, j: (0, 0)),
            pl.BlockSpec((FFT_N2, FFT_N2), lambda i, j: (0, 0)),
            pl.BlockSpec((FFT_N1, FFT_N2, V7X_LANES), lambda i, j: (0, 0, 0)),
            pl.BlockSpec((FFT_N1, FFT_N2, V7X_LANES), lambda i, j: (0, 0, 0)),
            _layer_vec(D_FOURIER, l),
        ],
        out_specs=pl.BlockSpec((None, FFT_N2, D_FOURIER), lambda i, j: (i, 0, j)),
        out_shape=jax.ShapeDtypeStruct((b, FFT_N2, FFT_N1 * D_FOURIER), BF16),
        scratch_shapes=[pltpu.VMEM((FFT_N1, FFT_N2, 2 * D_FOURIER), BF16)],
        compiler_params=_cparams(("arbitrary", "arbitrary")),
        name="fourier_mix",
    )(f, mst, c0, s0, cb, sb, g)
    return out.reshape(b * FFT_N2, FFT_N1 * D_FOURIER)


def _out_proj_body(a_ref, f_ref, x_ref, w_ref, g_ref, wbf_ref, fs_ref):
    @pl.when(pl.program_id(0) == 0)
    def _():
        wbf_ref[...] = w_ref[...].astype(BF16)

    n_blk = D_FOURIER // V7X_LANES
    for k1 in range(FFT_N1):
        for c in range(n_blk):
            col = k1 * D_FOURIER + c * V7X_LANES
            fs_ref[c, pl.ds(k1, TM // FFT_N1, stride=FFT_N1), :] = f_ref[:, col:col + V7X_LANES].astype(F32)
    fo = jnp.concatenate([fs_ref[c] for c in range(n_blk)], axis=1).astype(BF16)
    y = (jnp.dot(a_ref[...], wbf_ref[:D_ATTN, :], preferred_element_type=F32)
         + jnp.dot(fo, wbf_ref[D_ATTN:, :], preferred_element_type=F32))
    xn = x_ref[...] + y
    return xn, _rms(xn, g_ref[...])


def _out_proj_dense_kernel(a_ref, f_ref, x_ref, w_ref, g_ref, xo_ref, h_ref, wbf_ref, fs_ref):
    xn, h = _out_proj_body(a_ref, f_ref, x_ref, w_ref, g_ref, wbf_ref, fs_ref)
    xo_ref[...] = xn
    h_ref[...] = h.astype(BF16)


def _out_proj_moe_kernel(a_ref, f_ref, x_ref, w_ref, g_ref, wr_ref, lt_ref,
                         xo_ref, hrt_ref, gate_ref, rank_ref, cnt_ref,
                         wbf_ref, fs_ref, wr2_ref, hs_ref, carry_ref):
    @pl.when(pl.program_id(0) == 0)
    def _():
        hi, lo = _split_bf16(wr_ref[...])
        wr2_ref[:, :V7X_LANES] = hi
        wr2_ref[:, V7X_LANES:] = lo
        carry_ref[...] = jnp.zeros(carry_ref.shape, F32)

    xn, h = _out_proj_body(a_ref, f_ref, x_ref, w_ref, g_ref, wbf_ref, fs_ref)
    xo_ref[...] = xn
    for s in range(RT):
        hs_ref[pl.ds(s, TM, stride=RT), :] = h[:, s * V7X_LANES:(s + 1) * V7X_LANES]
    hrt_ref[...] = hs_ref[...].astype(BF16)
    h_hi, h_lo = _split_bf16(h)
    both = (jnp.dot(h_hi, wr2_ref[...], preferred_element_type=F32)
            + jnp.dot(h_lo, wr2_ref[...], preferred_element_type=F32))
    logits = both[:, :V7X_LANES] + both[:, V7X_LANES:]
    lane = lax.broadcasted_iota(jnp.int32, logits.shape, 1).astype(F32)
    logits = jnp.where(lane < N_EXPERTS, logits, -jnp.inf)
    m1 = jnp.max(logits, axis=-1, keepdims=True)
    i1 = jnp.min(jnp.where(logits == m1, lane, float(V7X_LANES)), axis=-1, keepdims=True)
    rest = jnp.where(lane == i1, -jnp.inf, logits)
    m2 = jnp.max(rest, axis=-1, keepdims=True)
    i2 = jnp.min(jnp.where(rest == m2, lane, float(V7X_LANES)), axis=-1, keepdims=True)
    e2 = jnp.exp(m2 - m1)
    den = 1.0 + e2
    sel1, sel2 = lane == i1, lane == i2
    sel = sel1 | sel2
    gate_ref[...] = jnp.where(sel1, 1.0 / den, jnp.where(sel2, e2 / den, 0.0))
    self32 = sel.astype(F32)
    rank = jnp.dot(lt_ref[...], self32.astype(BF16), preferred_element_type=F32) + carry_ref[...]
    rank_ref[...] = jnp.where(sel, rank, -1.0)
    carry_ref[...] = carry_ref[...] + jnp.sum(self32, axis=0, keepdims=True)
    cnt_ref[...] = carry_ref[...]


def _out_proj(l, a, f, x, w, g, moe=None):
    t = x.shape[0]
    row = lambda i: (i, 0)
    fixed = lambda i: (0, 0)
    in_specs = [
        pl.BlockSpec((TM, D_ATTN), row),
        pl.BlockSpec((TM // FFT_N1, FFT_N1 * D_FOURIER), row),
        pl.BlockSpec((TM, D_MODEL), row),
        pl.BlockSpec((None, D_MODEL, D_MODEL), lambda i: (l, 0, 0)),
        _layer_vec(D_MODEL, l),
    ]
    if moe is None:
        return pl.pallas_call(
            _out_proj_dense_kernel,
            grid=(t // TM,),
            in_specs=in_specs,
            out_specs=[pl.BlockSpec((TM, D_MODEL), row), pl.BlockSpec((TM, D_MODEL), row)],
            out_shape=[jax.ShapeDtypeStruct((t, D_MODEL), F32), jax.ShapeDtypeStruct((t, D_MODEL), BF16)],
            scratch_shapes=[pltpu.VMEM((D_MODEL, D_MODEL), BF16), pltpu.VMEM((D_FOURIER // V7X_LANES, TM, V7X_LANES), F32)],
            compiler_params=_cparams(("arbitrary",)),
            name="out_proj",
        )(a, f, x, w, g)
    li, w_router_pad, ltri = moe
    return pl.pallas_call(
        _out_proj_moe_kernel,
        grid=(t // TM,),
        in_specs=in_specs + [pl.BlockSpec((None, D_MODEL, V7X_LANES), lambda i: (li, 0, 0)),
                             pl.BlockSpec((TM, TM), fixed)],
        out_specs=[
            pl.BlockSpec((TM, D_MODEL), row),
            pl.BlockSpec((TM * RT, V7X_LANES), row),
            pl.BlockSpec((TM, V7X_LANES), row),
            pl.BlockSpec((TM, V7X_LANES), row),
            pl.BlockSpec((1, V7X_LANES), fixed),
        ],
        out_shape=[
            jax.ShapeDtypeStruct((t, D_MODEL), F32),
            jax.ShapeDtypeStruct((t * RT, V7X_LANES), BF16),
            jax.ShapeDtypeStruct((t, V7X_LANES), F32),
            jax.ShapeDtypeStruct((t, V7X_LANES), F32),
            jax.ShapeDtypeStruct((1, V7X_LANES), F32),
        ],
        scratch_shapes=[
            pltpu.VMEM((D_MODEL, D_MODEL), BF16),
            pltpu.VMEM((D_FOURIER // V7X_LANES, TM, V7X_LANES), F32),
            pltpu.VMEM((D_MODEL, 2 * V7X_LANES), BF16),
            pltpu.VMEM((TM * RT, V7X_LANES), F32),
            pltpu.VMEM((1, V7X_LANES), F32),
        ],
        compiler_params=_cparams(("arbitrary",)),
        name="out_proj_route",
    )(a, f, x, w, g, w_router_pad, ltri)


def _ffn_kernel(x_ref, h_ref, wg_ref, wu_ref, wd_ref, o_ref, act_ref):
    @pl.when(pl.program_id(1) == 0)
    def _():
        o_ref[...] = x_ref[...]

    h = h_ref[...]
    for c in range(TF_FFN // V7X_MXU):
        cols = slice(c * V7X_MXU, (c + 1) * V7X_MXU)
        g = jnp.dot(h, wg_ref[:, cols].astype(BF16), preferred_element_type=F32)
        u = jnp.dot(h, wu_ref[:, cols].astype(BF16), preferred_element_type=F32)
        act_ref[:, cols] = (g * jax.nn.sigmoid(g) * u).astype(BF16)
    o_ref[...] += jnp.dot(act_ref[...], wd_ref[...].astype(BF16), preferred_element_type=F32)


def _ffn(li, x, h, wg, wu, wd):
    t = x.shape[0]
    return pl.pallas_call(
        _ffn_kernel,
        grid=(t // TM_FFN, D_FF // TF_FFN),
        in_specs=[
            pl.BlockSpec((TM_FFN, D_MODEL), lambda i, j: (i, 0)),
            pl.BlockSpec((TM_FFN, D_MODEL), lambda i, j: (i, 0)),
            pl.BlockSpec((None, D_MODEL, TF_FFN), lambda i, j: (li, 0, j)),
            pl.BlockSpec((None, D_MODEL, TF_FFN), lambda i, j: (li, 0, j)),
            pl.BlockSpec((None, TF_FFN, D_MODEL), lambda i, j: (li, j, 0)),
        ],
        out_specs=pl.BlockSpec((TM_FFN, D_MODEL), lambda i, j: (i, 0)),
        out_shape=jax.ShapeDtypeStruct((t, D_MODEL), F32),
        scratch_shapes=[pltpu.VMEM((TM_FFN, TF_FFN), BF16)],
        compiler_params=_cparams(("arbitrary", "arbitrary")),
        name="ffn_dense",
    )(x, h, wg, wu, wd)


def _plan_kernel(gate_ref, rank_ref, cnt_ref, pos_ref, wgt_ref, meta_ref):
    lane = lax.broadcasted_iota(jnp.int32, (1, V7X_LANES), 1)
    cnt = cnt_ref[...]
    padded = jnp.floor((cnt + (TM_MOE - 1)) * (1.0 / TM_MOE)) * TM_MOE
    base = jnp.zeros_like(padded)
    for j in range(1, N_EXPERTS):
        base = base + jnp.where(lane >= j, pltpu.roll(padded, j, 1), 0.0)
    rank = rank_ref[...]
    pos_t = jnp.where(rank >= 0.0, base + rank, -1.0).T
    gate_t = gate_ref[...].T
    seen = jnp.zeros((1, TM_PLAN), F32)
    pos_lo = jnp.zeros((1, TM_PLAN), F32)
    pos_hi = jnp.zeros((1, TM_PLAN), F32)
    w_lo = jnp.zeros((1, TM_PLAN), F32)
    w_hi = jnp.zeros((1, TM_PLAN), F32)
    for e in range(N_EXPERTS):
        p = pos_t[e:e + 1, :]
        gt = gate_t[e:e + 1, :]
        chosen = p >= 0.0
        first = chosen & (seen == 0.0)
        second = chosen & (seen == 1.0)
        pos_lo = jnp.where(first, p, pos_lo)
        pos_hi = jnp.where(second, p, pos_hi)
        w_lo = jnp.where(first, gt, w_lo)
        w_hi = jnp.where(second, gt, w_hi)
        seen = seen + chosen.astype(F32)
    pos_ref[0:1, :] = pos_lo.astype(jnp.int32)
    pos_ref[1:2, :] = pos_hi.astype(jnp.int32)
    wgt_ref[0:1, :] = w_lo
    wgt_ref[1:2, :] = w_hi
    ends = base + padded
    tile_start = (lane * TM_MOE).astype(F32)
    tile_e = jnp.zeros((1, V7X_LANES), F32)
    for e in range(N_EXPERTS):
        tile_e = tile_e + (ends[:, e:e + 1] <= tile_start).astype(F32)
    tile_e = jnp.minimum(tile_e, float(N_EXPERTS - 1))
    n_used = ends[:, N_EXPERTS - 1:N_EXPERTS] * (1.0 / TM_MOE) + jnp.zeros((1, V7X_LANES), F32)
    meta_ref[0:1, :] = tile_e.astype(jnp.int32)
    meta_ref[1:2, :] = n_used.astype(jnp.int32)
    meta_ref[2:3, :] = cnt.astype(jnp.int32)
    meta_ref[3:4, :] = base.astype(jnp.int32)
    meta_ref[4:8, :] = jnp.zeros((4, V7X_LANES), jnp.int32)


def _plan(gates, rank, counts):
    t = gates.shape[0]
    return pl.pallas_call(
        _plan_kernel,
        grid=(t // TM_PLAN,),
        in_specs=[
            pl.BlockSpec((TM_PLAN, V7X_LANES), lambda i: (i, 0)),
            pl.BlockSpec((TM_PLAN, V7X_LANES), lambda i: (i, 0)),
            pl.BlockSpec((1, V7X_LANES), lambda i: (0, 0)),
        ],
        out_specs=[
            pl.BlockSpec((2, TM_PLAN), lambda i: (0, i)),
            pl.BlockSpec((2, TM_PLAN), lambda i: (0, i)),
            pl.BlockSpec((8, V7X_LANES), lambda i: (0, 0)),
        ],
        out_shape=[
            jax.ShapeDtypeStruct((2, t), jnp.int32),
            jax.ShapeDtypeStruct((2, t), F32),
            jax.ShapeDtypeStruct((8, V7X_LANES), jnp.int32),
        ],
        compiler_params=_cparams(("arbitrary",)),
        name="moe_plan",
    )(gates, rank, counts)


def _rt_rows(ref, row, n=1, per=RT):
    start = row * per if isinstance(row, int) else pl.multiple_of(row * per, per)
    return ref.at[pl.ds(start, n * per)]


def _dispatch_kernel(plo_ref, phi_ref, cnt_ref, base_ref, nt_ref, h_ref, o_ref, z_ref, sem):
    i = pl.program_id(0)
    t0 = i * TM

    def issue(r, carry):
        src = _rt_rows(h_ref, r)
        pltpu.make_async_copy(src, _rt_rows(o_ref, plo_ref[t0 + r]), sem).start(priority=0)
        pltpu.make_async_copy(src, _rt_rows(o_ref, phi_ref[t0 + r]), sem).start(priority=1)
        return carry

    lax.fori_loop(0, TM, issue, 0, unroll=DMA_UNROLL)
    for _ in range(2):
        pltpu.make_async_copy(h_ref, _rt_rows(o_ref, 0, TM), sem).wait()

    @pl.when(i == pl.num_programs(0) - 1)
    def _():
        z_ref[...] = jnp.zeros(z_ref.shape, z_ref.dtype)
        for e in range(N_EXPERTS):
            n = cnt_ref[e]
            n_pad = ((n + (TM_MOE - 1)) // TM_MOE) * TM_MOE - n
            first = base_ref[e] + n

            def pad_issue(r, carry, first=first):
                pltpu.make_async_copy(_rt_rows(z_ref, 0), _rt_rows(o_ref, first + r), sem).start()
                return carry

            def pad_drain(r, carry):
                pltpu.make_async_copy(_rt_rows(z_ref, 0), _rt_rows(o_ref, 0), sem).wait()
                return carry

            lax.fori_loop(0, n_pad, pad_issue, 0)
            lax.fori_loop(0, n_pad, pad_drain, 0)

        def tail(tile, carry):
            cp = pltpu.make_async_copy(z_ref, _rt_rows(o_ref, tile * TM_MOE, TM_MOE), sem)
            cp.start()
            cp.wait()
            return carry

        lax.fori_loop(nt_ref[0], o_ref.shape[0] // (TM_MOE * RT), tail, 0)


def _dispatch(pos_lo, pos_hi, cnt, base, n_used, h_rt, n_rows):
    t = h_rt.shape[0] // RT
    return pl.pallas_call(
        _dispatch_kernel,
        grid_spec=pltpu.PrefetchScalarGridSpec(
            num_scalar_prefetch=5,
            grid=(t // TM,),
            in_specs=[pl.BlockSpec((TM * RT, V7X_LANES), lambda i, *_: (i, 0))],
            out_specs=pl.BlockSpec(memory_space=pl.ANY),
            scratch_shapes=[pltpu.VMEM((TM_MOE * RT, V7X_LANES), BF16), pltpu.SemaphoreType.DMA(())],
        ),
        out_shape=jax.ShapeDtypeStruct((n_rows * RT, V7X_LANES), BF16),
        compiler_params=_cparams(("arbitrary",)),
        name="moe_dispatch",
    )(pos_lo, pos_hi, cnt, base, n_used, h_rt)


def _tile_of(i, nt_ref):
    return jnp.minimum(i, nt_ref[0] - 1)


def _moe_up_kernel(te_ref, nt_ref, hs_ref, wg_ref, wu_ref, o_ref, xs_ref):
    i = pl.program_id(1)

    @pl.when(i < nt_ref[0])
    def _():
        xs_ref[...] = hs_ref[...].astype(F32)
        x = jnp.concatenate([xs_ref[pl.ds(s, TM_MOE, stride=RT), :].astype(BF16) for s in range(RT)],
                            axis=1)
        for c in range(TF_UP // V7X_MXU):
            cols = slice(c * V7X_MXU, (c + 1) * V7X_MXU)
            g = jnp.dot(x, wg_ref[:, cols].astype(BF16), preferred_element_type=F32)
            u = jnp.dot(x, wu_ref[:, cols].astype(BF16), preferred_element_type=F32)
            o_ref[:, cols] = (g * jax.nn.sigmoid(g) * u).astype(BF16)

    @pl.when(i >= nt_ref[0])
    def _():
        o_ref[...] = jnp.zeros(o_ref.shape, o_ref.dtype)


def _moe_up(li, tile_e, n_used, hs, wg, wu):
    n_rows = hs.shape[0] // RT
    nt = n_rows // TM_MOE
    wspec = pl.BlockSpec((None, None, D_MODEL, TF_UP), lambda j, i, te, n: (li, te[_tile_of(i, n)], 0, j))
    return pl.pallas_call(
        _moe_up_kernel,
        grid_spec=pltpu.PrefetchScalarGridSpec(
            num_scalar_prefetch=2,
            grid=(D_FF // TF_UP, nt),
            in_specs=[pl.BlockSpec((TM_MOE * RT, V7X_LANES), lambda j, i, te, n: (_tile_of(i, n), 0)), wspec, wspec],
            out_specs=pl.BlockSpec((TM_MOE, TF_UP), lambda j, i, te, n: (i, j)),
            scratch_shapes=[pltpu.VMEM((TM_MOE * RT, V7X_LANES), F32)],
        ),
        out_shape=jax.ShapeDtypeStruct((n_rows, D_FF), BF16),
        compiler_params=_cparams(("arbitrary", "arbitrary")),
        name="moe_up",
    )(tile_e, n_used, hs, wg, wu)


def _moe_down_kernel(te_ref, nt_ref, a_ref, wd_ref, o_ref):
    i = pl.program_id(0)

    @pl.when(i < nt_ref[0])
    def _():
        a = a_ref[...]
        for c in range(D_MODEL // V7X_MXU):
            y = jnp.dot(a, wd_ref[:, c * V7X_MXU:(c + 1) * V7X_MXU].astype(BF16), preferred_element_type=F32)
            for k in range(V7X_MXU // V7X_LANES):
                s = c * (V7X_MXU // V7X_LANES) + k
                o_ref[pl.ds(s, TM_MOE, stride=RT), :] = y[:, k * V7X_LANES:(k + 1) * V7X_LANES]

    @pl.when(i >= nt_ref[0])
    def _():
        o_ref[...] = jnp.zeros(o_ref.shape, o_ref.dtype)


def _moe_down(li, tile_e, n_used, act, wd):
    n_rows = act.shape[0]
    nt = n_rows // TM_MOE
    return pl.pallas_call(
        _moe_down_kernel,
        grid_spec=pltpu.PrefetchScalarGridSpec(
            num_scalar_prefetch=2,
            grid=(nt,),
            in_specs=[
                pl.BlockSpec((TM_MOE, D_FF), lambda i, te, n: (_tile_of(i, n), 0)),
                pl.BlockSpec((None, None, D_FF, D_MODEL), lambda i, te, n: (li, te[_tile_of(i, n)], 0, 0)),
            ],
            out_specs=pl.BlockSpec((TM_MOE * RT, V7X_LANES), lambda i, te, n: (i, 0)),
        ),
        out_shape=jax.ShapeDtypeStruct((n_rows * RT, V7X_LANES), F32),
        compiler_params=_cparams(("arbitrary",)),
        name="moe_down",
    )(tile_e, n_used, act, wd)


def _combine_kernel(plo_ref, phi_ref, x_ref, wlo_ref, whi_ref, y_ref, o_ref, lo_ref, hi_ref, sem):
    t0 = pl.program_id(0) * TM

    def issue(r, carry):
        pltpu.make_async_copy(_rt_rows(y_ref, plo_ref[t0 + r]), _rt_rows(lo_ref, r), sem).start(priority=0)
        pltpu.make_async_copy(_rt_rows(y_ref, phi_ref[t0 + r]), _rt_rows(hi_ref, r), sem).start(priority=1)
        return carry

    lax.fori_loop(0, TM, issue, 0, unroll=DMA_UNROLL)
    pltpu.make_async_copy(_rt_rows(y_ref, 0, TM), lo_ref, sem).wait()
    pltpu.make_async_copy(_rt_rows(y_ref, 0, TM), hi_ref, sem).wait()
    w_lo, w_hi = wlo_ref[...], whi_ref[...]
    for s in range(RT):
        cols = slice(s * V7X_LANES, (s + 1) * V7X_LANES)
        o_ref[:, cols] = x_ref[:, cols] + (w_lo * lo_ref[pl.ds(s, TM, stride=RT), :]
                                           + w_hi * hi_ref[pl.ds(s, TM, stride=RT), :])


def _combine(pos_lo, pos_hi, x, w_lo, w_hi, ys):
    t = x.shape[0]
    return pl.pallas_call(
        _combine_kernel,
        grid_spec=pltpu.PrefetchScalarGridSpec(
            num_scalar_prefetch=2,
            grid=(t // TM,),
            in_specs=[
                pl.BlockSpec((TM, D_MODEL), lambda i, a, b: (i, 0)),
                pl.BlockSpec((TM, 1), lambda i, a, b: (i, 0)),
                pl.BlockSpec((TM, 1), lambda i, a, b: (i, 0)),
                pl.BlockSpec(memory_space=pl.ANY),
            ],
            out_specs=pl.BlockSpec((TM, D_MODEL), lambda i, a, b: (i, 0)),
            scratch_shapes=[pltpu.VMEM((TM * RT, V7X_LANES), F32), pltpu.VMEM((TM * RT, V7X_LANES), F32),
                            pltpu.SemaphoreType.DMA(())],
        ),
        out_shape=jax.ShapeDtypeStruct((t, D_MODEL), F32),
        compiler_params=_cparams(("arbitrary",)),
        name="moe_combine",
    )(pos_lo, pos_hi, x, w_lo, w_hi, ys)


def _moe(li, x, h_rt, gates, rank, counts, e_gate, e_up, e_down):
    t = x.shape[0]
    n_rows = ((2 * t) // TM_MOE + N_EXPERTS) * TM_MOE
    pos, wgt, meta = _plan(gates, rank, counts)
    pos_lo, pos_hi = pos[0], pos[1]
    tile_e, n_used = meta[0], meta[1, :1]
    cnt, base = meta[2, :N_EXPERTS], meta[3, :N_EXPERTS]
    hs = _dispatch(pos_lo, pos_hi, cnt, base, n_used, h_rt, n_rows)
    act = _moe_up(li, tile_e, n_used, hs, e_gate, e_up)
    ys = _moe_down(li, tile_e, n_used, act, e_down)
    return _combine(pos_lo, pos_hi, x, wgt[0].reshape(t, 1), wgt[1].reshape(t, 1), ys)


def _rope_tables(positions):
    inv_freq = ROPE_THETA ** (-jnp.arange(0, ROT_DIM, 2, dtype=F32) / ROT_DIM)
    ang = positions.astype(F32).reshape(-1, 1) * inv_freq
    cos, sin = jnp.cos(ang), jnp.sin(ang)
    t = cos.shape[0]
    half = ROT_DIM // 2
    ones = jnp.ones((t, HEAD_DIM - ROT_DIM), F32)
    zeros = jnp.zeros((t, HEAD_DIM - ROT_DIM), F32)
    zh = jnp.zeros((t, half), F32)
    reps = V7X_LANES // HEAD_DIM
    cos_t = jnp.tile(jnp.concatenate([cos, cos, ones], axis=1), (1, reps))
    sin_a = jnp.tile(jnp.concatenate([zh, sin, zeros], axis=1), (1, reps))
    sin_b = jnp.tile(jnp.concatenate([-sin, zh, zeros], axis=1), (1, reps))
    return cos_t, sin_a, sin_b


def _dft_tables(seq):
    n2 = jnp.arange(FFT_N2, dtype=jnp.int32)
    m = (n2[:, None] * n2[None, :]) % FFT_N2
    ang = m.astype(F32) * (2.0 * np.pi / FFT_N2)
    c0, s0 = jnp.cos(ang).astype(BF16), jnp.sin(ang).astype(BF16)
    k1 = jnp.arange(FFT_N1, dtype=jnp.int32)
    beta = (k1[:, None] * n2[None, :]).astype(F32) * (2.0 * np.pi / seq)
    cb = jnp.broadcast_to(jnp.cos(beta)[:, :, None], (FFT_N1, FFT_N2, V7X_LANES))
    sb = jnp.broadcast_to(jnp.sin(beta)[:, :, None], (FFT_N1, FFT_N2, V7X_LANES))
    c = jnp.arange(FOURIER_GROUP, dtype=jnp.int32)
    angc = ((c[:, None] * c[None, :]) % FOURIER_GROUP).astype(F32) * (2.0 * np.pi / FOURIER_GROUP)
    return c0, s0, cb, sb, jnp.cos(angc), jnp.sin(angc)


def kernel(x, positions, norm_mix, w_in, q_norm, k_norm, sink, w_fmix, g_attn_out, g_fourier_out, w_out, norm_ffn,
           ffn_gate, ffn_up, ffn_down, w_router, e_gate, e_up, e_down):
    b, s, d = x.shape
    depth = w_in.shape[0]
    assert (d, s) == (D_MODEL, FFT_N1 * FFT_N2) and (b * s) % TM_PLAN == 0
    t = b * s
    cos_t, sin_a, sin_b = _rope_tables(positions)
    c0, s0, cb, sb, cc, sc = _dft_tables(s)
    mst = _fmix_prep(w_fmix, cc, sc)
    head = jnp.arange(D_QK) // HEAD_DIM
    member = head[:, None] == jnp.arange(V7X_LANES)[None, :]
    head_sum = member.astype(BF16) * (1.0 / HEAD_DIM)
    head_expand = member.T.astype(BF16)
    row_i = jnp.arange(TM)
    ltri = (row_i[None, :] < row_i[:, None]).astype(BF16)
    gain = jnp.concatenate([jnp.tile(q_norm, (1, N_Q_HEADS)) * (HEAD_DIM ** -0.5),
                            jnp.tile(k_norm, (1, N_KV_HEADS))], axis=1).reshape(depth, 1, D_QK)
    vec = lambda p: p.reshape(depth, 1, p.shape[-1])
    norm_mix, norm_ffn, g_attn_out, g_fourier_out = vec(norm_mix), vec(norm_ffn), vec(g_attn_out), vec(g_fourier_out)
    wr = jnp.pad(w_router, ((0, 0), (0, 0), (0, V7X_LANES - N_EXPERTS)))
    xt = x.reshape(t, d)
    for l in range(depth):
        q, k, v, f = _in_proj(l, xt, norm_mix, w_in, head_sum, head_expand, gain, cos_t, sin_a, sin_b)
        a = _attention(l, q.reshape(b, s, D_ATTN), k.reshape(b, s, D_KV), v.reshape(b, s, D_KV), sink, g_attn_out)
        fo = _fourier(l, f.reshape(b, s, D_FOURIER), mst, c0, s0, cb, sb, g_fourier_out)
        a2, f2 = a.reshape(t, D_ATTN), fo
        li = l // 2
        if l % 2 == 0:
            xt, h = _out_proj(l, a2, f2, xt, w_out, norm_ffn)
            xt = _ffn(li, xt, h, ffn_gate, ffn_up, ffn_down)
        else:
            xt, h_rt, gates, rank, counts = _out_proj(l, a2, f2, xt, w_out, norm_ffn, moe=(li, wr, ltri))
            xt = _moe(li, xt, h_rt, gates, rank, counts, e_gate, e_up, e_down)
    return xt.reshape(b, s, d)
```

```python
import functools

import jax
import jax.numpy as jnp
import numpy as np
from jax import lax
from jax.experimental import pallas as pl
from jax.experimental.pallas import tpu as pltpu

F32 = jnp.float32
BF16 = jnp.bfloat16

D_MODEL = 1024
HEAD_DIM = 64
D_ATTN = 512
N_Q_HEADS = 8
N_KV_HEADS = 2
Q_PER_KV = 4
D_KV = 128
D_FOURIER = 512
N_FOURIER_GROUPS = 8
FOURIER_GROUP = 64
D_QK = D_ATTN + D_KV
D_IN = D_ATTN + 2 * D_KV + D_FOURIER
WINDOW = 128
ROPE_THETA = 500000.0
ROT_DIM = 16
D_FF = 3584
N_EXPERTS = 8
EPS = 1e-6

V7X_LANES = 128
V7X_SUBLANES = 8
V7X_MXU = 256
V7X_VMEM_LIMIT = 56 * 1024 * 1024
RT = D_MODEL // V7X_LANES
assert RT == V7X_SUBLANES

TM = 1024
TQ = 512
QB = 128
KV_WIN = QB + 2 * WINDOW
FFT_N1 = 4
FFT_N2 = 1024
FFT_CHUNK = 128
TM_FFN = 2048
TF_FFN = 512
TM_MOE = 512
TF_UP = D_FF // 2
TM_PLAN = 1024
DMA_UNROLL = 8


def _cparams(sem, vmem=V7X_VMEM_LIMIT):
    return pltpu.CompilerParams(dimension_semantics=sem, vmem_limit_bytes=vmem)


def _rms(xf, g):
    ms = jnp.mean(xf * xf, axis=-1, keepdims=True)
    return xf * lax.rsqrt(ms + EPS) * g


def _split_bf16(xf):
    hi = xf.astype(BF16)
    lo = (xf - hi.astype(F32)).astype(BF16)
    return hi, lo


def _layer_vec(n, l):
    return pl.BlockSpec((None, 1, n), lambda *_: (l, 0, 0))


def _in_proj_kernel(x_ref, g_ref, w_ref, hs_ref, he_ref, gain_ref, cos_ref, sa_ref, sb_ref,
                    q_ref, k_ref, v_ref, f_ref, wbf_ref):
    @pl.when(pl.program_id(0) == 0)
    def _():
        wbf_ref[...] = w_ref[...].astype(BF16)

    h = _rms(x_ref[...], g_ref[...]).astype(BF16)
    z = jnp.dot(h, wbf_ref[...], preferred_element_type=F32)
    qk = z[:, :D_QK]
    hi, lo = _split_bf16(qk * qk)
    ms = (jnp.dot(hi, hs_ref[...], preferred_element_type=F32)
          + jnp.dot(lo, hs_ref[...], preferred_element_type=F32))
    r_hi, r_lo = _split_bf16(lax.rsqrt(ms + EPS))
    inv_rms = (jnp.dot(r_hi, he_ref[...], preferred_element_type=F32)
               + jnp.dot(r_lo, he_ref[...], preferred_element_type=F32))
    qkn = qk * inv_rms * gain_ref[...]
    cos_t, sin_a, sin_b = cos_ref[...], sa_ref[...], sb_ref[...]
    for c in range(D_QK // V7X_LANES):
        blk = qkn[:, c * V7X_LANES:(c + 1) * V7X_LANES]
        rot = (blk * cos_t + pltpu.roll(blk, ROT_DIM // 2, 1) * sin_a
               + pltpu.roll(blk, V7X_LANES - ROT_DIM // 2, 1) * sin_b).astype(BF16)
        if c < D_ATTN // V7X_LANES:
            q_ref[:, c * V7X_LANES:(c + 1) * V7X_LANES] = rot
        else:
            k_ref[...] = rot
    v_ref[...] = z[:, D_QK:D_QK + D_KV].astype(BF16)
    f_ref[...] = z[:, D_QK + D_KV:].astype(BF16)


def _in_proj(l, x, g, w, head_sum, head_expand, gain, cos_t, sin_a, sin_b):
    t = x.shape[0]
    row = lambda i: (i, 0)
    fixed = lambda i: (0, 0)
    return pl.pallas_call(
        _in_proj_kernel,
        grid=(t // TM,),
        in_specs=[
            pl.BlockSpec((TM, D_MODEL), row),
            _layer_vec(D_MODEL, l),
            pl.BlockSpec((None, D_MODEL, D_IN), lambda i: (l, 0, 0)),
            pl.BlockSpec((D_QK, V7X_LANES), fixed),
            pl.BlockSpec((V7X_LANES, D_QK), fixed),
            _layer_vec(D_QK, l),
            pl.BlockSpec((TM, V7X_LANES), row),
            pl.BlockSpec((TM, V7X_LANES), row),
            pl.BlockSpec((TM, V7X_LANES), row),
        ],
        out_specs=[
            pl.BlockSpec((TM, D_ATTN), row),
            pl.BlockSpec((TM, D_KV), row),
            pl.BlockSpec((TM, D_KV), row),
            pl.BlockSpec((TM, D_FOURIER), row),
        ],
        out_shape=[
            jax.ShapeDtypeStruct((t, D_ATTN), BF16),
            jax.ShapeDtypeStruct((t, D_KV), BF16),
            jax.ShapeDtypeStruct((t, D_KV), BF16),
            jax.ShapeDtypeStruct((t, D_FOURIER), BF16),
        ],
        scratch_shapes=[pltpu.VMEM((D_MODEL, D_IN), BF16)],
        compiler_params=_cparams(("arbitrary",)),
        name="in_proj",
    )(x, g, w, head_sum, head_expand, gain, cos_t, sin_a, sin_b)


def _attn_kernel(sink_ref, q_ref, k_ref, v_ref, g_ref, o_ref, s_ref, p_ref, *, seq, layer):
    q0 = pl.program_id(1) * TQ
    gain = g_ref[...]
    heads = range(N_Q_HEADS)

    def block(j, carry):
        r0 = pl.multiple_of(j * QB, QB)
        qs = q0 + r0
        ws = pl.multiple_of(jnp.clip(qs - WINDOW, 0, seq - KV_WIN), QB)
        kw = k_ref[pl.ds(ws, KV_WIN), :]
        vw = v_ref[pl.ds(ws, KV_WIN), :]
        q_abs = qs + lax.broadcasted_iota(jnp.int32, (QB, KV_WIN), 0)
        k_abs = ws + lax.broadcasted_iota(jnp.int32, (QB, KV_WIN), 1)
        mask = jnp.abs(q_abs - k_abs) <= WINDOW
        q = q_ref[pl.ds(r0, QB), :]
        for h in heads:
            kvh = h // Q_PER_KV
            qh = q[:, h * HEAD_DIM:(h + 1) * HEAD_DIM]
            kh = kw[:, kvh * HEAD_DIM:(kvh + 1) * HEAD_DIM]
            s = lax.dot_general(qh, kh, (((1,), (1,)), ((), ())), preferred_element_type=F32)
            s_ref[h] = jnp.where(mask, s, -jnp.inf)
        m = [jnp.maximum(jnp.max(s_ref[h], axis=-1, keepdims=True), sink_ref[layer, h]) for h in heads]
        inv = []
        for h in heads:
            p = jnp.exp(s_ref[h] - m[h])
            inv.append(1.0 / (jnp.sum(p, axis=-1, keepdims=True) + jnp.exp(sink_ref[layer, h] - m[h])))
            p_ref[h] = p.astype(BF16)
        outs = []
        for h in heads:
            kvh = h // Q_PER_KV
            vh = vw[:, kvh * HEAD_DIM:(kvh + 1) * HEAD_DIM]
            outs.append(jnp.dot(p_ref[h], vh, preferred_element_type=F32) * inv[h])
        a = jnp.concatenate(outs, axis=-1)
        o_ref[pl.ds(r0, QB), :] = _rms(a, gain).astype(BF16)
        return carry

    lax.fori_loop(0, TQ // QB, block, 0)


def _attention(l, q, k, v, sink, g):
    b, s, _ = q.shape
    return pl.pallas_call(
        functools.partial(_attn_kernel, seq=s, layer=l),
        grid=(b, s // TQ),
        in_specs=[
            pl.BlockSpec(memory_space=pltpu.SMEM),
            pl.BlockSpec((None, TQ, D_ATTN), lambda i, j: (i, j, 0)),
            pl.BlockSpec((None, s, D_KV), lambda i, j: (i, 0, 0)),
            pl.BlockSpec((None, s, D_KV), lambda i, j: (i, 0, 0)),
            _layer_vec(D_ATTN, l),
        ],
        out_specs=pl.BlockSpec((None, TQ, D_ATTN), lambda i, j: (i, j, 0)),
        out_shape=jax.ShapeDtypeStruct((b, s, D_ATTN), BF16),
        scratch_shapes=[pltpu.VMEM((N_Q_HEADS, QB, KV_WIN), F32), pltpu.VMEM((N_Q_HEADS, QB, KV_WIN), BF16)],
        compiler_params=_cparams(("arbitrary", "arbitrary")),
        name="band_attn",
    )(sink, q, k, v, g)


def _fmix_prep_kernel(w_ref, cc_ref, sc_ref, o_ref):
    o_ref[...] = jnp.zeros(o_ref.shape, o_ref.dtype)
    scale = 1.0 / 512.0
    for g in range(N_FOURIER_GROUPS):
        w = w_ref[g]
        mr = jnp.dot(cc_ref[...], w, preferred_element_type=F32, precision=lax.Precision.HIGHEST) * scale
        mi = jnp.dot(sc_ref[...], w, preferred_element_type=F32, precision=lax.Precision.HIGHEST) * (-scale)
        c = (g * FOURIER_GROUP) // FFT_CHUNK
        off = (g * FOURIER_GROUP) % FFT_CHUNK
        rows = slice(g * FOURIER_GROUP, (g + 1) * FOURIER_GROUP)
        o_ref[c, rows, off:off + FOURIER_GROUP] = mr.astype(BF16)
        o_ref[c, rows, FFT_CHUNK + off:FFT_CHUNK + off + FOURIER_GROUP] = mi.astype(BF16)


def _fmix_prep(w_fmix, cc, sc):
    depth = w_fmix.shape[0]
    n_chunk = D_FOURIER // FFT_CHUNK
    return pl.pallas_call(
        _fmix_prep_kernel,
        grid=(depth,),
        in_specs=[
            pl.BlockSpec((None, N_FOURIER_GROUPS, FOURIER_GROUP, FOURIER_GROUP), lambda l: (l, 0, 0, 0)),
            pl.BlockSpec((FOURIER_GROUP, FOURIER_GROUP), lambda l: (0, 0)),
            pl.BlockSpec((FOURIER_GROUP, FOURIER_GROUP), lambda l: (0, 0)),
        ],
        out_specs=pl.BlockSpec((None, n_chunk, D_FOURIER, 2 * FFT_CHUNK), lambda l: (l, 0, 0, 0)),
        out_shape=jax.ShapeDtypeStruct((depth, n_chunk, D_FOURIER, 2 * FFT_CHUNK), BF16),
        compiler_params=_cparams(("arbitrary",)),
        name="fmix_prep",
    )(w_fmix, cc, sc)


def _fourier_kernel(f_ref, m_ref, c0_ref, s0_ref, cb_ref, sb_ref, g_ref, o_ref, v_ref):
    k1 = pl.program_id(1)

    @pl.when(k1 == 0)
    def _():
        for c in range(D_FOURIER // FFT_CHUNK):
            mc = m_ref[c]
            z = [jnp.dot(f_ref[n1 * FFT_N2:(n1 + 1) * FFT_N2, :], mc, preferred_element_type=F32)
                 for n1 in range(FFT_N1)]
            zr = [t[:, :FFT_CHUNK] for t in z]
            zi = [t[:, FFT_CHUNK:] for t in z]
            ar, ai = zr[0] + zr[2], zi[0] + zi[2]
            br, bi = zr[1] + zr[3], zi[1] + zi[3]
            cr, ci = zr[0] - zr[2], zi[0] - zi[2]
            dr, di = zr[1] - zr[3], zi[1] - zi[3]
            u = [(ar + br, ai + bi), (cr + di, ci - dr), (ar - br, ai - bi), (cr - di, ci + dr)]
            for kk in range(FFT_N1):
                ur, ui = u[kk]
                cb, sb = cb_ref[kk], sb_ref[kk]
                v_ref[kk, :, c * FFT_CHUNK:(c + 1) * FFT_CHUNK] = (ur * cb + ui * sb).astype(BF16)
                v_ref[kk, :, D_FOURIER + c * FFT_CHUNK:D_FOURIER + (c + 1) * FFT_CHUNK] = (
                    ui * cb - ur * sb).astype(BF16)

    vr = v_ref[k1, :, :D_FOURIER]
    vi = v_ref[k1, :, D_FOURIER:]
    out = (jnp.dot(c0_ref[...], vr, preferred_element_type=F32)
           + jnp.dot(s0_ref[...], vi, preferred_element_type=F32))
    o_ref[...] = _rms(out, g_ref[...]).astype(BF16)


def _fourier(l, f, mst, c0, s0, cb, sb, g):
    b, s, _ = f.shape
    n_chunk = D_FOURIER // FFT_CHUNK
    out = pl.pallas_call(
        _fourier_kernel,
        grid=(b, FFT_N1),
        in_specs=[
            pl.BlockSpec((None, s, D_FOURIER), lambda i, j: (i, 0, 0)),
            pl.BlockSpec((None, n_chunk, D_FOURIER, 2 * FFT_CHUNK), lambda i, j: (l, 0, 0, 0)),
            pl.BlockSpec((FFT_N2, FFT_N2), lambda i, j: (0, 0)),
            pl.BlockSpec((FFT_N2, FFT_N2), lambda i, j: (0, 0)),
            pl.BlockSpec((FFT_N1, FFT_N2, V7X_LANES), lambda i, j: (0, 0, 0)),
            pl.BlockSpec((FFT_N1, FFT_N2, V7X_LANES), lambda i, j: (0, 0, 0)),
            _layer_vec(D_FOURIER, l),
        ],
        out_specs=pl.BlockSpec((None, FFT_N2, D_FOURIER), lambda i, j: (i, 0, j)),
        out_shape=jax.ShapeDtypeStruct((b, FFT_N2, FFT_N1 * D_FOURIER), BF16),
        scratch_shapes=[pltpu.VMEM((FFT_N1, FFT_N2, 2 * D_FOURIER), BF16)],
        compiler_params=_cparams(("arbitrary", "arbitrary")),
        name="fourier_mix",
    )(f, mst, c0, s0, cb, sb, g)
    return out.reshape(b * FFT_N2, FFT_N1 * D_FOURIER)


def _out_proj_body(a_ref, f_ref, x_ref, w_ref, g_ref, wbf_ref, fs_ref):
    @pl.when(pl.program_id(0) == 0)
    def _():
        wbf_ref[...] = w_ref[...].astype(BF16)

    n_blk = D_FOURIER // V7X_LANES
    for k1 in range(FFT_N1):
        for c in range(n_blk):
            col = k1 * D_FOURIER + c * V7X_LANES
            fs_ref[c, pl.ds(k1, TM // FFT_N1, stride=FFT_N1), :] = f_ref[:, col:col + V7X_LANES].astype(F32)
    fo = jnp.concatenate([fs_ref[c] for c in range(n_blk)], axis=1).astype(BF16)
    y = (jnp.dot(a_ref[...], wbf_ref[:D_ATTN, :], preferred_element_type=F32)
         + jnp.dot(fo, wbf_ref[D_ATTN:, :], preferred_element_type=F32))
    xn = x_ref[...] + y
    return xn, _rms(xn, g_ref[...])


def _out_proj_dense_kernel(a_ref, f_ref, x_ref, w_ref, g_ref, xo_ref, h_ref, wbf_ref, fs_ref):
    xn, h = _out_proj_body(a_ref, f_ref, x_ref, w_ref, g_ref, wbf_ref, fs_ref)
    xo_ref[...] = xn
    h_ref[...] = h.astype(BF16)


def _out_proj_moe_kernel(a_ref, f_ref, x_ref, w_ref, g_ref, wr_ref, lt_ref,
                         xo_ref, hrt_ref, gate_ref, rank_ref, cnt_ref,
                         wbf_ref, fs_ref, wr2_ref, hs_ref, carry_ref):
    @pl.when(pl.program_id(0) == 0)
    def _():
        hi, lo = _split_bf16(wr_ref[...])
        wr2_ref[:, :V7X_LANES] = hi
        wr2_ref[:, V7X_LANES:] = lo
        carry_ref[...] = jnp.zeros(carry_ref.shape, F32)

    xn, h = _out_proj_body(a_ref, f_ref, x_ref, w_ref, g_ref, wbf_ref, fs_ref)
    xo_ref[...] = xn
    for s in range(RT):
        hs_ref[pl.ds(s, TM, stride=RT), :] = h[:, s * V7X_LANES:(s + 1) * V7X_LANES]
    hrt_ref[...] = hs_ref[...].astype(BF16)
    h_hi, h_lo = _split_bf16(h)
    both = (jnp.dot(h_hi, wr2_ref[...], preferred_element_type=F32)
            + jnp.dot(h_lo, wr2_ref[...], preferred_element_type=F32))
    logits = both[:, :V7X_LANES] + both[:, V7X_LANES:]
    lane = lax.broadcasted_iota(jnp.int32, logits.shape, 1).astype(F32)
    logits = jnp.where(lane < N_EXPERTS, logits, -jnp.inf)
    m1 = jnp.max(logits, axis=-1, keepdims=True)
    i1 = jnp.min(jnp.where(logits == m1, lane, float(V7X_LANES)), axis=-1, keepdims=True)
    rest = jnp.where(lane == i1, -jnp.inf, logits)
    m2 = jnp.max(rest, axis=-1, keepdims=True)
    i2 = jnp.min(jnp.where(rest == m2, lane, float(V7X_LANES)), axis=-1, keepdims=True)
    e2 = jnp.exp(m2 - m1)
    den = 1.0 + e2
    sel1, sel2 = lane == i1, lane == i2
    sel = sel1 | sel2
    gate_ref[...] = jnp.where(sel1, 1.0 / den, jnp.where(sel2, e2 / den, 0.0))
    self32 = sel.astype(F32)
    rank = jnp.dot(lt_ref[...], self32.astype(BF16), preferred_element_type=F32) + carry_ref[...]
    rank_ref[...] = jnp.where(sel, rank, -1.0)
    carry_ref[...] = carry_ref[...] + jnp.sum(self32, axis=0, keepdims=True)
    cnt_ref[...] = carry_ref[...]


def _out_proj(l, a, f, x, w, g, moe=None):
    t = x.shape[0]
    row = lambda i: (i, 0)
    fixed = lambda i: (0, 0)
    in_specs = [
        pl.BlockSpec((TM, D_ATTN), row),
        pl.BlockSpec((TM // FFT_N1, FFT_N1 * D_FOURIER), row),
        pl.BlockSpec((TM, D_MODEL), row),
        pl.BlockSpec((None, D_MODEL, D_MODEL), lambda i: (l, 0, 0)),
        _layer_vec(D_MODEL, l),
    ]
    if moe is None:
        return pl.pallas_call(
            _out_proj_dense_kernel,
            grid=(t // TM,),
            in_specs=in_specs,
            out_specs=[pl.BlockSpec((TM, D_MODEL), row), pl.BlockSpec((TM, D_MODEL), row)],
            out_shape=[jax.ShapeDtypeStruct((t, D_MODEL), F32), jax.ShapeDtypeStruct((t, D_MODEL), BF16)],
            scratch_shapes=[pltpu.VMEM((D_MODEL, D_MODEL), BF16), pltpu.VMEM((D_FOURIER // V7X_LANES, TM, V7X_LANES), F32)],
            compiler_params=_cparams(("arbitrary",)),
            name="out_proj",
        )(a, f, x, w, g)
    li, w_router_pad, ltri = moe
    return pl.pallas_call(
        _out_proj_moe_kernel,
        grid=(t // TM,),
        in_specs=in_specs + [pl.BlockSpec((None, D_MODEL, V7X_LANES), lambda i: (li, 0, 0)),
                             pl.BlockSpec((TM, TM), fixed)],
        out_specs=[
            pl.BlockSpec((TM, D_MODEL), row),
            pl.BlockSpec((TM * RT, V7X_LANES), row),
            pl.BlockSpec((TM, V7X_LANES), row),
            pl.BlockSpec((TM, V7X_LANES), row),
            pl.BlockSpec((1, V7X_LANES), fixed),
        ],
        out_shape=[
            jax.ShapeDtypeStruct((t, D_MODEL), F32),
            jax.ShapeDtypeStruct((t * RT, V7X_LANES), BF16),
            jax.ShapeDtypeStruct((t, V7X_LANES), F32),
            jax.ShapeDtypeStruct((t, V7X_LANES), F32),
            jax.ShapeDtypeStruct((1, V7X_LANES), F32),
        ],
        scratch_shapes=[
            pltpu.VMEM((D_MODEL, D_MODEL), BF16),
            pltpu.VMEM((D_FOURIER // V7X_LANES, TM, V7X_LANES), F32),
            pltpu.VMEM((D_MODEL, 2 * V7X_LANES), BF16),
            pltpu.VMEM((TM * RT, V7X_LANES), F32),
            pltpu.VMEM((1, V7X_LANES), F32),
        ],
        compiler_params=_cparams(("arbitrary",)),
        name="out_proj_route",
    )(a, f, x, w, g, w_router_pad, ltri)


def _ffn_kernel(x_hbm, h_ref, wg_ref, wu_ref, wd_ref, o_ref, act_ref, sem):
    first = pl.program_id(1) == 0
    rows = pl.ds(pl.multiple_of(pl.program_id(0) * TM_FFN, TM_FFN), TM_FFN)
    x_copy = pltpu.make_async_copy(x_hbm.at[rows], o_ref, sem)

    @pl.when(first)
    def _():
        x_copy.start()

    h = h_ref[...]
    for c in range(TF_FFN // V7X_MXU):
        cols = slice(c * V7X_MXU, (c + 1) * V7X_MXU)
        g = jnp.dot(h, wg_ref[:, cols].astype(BF16), preferred_element_type=F32)
        u = jnp.dot(h, wu_ref[:, cols].astype(BF16), preferred_element_type=F32)
        act_ref[:, cols] = (g * jax.nn.sigmoid(g) * u).astype(BF16)

    @pl.when(first)
    def _():
        x_copy.wait()

    o_ref[...] += jnp.dot(act_ref[...], wd_ref[...].astype(BF16), preferred_element_type=F32)


def _ffn(li, x, h, wg, wu, wd):
    t = x.shape[0]
    return pl.pallas_call(
        _ffn_kernel,
        grid=(t // TM_FFN, D_FF // TF_FFN),
        in_specs=[
            pl.BlockSpec(memory_space=pl.ANY),
            pl.BlockSpec((TM_FFN, D_MODEL), lambda i, j: (i, 0)),
            pl.BlockSpec((None, D_MODEL, TF_FFN), lambda i, j: (li, 0, j)),
            pl.BlockSpec((None, D_MODEL, TF_FFN), lambda i, j: (li, 0, j)),
            pl.BlockSpec((None, TF_FFN, D_MODEL), lambda i, j: (li, j, 0)),
        ],
        out_specs=pl.BlockSpec((TM_FFN, D_MODEL), lambda i, j: (i, 0)),
        out_shape=jax.ShapeDtypeStruct((t, D_MODEL), F32),
        scratch_shapes=[pltpu.VMEM((TM_FFN, TF_FFN), BF16), pltpu.SemaphoreType.DMA(())],
        compiler_params=_cparams(("arbitrary", "arbitrary")),
        name="ffn_dense",
    )(x, h, wg, wu, wd)


def _plan_kernel(gate_ref, rank_ref, cnt_ref, pos_ref, wgt_ref, meta_ref):
    lane = lax.broadcasted_iota(jnp.int32, (1, V7X_LANES), 1)
    cnt = cnt_ref[...]
    padded = jnp.floor((cnt + (TM_MOE - 1)) * (1.0 / TM_MOE)) * TM_MOE
    base = jnp.zeros_like(padded)
    for j in range(1, N_EXPERTS):
        base = base + jnp.where(lane >= j, pltpu.roll(padded, j, 1), 0.0)
    rank = rank_ref[...]
    pos_t = jnp.where(rank >= 0.0, base + rank, -1.0).T
    gate_t = gate_ref[...].T
    seen = jnp.zeros((1, TM_PLAN), F32)
    pos_lo = jnp.zeros((1, TM_PLAN), F32)
    pos_hi = jnp.zeros((1, TM_PLAN), F32)
    w_lo = jnp.zeros((1, TM_PLAN), F32)
    w_hi = jnp.zeros((1, TM_PLAN), F32)
    for e in range(N_EXPERTS):
        p = pos_t[e:e + 1, :]
        gt = gate_t[e:e + 1, :]
        chosen = p >= 0.0
        first = chosen & (seen == 0.0)
        second = chosen & (seen == 1.0)
        pos_lo = jnp.where(first, p, pos_lo)
        pos_hi = jnp.where(second, p, pos_hi)
        w_lo = jnp.where(first, gt, w_lo)
        w_hi = jnp.where(second, gt, w_hi)
        seen = seen + chosen.astype(F32)
    pos_ref[0:1, :] = pos_lo.astype(jnp.int32)
    pos_ref[1:2, :] = pos_hi.astype(jnp.int32)
    wgt_ref[0:1, :] = w_lo
    wgt_ref[1:2, :] = w_hi
    ends = base + padded
    tile_start = (lane * TM_MOE).astype(F32)
    tile_e = jnp.zeros((1, V7X_LANES), F32)
    for e in range(N_EXPERTS):
        tile_e = tile_e + (ends[:, e:e + 1] <= tile_start).astype(F32)
    tile_e = jnp.minimum(tile_e, float(N_EXPERTS - 1))
    n_used = ends[:, N_EXPERTS - 1:N_EXPERTS] * (1.0 / TM_MOE) + jnp.zeros((1, V7X_LANES), F32)
    meta_ref[0:1, :] = tile_e.astype(jnp.int32)
    meta_ref[1:2, :] = n_used.astype(jnp.int32)
    meta_ref[2:3, :] = cnt.astype(jnp.int32)
    meta_ref[3:4, :] = base.astype(jnp.int32)
    meta_ref[4:8, :] = jnp.zeros((4, V7X_LANES), jnp.int32)


def _plan(gates, rank, counts):
    t = gates.shape[0]
    return pl.pallas_call(
        _plan_kernel,
        grid=(t // TM_PLAN,),
        in_specs=[
            pl.BlockSpec((TM_PLAN, V7X_LANES), lambda i: (i, 0)),
            pl.BlockSpec((TM_PLAN, V7X_LANES), lambda i: (i, 0)),
            pl.BlockSpec((1, V7X_LANES), lambda i: (0, 0)),
        ],
        out_specs=[
            pl.BlockSpec((2, TM_PLAN), lambda i: (0, i)),
            pl.BlockSpec((2, TM_PLAN), lambda i: (0, i)),
            pl.BlockSpec((8, V7X_LANES), lambda i: (0, 0)),
        ],
        out_shape=[
            jax.ShapeDtypeStruct((2, t), jnp.int32),
            jax.ShapeDtypeStruct((2, t), F32),
            jax.ShapeDtypeStruct((8, V7X_LANES), jnp.int32),
        ],
        compiler_params=_cparams(("arbitrary",)),
        name="moe_plan",
    )(gates, rank, counts)


def _rt_rows(ref, row, n=1, per=RT):
    start = row * per if isinstance(row, int) else pl.multiple_of(row * per, per)
    return ref.at[pl.ds(start, n * per)]


def _dispatch_kernel(plo_ref, phi_ref, cnt_ref, base_ref, nt_ref, h_ref, o_ref, z_ref, sem):
    i = pl.program_id(0)
    t0 = i * TM

    def issue(r, carry):
        src = _rt_rows(h_ref, r)
        pltpu.make_async_copy(src, _rt_rows(o_ref, plo_ref[t0 + r]), sem).start(priority=0)
        pltpu.make_async_copy(src, _rt_rows(o_ref, phi_ref[t0 + r]), sem).start(priority=1)
        return carry

    lax.fori_loop(0, TM, issue, 0, unroll=DMA_UNROLL)
    for _ in range(2):
        pltpu.make_async_copy(h_ref, _rt_rows(o_ref, 0, TM), sem).wait()

    @pl.when(i == pl.num_programs(0) - 1)
    def _():
        z_ref[...] = jnp.zeros(z_ref.shape, z_ref.dtype)
        for e in range(N_EXPERTS):
            n = cnt_ref[e]
            n_pad = ((n + (TM_MOE - 1)) // TM_MOE) * TM_MOE - n
            first = base_ref[e] + n

            def pad_issue(r, carry, first=first):
                pltpu.make_async_copy(_rt_rows(z_ref, 0), _rt_rows(o_ref, first + r), sem).start()
                return carry

            def pad_drain(r, carry):
                pltpu.make_async_copy(_rt_rows(z_ref, 0), _rt_rows(o_ref, 0), sem).wait()
                return carry

            lax.fori_loop(0, n_pad, pad_issue, 0)
            lax.fori_loop(0, n_pad, pad_drain, 0)

        def tail(tile, carry):
            cp = pltpu.make_async_copy(z_ref, _rt_rows(o_ref, tile * TM_MOE, TM_MOE), sem)
            cp.start()
            cp.wait()
            return carry

        lax.fori_loop(nt_ref[0], o_ref.shape[0] // (TM_MOE * RT), tail, 0)


def _dispatch(pos_lo, pos_hi, cnt, base, n_used, h_rt, n_rows):
    t = h_rt.shape[0] // RT
    return pl.pallas_call(
        _dispatch_kernel,
        grid_spec=pltpu.PrefetchScalarGridSpec(
            num_scalar_prefetch=5,
            grid=(t // TM,),
            in_specs=[pl.BlockSpec((TM * RT, V7X_LANES), lambda i, *_: (i, 0))],
            out_specs=pl.BlockSpec(memory_space=pl.ANY),
            scratch_shapes=[pltpu.VMEM((TM_MOE * RT, V7X_LANES), BF16), pltpu.SemaphoreType.DMA(())],
        ),
        out_shape=jax.ShapeDtypeStruct((n_rows * RT, V7X_LANES), BF16),
        compiler_params=_cparams(("arbitrary",)),
        name="moe_dispatch",
    )(pos_lo, pos_hi, cnt, base, n_used, h_rt)


def _tile_of(i, nt_ref):
    return jnp.minimum(i, nt_ref[0] - 1)


def _moe_up_kernel(te_ref, nt_ref, hs_ref, wg_ref, wu_ref, o_ref, xs_ref):
    i = pl.program_id(1)

    @pl.when(i < nt_ref[0])
    def _():
        xs_ref[...] = hs_ref[...].astype(F32)
        x = jnp.concatenate([xs_ref[pl.ds(s, TM_MOE, stride=RT), :].astype(BF16) for s in range(RT)],
                            axis=1)
        for c in range(TF_UP // V7X_MXU):
            cols = slice(c * V7X_MXU, (c + 1) * V7X_MXU)
            g = jnp.dot(x, wg_ref[:, cols].astype(BF16), preferred_element_type=F32)
            u = jnp.dot(x, wu_ref[:, cols].astype(BF16), preferred_element_type=F32)
            o_ref[:, cols] = (g * jax.nn.sigmoid(g) * u).astype(BF16)

    @pl.when(i >= nt_ref[0])
    def _():
        o_ref[...] = jnp.zeros(o_ref.shape, o_ref.dtype)


def _moe_up(li, tile_e, n_used, hs, wg, wu):
    n_rows = hs.shape[0] // RT
    nt = n_rows // TM_MOE
    wspec = pl.BlockSpec((None, None, D_MODEL, TF_UP), lambda j, i, te, n: (li, te[_tile_of(i, n)], 0, j))
    return pl.pallas_call(
        _moe_up_kernel,
        grid_spec=pltpu.PrefetchScalarGridSpec(
            num_scalar_prefetch=2,
            grid=(D_FF // TF_UP, nt),
            in_specs=[pl.BlockSpec((TM_MOE * RT, V7X_LANES), lambda j, i, te, n: (_tile_of(i, n), 0)), wspec, wspec],
            out_specs=pl.BlockSpec((TM_MOE, TF_UP), lambda j, i, te, n: (i, j)),
            scratch_shapes=[pltpu.VMEM((TM_MOE * RT, V7X_LANES), F32)],
        ),
        out_shape=jax.ShapeDtypeStruct((n_rows, D_FF), BF16),
        compiler_params=_cparams(("arbitrary", "arbitrary")),
        name="moe_up",
    )(tile_e, n_used, hs, wg, wu)


def _moe_down_kernel(te_ref, nt_ref, a_ref, wd_ref, o_ref):
    i = pl.program_id(0)

    @pl.when(i < nt_ref[0])
    def _():
        a = a_ref[...]
        for c in range(D_MODEL // V7X_MXU):
            y = jnp.dot(a, wd_ref[:, c * V7X_MXU:(c + 1) * V7X_MXU].astype(BF16), preferred_element_type=F32)
            for k in range(V7X_MXU // V7X_LANES):
                s = c * (V7X_MXU // V7X_LANES) + k
                o_ref[pl.ds(s, TM_MOE, stride=RT), :] = y[:, k * V7X_LANES:(k + 1) * V7X_LANES]

    @pl.when(i >= nt_ref[0])
    def _():
        o_ref[...] = jnp.zeros(o_ref.shape, o_ref.dtype)


def _moe_down(li, tile_e, n_used, act, wd):
    n_rows = act.shape[0]
    nt = n_rows // TM_MOE
    return pl.pallas_call(
        _moe_down_kernel,
        grid_spec=pltpu.PrefetchScalarGridSpec(
            num_scalar_prefetch=2,
            grid=(nt,),
            in_specs=[
                pl.BlockSpec((TM_MOE, D_FF), lambda i, te, n: (_tile_of(i, n), 0)),
                pl.BlockSpec((None, None, D_FF, D_MODEL), lambda i, te, n: (li, te[_tile_of(i, n)], 0, 0)),
            ],
            out_specs=pl.BlockSpec((TM_MOE * RT, V7X_LANES), lambda i, te, n: (i, 0)),
        ),
        out_shape=jax.ShapeDtypeStruct((n_rows * RT, V7X_LANES), F32),
        compiler_params=_cparams(("arbitrary",)),
        name="moe_down",
    )(tile_e, n_used, act, wd)


def _combine_kernel(plo_ref, phi_ref, x_ref, wlo_ref, whi_ref, y_ref, o_ref, lo_ref, hi_ref, sem):
    t0 = pl.program_id(0) * TM

    def issue(r, carry):
        pltpu.make_async_copy(_rt_rows(y_ref, plo_ref[t0 + r]), _rt_rows(lo_ref, r), sem).start(priority=0)
        pltpu.make_async_copy(_rt_rows(y_ref, phi_ref[t0 + r]), _rt_rows(hi_ref, r), sem).start(priority=1)
        return carry

    lax.fori_loop(0, TM, issue, 0, unroll=DMA_UNROLL)
    pltpu.make_async_copy(_rt_rows(y_ref, 0, TM), lo_ref, sem).wait()
    pltpu.make_async_copy(_rt_rows(y_ref, 0, TM), hi_ref, sem).wait()
    w_lo, w_hi = wlo_ref[...], whi_ref[...]
    for s in range(RT):
        cols = slice(s * V7X_LANES, (s + 1) * V7X_LANES)
        o_ref[:, cols] = x_ref[:, cols] + (w_lo * lo_ref[pl.ds(s, TM, stride=RT), :]
                                           + w_hi * hi_ref[pl.ds(s, TM, stride=RT), :])


def _combine(pos_lo, pos_hi, x, w_lo, w_hi, ys):
    t = x.shape[0]
    return pl.pallas_call(
        _combine_kernel,
        grid_spec=pltpu.PrefetchScalarGridSpec(
            num_scalar_prefetch=2,
            grid=(t // TM,),
            in_specs=[
                pl.BlockSpec((TM, D_MODEL), lambda i, a, b: (i, 0)),
                pl.BlockSpec((TM, 1), lambda i, a, b: (i, 0)),
                pl.BlockSpec((TM, 1), lambda i, a, b: (i, 0)),
                pl.BlockSpec(memory_space=pl.ANY),
            ],
            out_specs=pl.BlockSpec((TM, D_MODEL), lambda i, a, b: (i, 0)),
            scratch_shapes=[pltpu.VMEM((TM * RT, V7X_LANES), F32), pltpu.VMEM((TM * RT, V7X_LANES), F32),
                            pltpu.SemaphoreType.DMA(())],
        ),
        out_shape=jax.ShapeDtypeStruct((t, D_MODEL), F32),
        compiler_params=_cparams(("arbitrary",)),
        name="moe_combine",
    )(pos_lo, pos_hi, x, w_lo, w_hi, ys)


def _moe(li, x, h_rt, gates, rank, counts, e_gate, e_up, e_down):
    t = x.shape[0]
    n_rows = ((2 * t) // TM_MOE + N_EXPERTS) * TM_MOE
    pos, wgt, meta = _plan(gates, rank, counts)
    pos_lo, pos_hi = pos[0], pos[1]
    tile_e, n_used = meta[0], meta[1, :1]
    cnt, base = meta[2, :N_EXPERTS], meta[3, :N_EXPERTS]
    hs = _dispatch(pos_lo, pos_hi, cnt, base, n_used, h_rt, n_rows)
    act = _moe_up(li, tile_e, n_used, hs, e_gate, e_up)
    ys = _moe_down(li, tile_e, n_used, act, e_down)
    return _combine(pos_lo, pos_hi, x, wgt[0].reshape(t, 1), wgt[1].reshape(t, 1), ys)


def _rope_tables(positions):
    inv_freq = ROPE_THETA ** (-jnp.arange(0, ROT_DIM, 2, dtype=F32) / ROT_DIM)
    ang = positions.astype(F32).reshape(-1, 1) * inv_freq
    cos, sin = jnp.cos(ang), jnp.sin(ang)
    t = cos.shape[0]
    half = ROT_DIM // 2
    ones = jnp.ones((t, HEAD_DIM - ROT_DIM), F32)
    zeros = jnp.zeros((t, HEAD_DIM - ROT_DIM), F32)
    zh = jnp.zeros((t, half), F32)
    reps = V7X_LANES // HEAD_DIM
    cos_t = jnp.tile(jnp.concatenate([cos, cos, ones], axis=1), (1, reps))
    sin_a = jnp.tile(jnp.concatenate([zh, sin, zeros], axis=1), (1, reps))
    sin_b = jnp.tile(jnp.concatenate([-sin, zh, zeros], axis=1), (1, reps))
    return cos_t, sin_a, sin_b


def _dft_tables(seq):
    n2 = jnp.arange(FFT_N2, dtype=jnp.int32)
    m = (n2[:, None] * n2[None, :]) % FFT_N2
    ang = m.astype(F32) * (2.0 * np.pi / FFT_N2)
    c0, s0 = jnp.cos(ang).astype(BF16), jnp.sin(ang).astype(BF16)
    k1 = jnp.arange(FFT_N1, dtype=jnp.int32)
    beta = (k1[:, None] * n2[None, :]).astype(F32) * (2.0 * np.pi / seq)
    cb = jnp.broadcast_to(jnp.cos(beta)[:, :, None], (FFT_N1, FFT_N2, V7X_LANES))
    sb = jnp.broadcast_to(jnp.sin(beta)[:, :, None], (FFT_N1, FFT_N2, V7X_LANES))
    c = jnp.arange(FOURIER_GROUP, dtype=jnp.int32)
    angc = ((c[:, None] * c[None, :]) % FOURIER_GROUP).astype(F32) * (2.0 * np.pi / FOURIER_GROUP)
    return c0, s0, cb, sb, jnp.cos(angc), jnp.sin(angc)


def kernel(x, positions, norm_mix, w_in, q_norm, k_norm, sink, w_fmix, g_attn_out, g_fourier_out, w_out, norm_ffn,
           ffn_gate, ffn_up, ffn_down, w_router, e_gate, e_up, e_down):
    b, s, d = x.shape
    depth = w_in.shape[0]
    assert (d, s) == (D_MODEL, FFT_N1 * FFT_N2) and (b * s) % TM_PLAN == 0
    t = b * s
    cos_t, sin_a, sin_b = _rope_tables(positions)
    c0, s0, cb, sb, cc, sc = _dft_tables(s)
    mst = _fmix_prep(w_fmix, cc, sc)
    head = jnp.arange(D_QK) // HEAD_DIM
    member = head[:, None] == jnp.arange(V7X_LANES)[None, :]
    head_sum = member.astype(BF16) * (1.0 / HEAD_DIM)
    head_expand = member.T.astype(BF16)
    row_i = jnp.arange(TM)
    ltri = (row_i[None, :] < row_i[:, None]).astype(BF16)
    gain = jnp.concatenate([jnp.tile(q_norm, (1, N_Q_HEADS)) * (HEAD_DIM ** -0.5),
                            jnp.tile(k_norm, (1, N_KV_HEADS))], axis=1).reshape(depth, 1, D_QK)
    vec = lambda p: p.reshape(depth, 1, p.shape[-1])
    norm_mix, norm_ffn, g_attn_out, g_fourier_out = vec(norm_mix), vec(norm_ffn), vec(g_attn_out), vec(g_fourier_out)
    wr = jnp.pad(w_router, ((0, 0), (0, 0), (0, V7X_LANES - N_EXPERTS)))
    xt = x.reshape(t, d)
    for l in range(depth):
        q, k, v, f = _in_proj(l, xt, norm_mix, w_in, head_sum, head_expand, gain, cos_t, sin_a, sin_b)
        a = _attention(l, q.reshape(b, s, D_ATTN), k.reshape(b, s, D_KV), v.reshape(b, s, D_KV), sink, g_attn_out)
        fo = _fourier(l, f.reshape(b, s, D_FOURIER), mst, c0, s0, cb, sb, g_fourier_out)
        a2, f2 = a.reshape(t, D_ATTN), fo
        li = l // 2
        if l % 2 == 0:
            xt, h = _out_proj(l, a2, f2, xt, w_out, norm_ffn)
            xt = _ffn(li, xt, h, ffn_gate, ffn_up, ffn_down)
        else:
            xt, h_rt, gates, rank, counts = _out_proj(l, a2, f2, xt, w_out, norm_ffn, moe=(li, wr, ltri))
            xt = _moe(li, xt, h_rt, gates, rank, counts, e_gate, e_up, e_down)
    return xt.reshape(b, s, d)
```

```python
import functools

import jax
import jax.numpy as jnp
import numpy as np
from jax import lax
from jax.experimental import pallas as pl
from jax.experimental.pallas import tpu as pltpu

F32 = jnp.float32
BF16 = jnp.bfloat16

D_MODEL = 1024
HEAD_DIM = 64
D_ATTN = 512
N_Q_HEADS = 8
N_KV_HEADS = 2
Q_PER_KV = 4
D_KV = 128
D_FOURIER = 512
N_FOURIER_GROUPS = 8
FOURIER_GROUP = 64
D_QK = D_ATTN + D_KV
D_IN = D_ATTN + 2 * D_KV + D_FOURIER
WINDOW = 128
ROPE_THETA = 500000.0
ROT_DIM = 16
D_FF = 3584
N_EXPERTS = 8
EPS = 1e-6

V7X_LANES = 128
V7X_SUBLANES = 8
V7X_MXU = 256
V7X_VMEM_LIMIT = 56 * 1024 * 1024
RT = D_MODEL // V7X_LANES
assert RT == V7X_SUBLANES

TM = 1024
TQ = 1024
QB = 128
QB_PER_ITER = 1
KV_WIN = QB + 2 * WINDOW
FFT_N1 = 4
FFT_N2 = 1024
FFT_CHUNK = 128
TM_FFN = 1024
TF_FFN = 512
TM_MOE = 512
TF_UP = D_FF // 2
TM_PLAN = 1024
DMA_UNROLL = 8


def _cparams(sem, vmem=V7X_VMEM_LIMIT):
    return pltpu.CompilerParams(dimension_semantics=sem, vmem_limit_bytes=vmem)


def _rms(xf, g):
    ms = jnp.mean(xf * xf, axis=-1, keepdims=True)
    return xf * lax.rsqrt(ms + EPS) * g


def _split_bf16(xf):
    hi = xf.astype(BF16)
    lo = (xf - hi.astype(F32)).astype(BF16)
    return hi, lo


def _layer_vec(n, l):
    return pl.BlockSpec((None, 1, n), lambda *_: (l, 0, 0))


def _in_proj_kernel(x_ref, g_ref, w_ref, hs_ref, he_ref, gain_ref, cos_ref, sa_ref, sb_ref,
                    q_ref, k_ref, v_ref, f_ref, wbf_ref):
    @pl.when(pl.program_id(0) == 0)
    def _():
        wbf_ref[...] = w_ref[...].astype(BF16)

    h = _rms(x_ref[...], g_ref[...]).astype(BF16)
    z = jnp.dot(h, wbf_ref[...], preferred_element_type=F32)
    qk = z[:, :D_QK]
    hi, lo = _split_bf16(qk * qk)
    ms = (jnp.dot(hi, hs_ref[...], preferred_element_type=F32)
          + jnp.dot(lo, hs_ref[...], preferred_element_type=F32))
    r_hi, r_lo = _split_bf16(lax.rsqrt(ms + EPS))
    inv_rms = (jnp.dot(r_hi, he_ref[...], preferred_element_type=F32)
               + jnp.dot(r_lo, he_ref[...], preferred_element_type=F32))
    qkn = qk * inv_rms * gain_ref[...]
    cos_t, sin_a, sin_b = cos_ref[...], sa_ref[...], sb_ref[...]
    for c in range(D_QK // V7X_LANES):
        blk = qkn[:, c * V7X_LANES:(c + 1) * V7X_LANES]
        rot = (blk * cos_t + pltpu.roll(blk, ROT_DIM // 2, 1) * sin_a
               + pltpu.roll(blk, V7X_LANES - ROT_DIM // 2, 1) * sin_b).astype(BF16)
        if c < D_ATTN // V7X_LANES:
            q_ref[:, c * V7X_LANES:(c + 1) * V7X_LANES] = rot
        else:
            k_ref[...] = rot
    v_ref[...] = z[:, D_QK:D_QK + D_KV].astype(BF16)
    f_ref[...] = z[:, D_QK + D_KV:].astype(BF16)


def _in_proj(l, x, g, w, head_sum, head_expand, gain, cos_t, sin_a, sin_b):
    t = x.shape[0]
    row = lambda i: (i, 0)
    fixed = lambda i: (0, 0)
    return pl.pallas_call(
        _in_proj_kernel,
        grid=(t // TM,),
        in_specs=[
            pl.BlockSpec((TM, D_MODEL), row),
            _layer_vec(D_MODEL, l),
            pl.BlockSpec((None, D_MODEL, D_IN), lambda i: (l, 0, 0)),
            pl.BlockSpec((D_QK, V7X_LANES), fixed),
            pl.BlockSpec((V7X_LANES, D_QK), fixed),
            _layer_vec(D_QK, l),
            pl.BlockSpec((TM, V7X_LANES), row),
            pl.BlockSpec((TM, V7X_LANES), row),
            pl.BlockSpec((TM, V7X_LANES), row),
        ],
        out_specs=[
            pl.BlockSpec((TM, D_ATTN), row),
            pl.BlockSpec((TM, D_KV), row),
            pl.BlockSpec((TM, D_KV), row),
            pl.BlockSpec((TM, D_FOURIER), row),
        ],
        out_shape=[
            jax.ShapeDtypeStruct((t, D_ATTN), BF16),
            jax.ShapeDtypeStruct((t, D_KV), BF16),
            jax.ShapeDtypeStruct((t, D_KV), BF16),
            jax.ShapeDtypeStruct((t, D_FOURIER), BF16),
        ],
        scratch_shapes=[pltpu.VMEM((D_MODEL, D_IN), BF16)],
        compiler_params=_cparams(("arbitrary",)),
        name="in_proj",
    )(x, g, w, head_sum, head_expand, gain, cos_t, sin_a, sin_b)


def _attn_kernel(sink_ref, q_ref, k_ref, v_ref, g_ref, o_ref, s_ref, p_ref, *, seq, layer):
    q0 = pl.program_id(1) * TQ
    gain = g_ref[...]
    heads = range(N_Q_HEADS)

    def blocks(j, carry):
        units = [(b, h) for b in range(QB_PER_ITER) for h in heads]
        r0, kw, vw, mask, q = [], [], [], [], []
        for b in range(QB_PER_ITER):
            r = pl.multiple_of((j * QB_PER_ITER + b) * QB, QB)
            qs = q0 + r
            ws = pl.multiple_of(jnp.clip(qs - WINDOW, 0, seq - KV_WIN), QB)
            r0.append(r)
            kw.append(k_ref[pl.ds(ws, KV_WIN), :])
            vw.append(v_ref[pl.ds(ws, KV_WIN), :])
            q_abs = qs + lax.broadcasted_iota(jnp.int32, (QB, KV_WIN), 0)
            k_abs = ws + lax.broadcasted_iota(jnp.int32, (QB, KV_WIN), 1)
            mask.append(jnp.abs(q_abs - k_abs) <= WINDOW)
            q.append(q_ref[pl.ds(r, QB), :])
        for b, h in units:
            kvh = h // Q_PER_KV
            qh = q[b][:, h * HEAD_DIM:(h + 1) * HEAD_DIM]
            kh = kw[b][:, kvh * HEAD_DIM:(kvh + 1) * HEAD_DIM]
            s = lax.dot_general(qh, kh, (((1,), (1,)), ((), ())), preferred_element_type=F32)
            s_ref[b, h] = jnp.where(mask[b], s, -jnp.inf)
        m = {(b, h): jnp.maximum(jnp.max(s_ref[b, h], axis=-1, keepdims=True), sink_ref[layer, h])
             for b, h in units}
        inv = {}
        for b, h in units:
            p = jnp.exp(s_ref[b, h] - m[b, h])
            inv[b, h] = 1.0 / (jnp.sum(p, axis=-1, keepdims=True) + jnp.exp(sink_ref[layer, h] - m[b, h]))
            p_ref[b, h] = p.astype(BF16)
        for b in range(QB_PER_ITER):
            outs = []
            for h in heads:
                kvh = h // Q_PER_KV
                vh = vw[b][:, kvh * HEAD_DIM:(kvh + 1) * HEAD_DIM]
                outs.append(jnp.dot(p_ref[b, h], vh, preferred_element_type=F32) * inv[b, h])
            a = jnp.concatenate(outs, axis=-1)
            o_ref[pl.ds(r0[b], QB), :] = _rms(a, gain).astype(BF16)
        return carry

    lax.fori_loop(0, TQ // (QB * QB_PER_ITER), blocks, 0)


def _attention(l, q, k, v, sink, g):
    b, s, _ = q.shape
    return pl.pallas_call(
        functools.partial(_attn_kernel, seq=s, layer=l),
        grid=(b, s // TQ),
        in_specs=[
            pl.BlockSpec(memory_space=pltpu.SMEM),
            pl.BlockSpec((None, TQ, D_ATTN), lambda i, j: (i, j, 0)),
            pl.BlockSpec((None, s, D_KV), lambda i, j: (i, 0, 0)),
            pl.BlockSpec((None, s, D_KV), lambda i, j: (i, 0, 0)),
            _layer_vec(D_ATTN, l),
        ],
        out_specs=pl.BlockSpec((None, TQ, D_ATTN), lambda i, j: (i, j, 0)),
        out_shape=jax.ShapeDtypeStruct((b, s, D_ATTN), BF16),
        scratch_shapes=[pltpu.VMEM((QB_PER_ITER, N_Q_HEADS, QB, KV_WIN), F32),
                        pltpu.VMEM((QB_PER_ITER, N_Q_HEADS, QB, KV_WIN), BF16)],
        compiler_params=_cparams(("arbitrary", "arbitrary")),
        name="band_attn",
    )(sink, q, k, v, g)


def _fmix_prep_kernel(w_ref, cc_ref, sc_ref, o_ref):
    o_ref[...] = jnp.zeros(o_ref.shape, o_ref.dtype)
    scale = float(FFT_N1 * FFT_N2 * FOURIER_GROUP) ** -0.5
    for g in range(N_FOURIER_GROUPS):
        w = w_ref[g]
        mr = jnp.dot(cc_ref[...], w, preferred_element_type=F32, precision=lax.Precision.HIGHEST) * scale
        mi = jnp.dot(sc_ref[...], w, preferred_element_type=F32, precision=lax.Precision.HIGHEST) * (-scale)
        c = (g * FOURIER_GROUP) // FFT_CHUNK
        off = (g * FOURIER_GROUP) % FFT_CHUNK
        rows = slice(g * FOURIER_GROUP, (g + 1) * FOURIER_GROUP)
        o_ref[c, rows, off:off + FOURIER_GROUP] = mr.astype(BF16)
        o_ref[c, rows, FFT_CHUNK + off:FFT_CHUNK + off + FOURIER_GROUP] = mi.astype(BF16)


def _fmix_prep(w_fmix, cc, sc):
    depth = w_fmix.shape[0]
    n_chunk = D_FOURIER // FFT_CHUNK
    return pl.pallas_call(
        _fmix_prep_kernel,
        grid=(depth,),
        in_specs=[
            pl.BlockSpec((None, N_FOURIER_GROUPS, FOURIER_GROUP, FOURIER_GROUP), lambda l: (l, 0, 0, 0)),
            pl.BlockSpec((FOURIER_GROUP, FOURIER_GROUP), lambda l: (0, 0)),
            pl.BlockSpec((FOURIER_GROUP, FOURIER_GROUP), lambda l: (0, 0)),
        ],
        out_specs=pl.BlockSpec((None, n_chunk, D_FOURIER, 2 * FFT_CHUNK), lambda l: (l, 0, 0, 0)),
        out_shape=jax.ShapeDtypeStruct((depth, n_chunk, D_FOURIER, 2 * FFT_CHUNK), BF16),
        compiler_params=_cparams(("arbitrary",)),
        name="fmix_prep",
    )(w_fmix, cc, sc)


def _fourier_kernel(f_ref, m_ref, c0_ref, s0_ref, cb_ref, sb_ref, g_ref, o_ref, v_ref):
    k1 = pl.program_id(1)

    @pl.when(k1 == 0)
    def _():
        for c in range(D_FOURIER // FFT_CHUNK):
            mc = m_ref[c]
            z = [jnp.dot(f_ref[n1 * FFT_N2:(n1 + 1) * FFT_N2, :], mc, preferred_element_type=F32)
                 for n1 in range(FFT_N1)]
            zr = [t[:, :FFT_CHUNK] for t in z]
            zi = [t[:, FFT_CHUNK:] for t in z]
            ar, ai = zr[0] + zr[2], zi[0] + zi[2]
            br, bi = zr[1] + zr[3], zi[1] + zi[3]
            cr, ci = zr[0] - zr[2], zi[0] - zi[2]
            dr, di = zr[1] - zr[3], zi[1] - zi[3]
            u = [(ar + br, ai + bi), (cr + di, ci - dr), (ar - br, ai - bi), (cr - di, ci + dr)]
            for kk in range(FFT_N1):
                ur, ui = u[kk]
                cb, sb = cb_ref[kk], sb_ref[kk]
                v_ref[kk, :, c * FFT_CHUNK:(c + 1) * FFT_CHUNK] = (ur * cb + ui * sb).astype(BF16)
                v_ref[kk, :, D_FOURIER + c * FFT_CHUNK:D_FOURIER + (c + 1) * FFT_CHUNK] = (
                    ui * cb - ur * sb).astype(BF16)

    vr = v_ref[k1, :, :D_FOURIER]
    vi = v_ref[k1, :, D_FOURIER:]
    out = (jnp.dot(c0_ref[...], vr, preferred_element_type=F32)
           + jnp.dot(s0_ref[...], vi, preferred_element_type=F32))
    o_ref[...] = _rms(out, g_ref[...]).astype(BF16)


def _fourier(l, f, mst, c0, s0, cb, sb, g):
    b, s, _ = f.shape
    n_chunk = D_FOURIER // FFT_CHUNK
    out = pl.pallas_call(
        _fourier_kernel,
        grid=(b, FFT_N1),
        in_specs=[
            pl.BlockSpec((None, s, D_FOURIER), lambda i, j: (i, 0, 0)),
            pl.BlockSpec((None, n_chunk, D_FOURIER, 2 * FFT_CHUNK), lambda i, j: (l, 0, 0, 0)),
            pl.BlockSpec((FFT_N2, FFT_N2), lambda i, j: (0, 0)),
            pl.BlockSpec((FFT_N2, FFT_N2), lambda i, j: (0, 0)),
            pl.BlockSpec((FFT_N1, FFT_N2, V7X_LANES), lambda i, j: (0, 0, 0)),
            pl.BlockSpec((FFT_N1, FFT_N2, V7X_LANES), lambda i, j: (0, 0, 0)),
            _layer_vec(D_FOURIER, l),
        ],
        out_specs=pl.BlockSpec((None, FFT_N2, D_FOURIER), lambda i, j: (i, 0, j)),
        out_shape=jax.ShapeDtypeStruct((b, FFT_N2, FFT_N1 * D_FOURIER), BF16),
        scratch_shapes=[pltpu.VMEM((FFT_N1, FFT_N2, 2 * D_FOURIER), BF16)],
        compiler_params=_cparams(("arbitrary", "arbitrary")),
        name="fourier_mix",
    )(f, mst, c0, s0, cb, sb, g)
    return out.reshape(b * FFT_N2, FFT_N1 * D_FOURIER)


def _out_proj_body(a_ref, f_ref, x_ref, w_ref, g_ref, wbf_ref, fs_ref):
    @pl.when(pl.program_id(0) == 0)
    def _():
        wbf_ref[...] = w_ref[...].astype(BF16)

    return _out_proj_tile(a_ref, f_ref, x_ref, g_ref, wbf_ref, fs_ref)


def _out_proj_tile(a_ref, f_ref, x_ref, g_ref, wbf_ref, fs_ref):
    n_blk = D_FOURIER // V7X_LANES
    rows = x_ref.shape[0]
    for k1 in range(FFT_N1):
        for c in range(n_blk):
            col = k1 * D_FOURIER + c * V7X_LANES
            fs_ref[c, pl.ds(k1, rows // FFT_N1, stride=FFT_N1), :] = f_ref[:, col:col + V7X_LANES].astype(F32)
    fo = jnp.concatenate([fs_ref[c] for c in range(n_blk)], axis=1).astype(BF16)
    y = (jnp.dot(a_ref[...], wbf_ref[:D_ATTN, :], preferred_element_type=F32)
         + jnp.dot(fo, wbf_ref[D_ATTN:, :], preferred_element_type=F32))
    xn = x_ref[...] + y
    return xn, _rms(xn, g_ref[...])


def _out_proj_dense_kernel(a_ref, f_ref, x_ref, w_ref, g_ref, xo_ref, h_ref, wbf_ref, fs_ref):
    xn, h = _out_proj_body(a_ref, f_ref, x_ref, w_ref, g_ref, wbf_ref, fs_ref)
    xo_ref[...] = xn
    h_ref[...] = h.astype(BF16)


def _out_proj_moe_kernel(a_ref, f_ref, x_ref, w_ref, g_ref, wr_ref, lt_ref,
                         xo_ref, hrt_ref, gate_ref, rank_ref, cnt_ref,
                         wbf_ref, fs_ref, wr2_ref, hs_ref, carry_ref):
    @pl.when(pl.program_id(0) == 0)
    def _():
        hi, lo = _split_bf16(wr_ref[...])
        wr2_ref[:, :V7X_LANES] = hi
        wr2_ref[:, V7X_LANES:] = lo
        carry_ref[...] = jnp.zeros(carry_ref.shape, F32)

    xn, h = _out_proj_body(a_ref, f_ref, x_ref, w_ref, g_ref, wbf_ref, fs_ref)
    xo_ref[...] = xn
    for s in range(RT):
        hs_ref[pl.ds(s, TM, stride=RT), :] = h[:, s * V7X_LANES:(s + 1) * V7X_LANES]
    hrt_ref[...] = hs_ref[...].astype(BF16)
    h_hi, h_lo = _split_bf16(h)
    both = (jnp.dot(h_hi, wr2_ref[...], preferred_element_type=F32)
            + jnp.dot(h_lo, wr2_ref[...], preferred_element_type=F32))
    logits = both[:, :V7X_LANES] + both[:, V7X_LANES:]
    lane = lax.broadcasted_iota(jnp.int32, logits.shape, 1).astype(F32)
    logits = jnp.where(lane < N_EXPERTS, logits, -jnp.inf)
    m1 = jnp.max(logits, axis=-1, keepdims=True)
    i1 = jnp.min(jnp.where(logits == m1, lane, float(V7X_LANES)), axis=-1, keepdims=True)
    rest = jnp.where(lane == i1, -jnp.inf, logits)
    m2 = jnp.max(rest, axis=-1, keepdims=True)
    i2 = jnp.min(jnp.where(rest == m2, lane, float(V7X_LANES)), axis=-1, keepdims=True)
    e2 = jnp.exp(m2 - m1)
    den = 1.0 + e2
    sel1, sel2 = lane == i1, lane == i2
    sel = sel1 | sel2
    gate_ref[...] = jnp.where(sel1, 1.0 / den, jnp.where(sel2, e2 / den, 0.0))
    self32 = sel.astype(F32)
    rank = jnp.dot(lt_ref[...], self32.astype(BF16), preferred_element_type=F32) + carry_ref[...]
    rank_ref[...] = jnp.where(sel, rank, -1.0)
    carry_ref[...] = carry_ref[...] + jnp.sum(self32, axis=0, keepdims=True)
    cnt_ref[...] = carry_ref[...]


def _out_proj_specs(l):
    row = lambda i: (i, 0)
    return [
        pl.BlockSpec((TM, D_ATTN), row),
        pl.BlockSpec((TM // FFT_N1, FFT_N1 * D_FOURIER), row),
        pl.BlockSpec((TM, D_MODEL), row),
        pl.BlockSpec((None, D_MODEL, D_MODEL), lambda i: (l, 0, 0)),
        _layer_vec(D_MODEL, l),
    ]


def _out_proj(l, a, f, x, w, g):
    t = x.shape[0]
    row = lambda i: (i, 0)
    return pl.pallas_call(
        _out_proj_dense_kernel,
        grid=(t // TM,),
        in_specs=_out_proj_specs(l),
        out_specs=[pl.BlockSpec((TM, D_MODEL), row), pl.BlockSpec((TM, D_MODEL), row)],
        out_shape=[jax.ShapeDtypeStruct((t, D_MODEL), F32), jax.ShapeDtypeStruct((t, D_MODEL), BF16)],
        scratch_shapes=[pltpu.VMEM((D_MODEL, D_MODEL), BF16), pltpu.VMEM((D_FOURIER // V7X_LANES, TM, V7X_LANES), F32)],
        compiler_params=_cparams(("arbitrary",)),
        name="out_proj",
    )(a, f, x, w, g)


def _out_proj_route(l, li, a, f, x, w, g, w_router_pad, ltri):
    t = x.shape[0]
    row = lambda i: (i, 0)
    fixed = lambda i: (0, 0)
    in_specs = _out_proj_specs(l)
    return pl.pallas_call(
        _out_proj_moe_kernel,
        grid=(t // TM,),
        in_specs=in_specs + [pl.BlockSpec((None, D_MODEL, V7X_LANES), lambda i: (li, 0, 0)),
                             pl.BlockSpec((TM, TM), fixed)],
        out_specs=[
            pl.BlockSpec((TM, D_MODEL), row),
            pl.BlockSpec((TM * RT, V7X_LANES), row),
            pl.BlockSpec((TM, V7X_LANES), row),
            pl.BlockSpec((TM, V7X_LANES), row),
            pl.BlockSpec((1, V7X_LANES), fixed),
        ],
        out_shape=[
            jax.ShapeDtypeStruct((t, D_MODEL), F32),
            jax.ShapeDtypeStruct((t * RT, V7X_LANES), BF16),
            jax.ShapeDtypeStruct((t, V7X_LANES), F32),
            jax.ShapeDtypeStruct((t, V7X_LANES), F32),
            jax.ShapeDtypeStruct((1, V7X_LANES), F32),
        ],
        scratch_shapes=[
            pltpu.VMEM((D_MODEL, D_MODEL), BF16),
            pltpu.VMEM((D_FOURIER // V7X_LANES, TM, V7X_LANES), F32),
            pltpu.VMEM((D_MODEL, 2 * V7X_LANES), BF16),
            pltpu.VMEM((TM * RT, V7X_LANES), F32),
            pltpu.VMEM((1, V7X_LANES), F32),
        ],
        compiler_params=_cparams(("arbitrary",)),
        name="out_proj_route",
    )(a, f, x, w, g, w_router_pad, ltri)


def _ffn_kernel(x_ref, h_ref, wg_ref, wu_ref, wd_ref, o_ref, act_ref):
    @pl.when(pl.program_id(1) == 0)
    def _():
        o_ref[...] = x_ref[...]

    h = h_ref[...]
    for c in range(TF_FFN // V7X_MXU):
        cols = slice(c * V7X_MXU, (c + 1) * V7X_MXU)
        g = jnp.dot(h, wg_ref[:, cols].astype(BF16), preferred_element_type=F32)
        u = jnp.dot(h, wu_ref[:, cols].astype(BF16), preferred_element_type=F32)
        act_ref[:, cols] = (g * jax.nn.sigmoid(g) * u).astype(BF16)
    o_ref[...] += jnp.dot(act_ref[...], wd_ref[...].astype(BF16), preferred_element_type=F32)


def _ffn(li, x, h, wg, wu, wd):
    t = x.shape[0]
    return pl.pallas_call(
        _ffn_kernel,
        grid=(t // TM_FFN, D_FF // TF_FFN),
        in_specs=[
            pl.BlockSpec((TM_FFN, D_MODEL), lambda i, j: (i, 0)),
            pl.BlockSpec((TM_FFN, D_MODEL), lambda i, j: (i, 0)),
            pl.BlockSpec((None, D_MODEL, TF_FFN), lambda i, j: (li, 0, j)),
            pl.BlockSpec((None, D_MODEL, TF_FFN), lambda i, j: (li, 0, j)),
            pl.BlockSpec((None, TF_FFN, D_MODEL), lambda i, j: (li, j, 0)),
        ],
        out_specs=pl.BlockSpec((TM_FFN, D_MODEL), lambda i, j: (i, 0)),
        out_shape=jax.ShapeDtypeStruct((t, D_MODEL), F32),
        scratch_shapes=[pltpu.VMEM((TM_FFN, TF_FFN), BF16)],
        compiler_params=_cparams(("arbitrary", "arbitrary")),
        name="ffn_dense",
    )(x, h, wg, wu, wd)


def _plan_kernel(gate_ref, rank_ref, cnt_ref, pos_ref, wgt_ref, meta_ref):
    lane = lax.broadcasted_iota(jnp.int32, (1, V7X_LANES), 1)
    cnt = cnt_ref[...]
    padded = jnp.floor((cnt + (TM_MOE - 1)) * (1.0 / TM_MOE)) * TM_MOE
    base = jnp.zeros_like(padded)
    for j in range(1, N_EXPERTS):
        base = base + jnp.where(lane >= j, pltpu.roll(padded, j, 1), 0.0)
    rank = rank_ref[...]
    pos_t = jnp.where(rank >= 0.0, base + rank, -1.0).T
    gate_t = gate_ref[...].T
    seen = jnp.zeros((1, TM_PLAN), F32)
    pos_lo = jnp.zeros((1, TM_PLAN), F32)
    pos_hi = jnp.zeros((1, TM_PLAN), F32)
    w_lo = jnp.zeros((1, TM_PLAN), F32)
    w_hi = jnp.zeros((1, TM_PLAN), F32)
    for e in range(N_EXPERTS):
        p = pos_t[e:e + 1, :]
        gt = gate_t[e:e + 1, :]
        chosen = p >= 0.0
        first = chosen & (seen == 0.0)
        second = chosen & (seen == 1.0)
        pos_lo = jnp.where(first, p, pos_lo)
        pos_hi = jnp.where(second, p, pos_hi)
        w_lo = jnp.where(first, gt, w_lo)
        w_hi = jnp.where(second, gt, w_hi)
        seen = seen + chosen.astype(F32)
    pos_ref[0:1, :] = pos_lo.astype(jnp.int32)
    pos_ref[1:2, :] = pos_hi.astype(jnp.int32)
    wgt_ref[0:1, :] = w_lo
    wgt_ref[1:2, :] = w_hi
    ends = base + padded
    tile_start = (lane * TM_MOE).astype(F32)
    tile_e = jnp.zeros((1, V7X_LANES), F32)
    for e in range(N_EXPERTS):
        tile_e = tile_e + (ends[:, e:e + 1] <= tile_start).astype(F32)
    tile_e = jnp.minimum(tile_e, float(N_EXPERTS - 1))
    n_used = ends[:, N_EXPERTS - 1:N_EXPERTS] * (1.0 / TM_MOE) + jnp.zeros((1, V7X_LANES), F32)
    meta_ref[0:1, :] = tile_e.astype(jnp.int32)
    meta_ref[1:2, :] = n_used.astype(jnp.int32)
    meta_ref[2:3, :] = cnt.astype(jnp.int32)
    meta_ref[3:4, :] = base.astype(jnp.int32)
    meta_ref[4:8, :] = jnp.zeros((4, V7X_LANES), jnp.int32)


def _plan(gates, rank, counts):
    t = gates.shape[0]
    return pl.pallas_call(
        _plan_kernel,
        grid=(t // TM_PLAN,),
        in_specs=[
            pl.BlockSpec((TM_PLAN, V7X_LANES), lambda i: (i, 0)),
            pl.BlockSpec((TM_PLAN, V7X_LANES), lambda i: (i, 0)),
            pl.BlockSpec((1, V7X_LANES), lambda i: (0, 0)),
        ],
        out_specs=[
            pl.BlockSpec((2, TM_PLAN), lambda i: (0, i)),
            pl.BlockSpec((2, TM_PLAN), lambda i: (0, i)),
            pl.BlockSpec((8, V7X_LANES), lambda i: (0, 0)),
        ],
        out_shape=[
            jax.ShapeDtypeStruct((2, t), jnp.int32),
            jax.ShapeDtypeStruct((2, t), F32),
            jax.ShapeDtypeStruct((8, V7X_LANES), jnp.int32),
        ],
        compiler_params=_cparams(("arbitrary",)),
        name="moe_plan",
    )(gates, rank, counts)


def _rt_rows(ref, row, n=1, per=RT):
    start = row * per if isinstance(row, int) else pl.multiple_of(row * per, per)
    return ref.at[pl.ds(start, n * per)]


def _dispatch_kernel(plo_ref, phi_ref, cnt_ref, base_ref, nt_ref, h_ref, o_ref, z_ref, sem):
    i = pl.program_id(0)
    t0 = i * TM

    def issue(r, carry):
        src = _rt_rows(h_ref, r)
        pltpu.make_async_copy(src, _rt_rows(o_ref, plo_ref[t0 + r]), sem).start(priority=0)
        pltpu.make_async_copy(src, _rt_rows(o_ref, phi_ref[t0 + r]), sem).start(priority=1)
        return carry

    lax.fori_loop(0, TM, issue, 0, unroll=DMA_UNROLL)
    for _ in range(2):
        pltpu.make_async_copy(h_ref, _rt_rows(o_ref, 0, TM), sem).wait()

    @pl.when(i == pl.num_programs(0) - 1)
    def _():
        z_ref[...] = jnp.zeros(z_ref.shape, z_ref.dtype)
        for e in range(N_EXPERTS):
            n = cnt_ref[e]
            n_pad = ((n + (TM_MOE - 1)) // TM_MOE) * TM_MOE - n
            first = base_ref[e] + n

            def pad_issue(r, carry, first=first):
                pltpu.make_async_copy(_rt_rows(z_ref, 0), _rt_rows(o_ref, first + r), sem).start()
                return carry

            def pad_drain(r, carry):
                pltpu.make_async_copy(_rt_rows(z_ref, 0), _rt_rows(o_ref, 0), sem).wait()
                return carry

            lax.fori_loop(0, n_pad, pad_issue, 0)
            lax.fori_loop(0, n_pad, pad_drain, 0)

        def tail(tile, carry):
            cp = pltpu.make_async_copy(z_ref, _rt_rows(o_ref, tile * TM_MOE, TM_MOE), sem)
            cp.start()
            cp.wait()
            return carry

        lax.fori_loop(nt_ref[0], o_ref.shape[0] // (TM_MOE * RT), tail, 0)


def _dispatch(pos_lo, pos_hi, cnt, base, n_used, h_rt, n_rows):
    t = h_rt.shape[0] // RT
    return pl.pallas_call(
        _dispatch_kernel,
        grid_spec=pltpu.PrefetchScalarGridSpec(
            num_scalar_prefetch=5,
            grid=(t // TM,),
            in_specs=[pl.BlockSpec((TM * RT, V7X_LANES), lambda i, *_: (i, 0))],
            out_specs=pl.BlockSpec(memory_space=pl.ANY),
            scratch_shapes=[pltpu.VMEM((TM_MOE * RT, V7X_LANES), BF16), pltpu.SemaphoreType.DMA(())],
        ),
        out_shape=jax.ShapeDtypeStruct((n_rows * RT, V7X_LANES), BF16),
        compiler_params=_cparams(("arbitrary",)),
        name="moe_dispatch",
    )(pos_lo, pos_hi, cnt, base, n_used, h_rt)


def _tile_of(i, nt_ref):
    return jnp.minimum(i, nt_ref[0] - 1)


def _moe_up_kernel(te_ref, nt_ref, hs_ref, wg_ref, wu_ref, o_ref, xs_ref):
    i = pl.program_id(1)

    @pl.when(i < nt_ref[0])
    def _():
        xs_ref[...] = hs_ref[...].astype(F32)
        x = jnp.concatenate([xs_ref[pl.ds(s, TM_MOE, stride=RT), :].astype(BF16) for s in range(RT)],
                            axis=1)
        for c in range(TF_UP // V7X_MXU):
            cols = slice(c * V7X_MXU, (c + 1) * V7X_MXU)
            g = jnp.dot(x, wg_ref[:, cols].astype(BF16), preferred_element_type=F32)
            u = jnp.dot(x, wu_ref[:, cols].astype(BF16), preferred_element_type=F32)
            o_ref[:, cols] = (g * jax.nn.sigmoid(g) * u).astype(BF16)

    @pl.when(i >= nt_ref[0])
    def _():
        o_ref[...] = jnp.zeros(o_ref.shape, o_ref.dtype)


def _moe_up(li, tile_e, n_used, hs, wg, wu):
    n_rows = hs.shape[0] // RT
    nt = n_rows // TM_MOE
    wspec = pl.BlockSpec((None, None, D_MODEL, TF_UP), lambda j, i, te, n: (li, te[_tile_of(i, n)], 0, j))
    return pl.pallas_call(
        _moe_up_kernel,
        grid_spec=pltpu.PrefetchScalarGridSpec(
            num_scalar_prefetch=2,
            grid=(D_FF // TF_UP, nt),
            in_specs=[pl.BlockSpec((TM_MOE * RT, V7X_LANES), lambda j, i, te, n: (_tile_of(i, n), 0)), wspec, wspec],
            out_specs=pl.BlockSpec((TM_MOE, TF_UP), lambda j, i, te, n: (i, j)),
            scratch_shapes=[pltpu.VMEM((TM_MOE * RT, V7X_LANES), F32)],
        ),
        out_shape=jax.ShapeDtypeStruct((n_rows, D_FF), BF16),
        compiler_params=_cparams(("arbitrary", "arbitrary")),
        name="moe_up",
    )(tile_e, n_used, hs, wg, wu)


def _moe_down_kernel(te_ref, nt_ref, a_ref, wd_ref, o_ref):
    i = pl.program_id(0)

    @pl.when(i < nt_ref[0])
    def _():
        a = a_ref[...]
        for c in range(D_MODEL // V7X_MXU):
            y = jnp.dot(a, wd_ref[:, c * V7X_MXU:(c + 1) * V7X_MXU].astype(BF16), preferred_element_type=F32)
            for k in range(V7X_MXU // V7X_LANES):
                s = c * (V7X_MXU // V7X_LANES) + k
                o_ref[pl.ds(s, TM_MOE, stride=RT), :] = y[:, k * V7X_LANES:(k + 1) * V7X_LANES]

    @pl.when(i >= nt_ref[0])
    def _():
        o_ref[...] = jnp.zeros(o_ref.shape, o_ref.dtype)


def _moe_down(li, tile_e, n_used, act, wd):
    n_rows = act.shape[0]
    nt = n_rows // TM_MOE
    return pl.pallas_call(
        _moe_down_kernel,
        grid_spec=pltpu.PrefetchScalarGridSpec(
            num_scalar_prefetch=2,
            grid=(nt,),
            in_specs=[
                pl.BlockSpec((TM_MOE, D_FF), lambda i, te, n: (_tile_of(i, n), 0)),
                pl.BlockSpec((None, None, D_FF, D_MODEL), lambda i, te, n: (li, te[_tile_of(i, n)], 0, 0)),
            ],
            out_specs=pl.BlockSpec((TM_MOE * RT, V7X_LANES), lambda i, te, n: (i, 0)),
        ),
        out_shape=jax.ShapeDtypeStruct((n_rows * RT, V7X_LANES), F32),
        compiler_params=_cparams(("arbitrary",)),
        name="moe_down",
    )(tile_e, n_used, act, wd)


def _combine_kernel(plo_ref, phi_ref, x_ref, wlo_ref, whi_ref, y_ref, o_ref, lo_ref, hi_ref, sem):
    t0 = pl.program_id(0) * TM

    def issue(r, carry):
        pltpu.make_async_copy(_rt_rows(y_ref, plo_ref[t0 + r]), _rt_rows(lo_ref, r), sem).start(priority=0)
        pltpu.make_async_copy(_rt_rows(y_ref, phi_ref[t0 + r]), _rt_rows(hi_ref, r), sem).start(priority=1)
        return carry

    lax.fori_loop(0, TM, issue, 0, unroll=DMA_UNROLL)
    pltpu.make_async_copy(_rt_rows(y_ref, 0, TM), lo_ref, sem).wait()
    pltpu.make_async_copy(_rt_rows(y_ref, 0, TM), hi_ref, sem).wait()
    w_lo, w_hi = wlo_ref[...], whi_ref[...]
    for s in range(RT):
        cols = slice(s * V7X_LANES, (s + 1) * V7X_LANES)
        o_ref[:, cols] = x_ref[:, cols] + (w_lo * lo_ref[pl.ds(s, TM, stride=RT), :]
                                           + w_hi * hi_ref[pl.ds(s, TM, stride=RT), :])


def _combine(pos_lo, pos_hi, x, w_lo, w_hi, ys):
    t = x.shape[0]
    return pl.pallas_call(
        _combine_kernel,
        grid_spec=pltpu.PrefetchScalarGridSpec(
            num_scalar_prefetch=2,
            grid=(t // TM,),
            in_specs=[
                pl.BlockSpec((TM, D_MODEL), lambda i, a, b: (i, 0)),
                pl.BlockSpec((TM, 1), lambda i, a, b: (i, 0)),
                pl.BlockSpec((TM, 1), lambda i, a, b: (i, 0)),
                pl.BlockSpec(memory_space=pl.ANY),
            ],
            out_specs=pl.BlockSpec((TM, D_MODEL), lambda i, a, b: (i, 0)),
            scratch_shapes=[pltpu.VMEM((TM * RT, V7X_LANES), F32), pltpu.VMEM((TM * RT, V7X_LANES), F32),
                            pltpu.SemaphoreType.DMA(())],
        ),
        out_shape=jax.ShapeDtypeStruct((t, D_MODEL), F32),
        compiler_params=_cparams(("arbitrary",)),
        name="moe_combine",
    )(pos_lo, pos_hi, x, w_lo, w_hi, ys)


def _moe(li, x, h_rt, gates, rank, counts, e_gate, e_up, e_down):
    t = x.shape[0]
    n_rows = ((2 * t) // TM_MOE + N_EXPERTS) * TM_MOE
    pos, wgt, meta = _plan(gates, rank, counts)
    pos_lo, pos_hi = pos[0], pos[1]
    tile_e, n_used = meta[0], meta[1, :1]
    cnt, base = meta[2, :N_EXPERTS], meta[3, :N_EXPERTS]
    hs = _dispatch(pos_lo, pos_hi, cnt, base, n_used, h_rt, n_rows)
    act = _moe_up(li, tile_e, n_used, hs, e_gate, e_up)
    ys = _moe_down(li, tile_e, n_used, act, e_down)
    return _combine(pos_lo, pos_hi, x, wgt[0].reshape(t, 1), wgt[1].reshape(t, 1), ys)


def _rope_tables(positions):
    inv_freq = ROPE_THETA ** (-jnp.arange(0, ROT_DIM, 2, dtype=F32) / ROT_DIM)
    ang = positions.astype(F32).reshape(-1, 1) * inv_freq
    cos, sin = jnp.cos(ang), jnp.sin(ang)
    t = cos.shape[0]
    half = ROT_DIM // 2
    ones = jnp.ones((t, HEAD_DIM - ROT_DIM), F32)
    zeros = jnp.zeros((t, HEAD_DIM - ROT_DIM), F32)
    zh = jnp.zeros((t, half), F32)
    reps = V7X_LANES // HEAD_DIM
    cos_t = jnp.tile(jnp.concatenate([cos, cos, ones], axis=1), (1, reps))
    sin_a = jnp.tile(jnp.concatenate([zh, sin, zeros], axis=1), (1, reps))
    sin_b = jnp.tile(jnp.concatenate([-sin, zh, zeros], axis=1), (1, reps))
    return cos_t, sin_a, sin_b


def _dft_tables(seq):
    n2 = jnp.arange(FFT_N2, dtype=jnp.int32)
    m = (n2[:, None] * n2[None, :]) % FFT_N2
    ang = m.astype(F32) * (2.0 * np.pi / FFT_N2)
    c0, s0 = jnp.cos(ang).astype(BF16), jnp.sin(ang).astype(BF16)
    k1 = jnp.arange(FFT_N1, dtype=jnp.int32)
    beta = (k1[:, None] * n2[None, :]).astype(F32) * (2.0 * np.pi / seq)
    cb = jnp.broadcast_to(jnp.cos(beta)[:, :, None], (FFT_N1, FFT_N2, V7X_LANES))
    sb = jnp.broadcast_to(jnp.sin(beta)[:, :, None], (FFT_N1, FFT_N2, V7X_LANES))
    c = jnp.arange(FOURIER_GROUP, dtype=jnp.int32)
    angc = ((c[:, None] * c[None, :]) % FOURIER_GROUP).astype(F32) * (2.0 * np.pi / FOURIER_GROUP)
    return c0, s0, cb, sb, jnp.cos(angc), jnp.sin(angc)


def kernel(x, positions, norm_mix, w_in, q_norm, k_norm, sink, w_fmix, g_attn_out, g_fourier_out, w_out, norm_ffn,
           ffn_gate, ffn_up, ffn_down, w_router, e_gate, e_up, e_down):
    b, s, d = x.shape
    depth = w_in.shape[0]
    assert (d, s) == (D_MODEL, FFT_N1 * FFT_N2) and (b * s) % TM_PLAN == 0
    t = b * s
    cos_t, sin_a, sin_b = _rope_tables(positions)
    c0, s0, cb, sb, cc, sc = _dft_tables(s)
    mst = _fmix_prep(w_fmix, cc, sc)
    head = jnp.arange(D_QK) // HEAD_DIM
    member = head[:, None] == jnp.arange(V7X_LANES)[None, :]
    head_sum = member.astype(BF16) * (1.0 / HEAD_DIM)
    head_expand = member.T.astype(BF16)
    row_i = jnp.arange(TM)
    ltri = (row_i[None, :] < row_i[:, None]).astype(BF16)
    gain = jnp.concatenate([jnp.tile(q_norm, (1, N_Q_HEADS)) * (HEAD_DIM ** -0.5),
                            jnp.tile(k_norm, (1, N_KV_HEADS))], axis=1).reshape(depth, 1, D_QK)
    vec = lambda p: p.reshape(depth, 1, p.shape[-1])
    norm_mix, norm_ffn, g_attn_out, g_fourier_out = vec(norm_mix), vec(norm_ffn), vec(g_attn_out), vec(g_fourier_out)
    wr = jnp.pad(w_router, ((0, 0), (0, 0), (0, V7X_LANES - N_EXPERTS)))
    xt = x.reshape(t, d)
    for l in range(depth):
        q, k, v, f = _in_proj(l, xt, norm_mix, w_in, head_sum, head_expand, gain, cos_t, sin_a, sin_b)
        a = _attention(l, q.reshape(b, s, D_ATTN), k.reshape(b, s, D_KV), v.reshape(b, s, D_KV), sink, g_attn_out)
        fo = _fourier(l, f.reshape(b, s, D_FOURIER), mst, c0, s0, cb, sb, g_fourier_out)
        a2, f2 = a.reshape(t, D_ATTN), fo
        li = l // 2
        if l % 2 == 0:
            xt, h = _out_proj(l, a2, f2, xt, w_out, norm_ffn)
            xt = _ffn(li, xt, h, ffn_gate, ffn_up, ffn_down)
        else:
            xt, h_rt, gates, rank, counts = _out_proj_route(l, li, a2, f2, xt, w_out, norm_ffn, wr, ltri)
            xt = _moe(li, xt, h_rt, gates, rank, counts, e_gate, e_up, e_down)
    return xt.reshape(b, s, d)
```

```python
import functools

import jax
import jax.numpy as jnp
import numpy as np
from jax import lax
from jax.experimental import pallas as pl
from jax.experimental.pallas import tpu as pltpu

F32 = jnp.float32
BF16 = jnp.bfloat16

D_MODEL = 1024
HEAD_DIM = 64
D_ATTN = 512
N_Q_HEADS = 8
N_KV_HEADS = 2
Q_PER_KV = 4
D_KV = 128
D_FOURIER = 512
N_FOURIER_GROUPS = 8
FOURIER_GROUP = 64
D_QK = D_ATTN + D_KV
D_IN = D_ATTN + 2 * D_KV + D_FOURIER
WINDOW = 128
ROPE_THETA = 500000.0
ROT_DIM = 16
D_FF = 3584
N_EXPERTS = 8
EPS = 1e-6

V7X_LANES = 128
V7X_SUBLANES = 8
V7X_MXU = 256
V7X_VMEM_LIMIT = 56 * 1024 * 1024
RT = D_MODEL // V7X_LANES
assert RT == V7X_SUBLANES

TM = 1024
TQ = 1024
QB = 128
QB_PER_ITER = 1
KV_WIN = QB + 2 * WINDOW
FFT_N1 = 4
FFT_N2 = 1024
FFT_CHUNK = 128
TM_FFN = 1024
TF_FFN = 512
TM_MOE = 512
TF_UP = D_FF // 2
TM_PLAN = 1024
TM_CMB = 512
DMA_UNROLL = 8


def _cparams(sem, vmem=V7X_VMEM_LIMIT):
    return pltpu.CompilerParams(dimension_semantics=sem, vmem_limit_bytes=vmem)


def _rms(xf, g):
    ms = jnp.mean(xf * xf, axis=-1, keepdims=True)
    return xf * lax.rsqrt(ms + EPS) * g


def _split_bf16(xf):
    hi = xf.astype(BF16)
    lo = (xf - hi.astype(F32)).astype(BF16)
    return hi, lo


def _layer_vec(n, l):
    return pl.BlockSpec((None, 1, n), lambda *_: (l, 0, 0))


def _in_proj_kernel(x_ref, g_ref, w_ref, hs_ref, he_ref, gain_ref, cos_ref, sa_ref, sb_ref,
                    q_ref, k_ref, v_ref, f_ref, wbf_ref):
    @pl.when(pl.program_id(0) == 0)
    def _():
        wbf_ref[...] = w_ref[...].astype(BF16)

    _in_proj_tile(x_ref[...], g_ref, hs_ref, he_ref, gain_ref, cos_ref, sa_ref, sb_ref,
                  q_ref, k_ref, v_ref, f_ref, wbf_ref)


def _in_proj_tile(x, g_ref, hs_ref, he_ref, gain_ref, cos_ref, sa_ref, sb_ref, q_ref, k_ref, v_ref, f_ref, wbf_ref):
    h = _rms(x, g_ref[...]).astype(BF16)
    z = jnp.dot(h, wbf_ref[...], preferred_element_type=F32)
    qk = z[:, :D_QK]
    hi, lo = _split_bf16(qk * qk)
    ms = (jnp.dot(hi, hs_ref[...], preferred_element_type=F32)
          + jnp.dot(lo, hs_ref[...], preferred_element_type=F32))
    r_hi, r_lo = _split_bf16(lax.rsqrt(ms + EPS))
    inv_rms = (jnp.dot(r_hi, he_ref[...], preferred_element_type=F32)
               + jnp.dot(r_lo, he_ref[...], preferred_element_type=F32))
    qkn = qk * inv_rms * gain_ref[...]
    cos_t, sin_a, sin_b = cos_ref[...], sa_ref[...], sb_ref[...]
    for c in range(D_QK // V7X_LANES):
        blk = qkn[:, c * V7X_LANES:(c + 1) * V7X_LANES]
        rot = (blk * cos_t + pltpu.roll(blk, ROT_DIM // 2, 1) * sin_a
               + pltpu.roll(blk, V7X_LANES - ROT_DIM // 2, 1) * sin_b).astype(BF16)
        if c < D_ATTN // V7X_LANES:
            q_ref[:, c * V7X_LANES:(c + 1) * V7X_LANES] = rot
        else:
            k_ref[...] = rot
    v_ref[...] = z[:, D_QK:D_QK + D_KV].astype(BF16)
    f_ref[...] = z[:, D_QK + D_KV:].astype(BF16)


def _in_proj_specs(l, tm, row, fixed):
    in_specs = [
        pl.BlockSpec((tm, D_MODEL), row),
        _layer_vec(D_MODEL, l),
        pl.BlockSpec((None, D_MODEL, D_IN), lambda i, *_: (l, 0, 0)),
        pl.BlockSpec((D_QK, V7X_LANES), fixed),
        pl.BlockSpec((V7X_LANES, D_QK), fixed),
        _layer_vec(D_QK, l),
        pl.BlockSpec((tm, V7X_LANES), row),
        pl.BlockSpec((tm, V7X_LANES), row),
        pl.BlockSpec((tm, V7X_LANES), row),
    ]
    out_specs = [
        pl.BlockSpec((tm, D_ATTN), row),
        pl.BlockSpec((tm, D_KV), row),
        pl.BlockSpec((tm, D_KV), row),
        pl.BlockSpec((tm, D_FOURIER), row),
    ]
    return in_specs, out_specs


def _in_proj_out_shapes(t):
    return [
        jax.ShapeDtypeStruct((t, D_ATTN), BF16),
        jax.ShapeDtypeStruct((t, D_KV), BF16),
        jax.ShapeDtypeStruct((t, D_KV), BF16),
        jax.ShapeDtypeStruct((t, D_FOURIER), BF16),
    ]


def _in_proj(l, x, g, w, head_sum, head_expand, gain, cos_t, sin_a, sin_b):
    t = x.shape[0]
    in_specs, out_specs = _in_proj_specs(l, TM, lambda i: (i, 0), lambda i: (0, 0))
    return pl.pallas_call(
        _in_proj_kernel,
        grid=(t // TM,),
        in_specs=in_specs,
        out_specs=out_specs,
        out_shape=_in_proj_out_shapes(t),
        scratch_shapes=[pltpu.VMEM((D_MODEL, D_IN), BF16)],
        compiler_params=_cparams(("arbitrary",)),
        name="in_proj",
    )(x, g, w, head_sum, head_expand, gain, cos_t, sin_a, sin_b)


def _combine_in_proj_kernel(plo_ref, phi_ref, x_ref, g_ref, w_ref, hs_ref, he_ref, gain_ref, cos_ref, sa_ref, sb_ref,
                            wlo_ref, whi_ref, y_ref, xo_ref, q_ref, k_ref, v_ref, f_ref,
                            wbf_ref, lo_ref, hi_ref, sems):
    i = pl.program_id(0)
    n = pl.num_programs(0)

    def gather(tile, slot):
        t0 = tile * TM_CMB
        for r in range(TM_CMB):
            pltpu.make_async_copy(_rt_rows(y_ref, plo_ref[t0 + r]), _rt_rows(lo_ref.at[slot], r),
                                  sems.at[slot]).start(priority=0)
            pltpu.make_async_copy(_rt_rows(y_ref, phi_ref[t0 + r]), _rt_rows(hi_ref.at[slot], r),
                                  sems.at[slot]).start(priority=1)

    def gather_wait(slot):
        pltpu.make_async_copy(_rt_rows(y_ref, 0, TM_CMB), lo_ref.at[slot], sems.at[slot]).wait()
        pltpu.make_async_copy(_rt_rows(y_ref, 0, TM_CMB), hi_ref.at[slot], sems.at[slot]).wait()

    @pl.when(i == 0)
    def _():
        wbf_ref[...] = w_ref[...].astype(BF16)

        def first(r, carry):
            pltpu.make_async_copy(_rt_rows(y_ref, plo_ref[r]), _rt_rows(lo_ref.at[0], r), sems.at[0]).start(priority=0)
            pltpu.make_async_copy(_rt_rows(y_ref, phi_ref[r]), _rt_rows(hi_ref.at[0], r), sems.at[0]).start(priority=1)
            return carry

        lax.fori_loop(0, TM_CMB, first, 0, unroll=DMA_UNROLL)

    slot = i % 2
    gather(jnp.minimum(i + 1, n - 1), 1 - slot)
    gather_wait(slot)
    w_lo, w_hi = wlo_ref[...], whi_ref[...]
    for s in range(RT):
        cols = slice(s * V7X_LANES, (s + 1) * V7X_LANES)
        xo_ref[:, cols] = x_ref[:, cols] + (w_lo * lo_ref[slot, pl.ds(s, TM_CMB, stride=RT), :]
                                             + w_hi * hi_ref[slot, pl.ds(s, TM_CMB, stride=RT), :])
    _in_proj_tile(xo_ref[...], g_ref, hs_ref, he_ref, gain_ref, cos_ref, sa_ref, sb_ref,
                  q_ref, k_ref, v_ref, f_ref, wbf_ref)

    @pl.when(i == n - 1)
    def _():
        gather_wait(1 - slot)


def _combine_in_proj(l, pending, g, w, head_sum, head_expand, gain, cos_t, sin_a, sin_b):
    pos_lo, pos_hi, x, w_lo, w_hi, ys = pending
    t = x.shape[0]
    row = lambda i, a, b: (i, 0)
    in_specs, out_specs = _in_proj_specs(l, TM_CMB, row, lambda i, a, b: (0, 0))
    return pl.pallas_call(
        _combine_in_proj_kernel,
        grid_spec=pltpu.PrefetchScalarGridSpec(
            num_scalar_prefetch=2,
            grid=(t // TM_CMB,),
            in_specs=in_specs + [pl.BlockSpec((TM_CMB, 1), row), pl.BlockSpec((TM_CMB, 1), row),
                                 pl.BlockSpec(memory_space=pl.ANY)],
            out_specs=[pl.BlockSpec((TM_CMB, D_MODEL), row)] + out_specs,
            scratch_shapes=[
                pltpu.VMEM((D_MODEL, D_IN), BF16),
                pltpu.VMEM((2, TM_CMB * RT, V7X_LANES), F32),
                pltpu.VMEM((2, TM_CMB * RT, V7X_LANES), F32),
                pltpu.SemaphoreType.DMA((2,)),
            ],
        ),
        out_shape=[jax.ShapeDtypeStruct((t, D_MODEL), F32)] + _in_proj_out_shapes(t),
        compiler_params=_cparams(("arbitrary",)),
        name="combine_in_proj",
    )(pos_lo, pos_hi, x, g, w, head_sum, head_expand, gain, cos_t, sin_a, sin_b, w_lo, w_hi, ys)


def _attn_kernel(sink_ref, q_ref, k_ref, v_ref, g_ref, o_ref, s_ref, p_ref, *, seq, layer):
    q0 = pl.program_id(1) * TQ
    gain = g_ref[...]
    heads = range(N_Q_HEADS)

    def blocks(j, carry):
        units = [(b, h) for b in range(QB_PER_ITER) for h in heads]
        r0, kw, vw, mask, q = [], [], [], [], []
        for b in range(QB_PER_ITER):
            r = pl.multiple_of((j * QB_PER_ITER + b) * QB, QB)
            qs = q0 + r
            ws = pl.multiple_of(jnp.clip(qs - WINDOW, 0, seq - KV_WIN), QB)
            r0.append(r)
            kw.append(k_ref[pl.ds(ws, KV_WIN), :])
            vw.append(v_ref[pl.ds(ws, KV_WIN), :])
            q_abs = qs + lax.broadcasted_iota(jnp.int32, (QB, KV_WIN), 0)
            k_abs = ws + lax.broadcasted_iota(jnp.int32, (QB, KV_WIN), 1)
            mask.append(jnp.abs(q_abs - k_abs) <= WINDOW)
            q.append(q_ref[pl.ds(r, QB), :])
        for b, h in units:
            kvh = h // Q_PER_KV
            qh = q[b][:, h * HEAD_DIM:(h + 1) * HEAD_DIM]
            kh = kw[b][:, kvh * HEAD_DIM:(kvh + 1) * HEAD_DIM]
            s = lax.dot_general(qh, kh, (((1,), (1,)), ((), ())), preferred_element_type=F32)
            s_ref[b, h] = jnp.where(mask[b], s, -jnp.inf)
        m = {(b, h): jnp.maximum(jnp.max(s_ref[b, h], axis=-1, keepdims=True), sink_ref[layer, h])
             for b, h in units}
        inv = {}
        for b, h in units:
            p = jnp.exp(s_ref[b, h] - m[b, h])
            inv[b, h] = 1.0 / (jnp.sum(p, axis=-1, keepdims=True) + jnp.exp(sink_ref[layer, h] - m[b, h]))
            p_ref[b, h] = p.astype(BF16)
        for b in range(QB_PER_ITER):
            outs = []
            for h in heads:
                kvh = h // Q_PER_KV
                vh = vw[b][:, kvh * HEAD_DIM:(kvh + 1) * HEAD_DIM]
                outs.append(jnp.dot(p_ref[b, h], vh, preferred_element_type=F32) * inv[b, h])
            a = jnp.concatenate(outs, axis=-1)
            o_ref[pl.ds(r0[b], QB), :] = _rms(a, gain).astype(BF16)
        return carry

    lax.fori_loop(0, TQ // (QB * QB_PER_ITER), blocks, 0)


def _attention(l, q, k, v, sink, g):
    b, s, _ = q.shape
    return pl.pallas_call(
        functools.partial(_attn_kernel, seq=s, layer=l),
        grid=(b, s // TQ),
        in_specs=[
            pl.BlockSpec(memory_space=pltpu.SMEM),
            pl.BlockSpec((None, TQ, D_ATTN), lambda i, j: (i, j, 0)),
            pl.BlockSpec((None, s, D_KV), lambda i, j: (i, 0, 0)),
            pl.BlockSpec((None, s, D_KV), lambda i, j: (i, 0, 0)),
            _layer_vec(D_ATTN, l),
        ],
        out_specs=pl.BlockSpec((None, TQ, D_ATTN), lambda i, j: (i, j, 0)),
        out_shape=jax.ShapeDtypeStruct((b, s, D_ATTN), BF16),
        scratch_shapes=[pltpu.VMEM((QB_PER_ITER, N_Q_HEADS, QB, KV_WIN), F32),
                        pltpu.VMEM((QB_PER_ITER, N_Q_HEADS, QB, KV_WIN), BF16)],
        compiler_params=_cparams(("arbitrary", "arbitrary")),
        name="band_attn",
    )(sink, q, k, v, g)


def _fmix_prep_kernel(w_ref, cc_ref, sc_ref, o_ref):
    o_ref[...] = jnp.zeros(o_ref.shape, o_ref.dtype)
    scale = float(FFT_N1 * FFT_N2 * FOURIER_GROUP) ** -0.5
    for g in range(N_FOURIER_GROUPS):
        w = w_ref[g]
        mr = jnp.dot(cc_ref[...], w, preferred_element_type=F32, precision=lax.Precision.HIGHEST) * scale
        mi = jnp.dot(sc_ref[...], w, preferred_element_type=F32, precision=lax.Precision.HIGHEST) * (-scale)
        c = (g * FOURIER_GROUP) // FFT_CHUNK
        off = (g * FOURIER_GROUP) % FFT_CHUNK
        rows = slice(g * FOURIER_GROUP, (g + 1) * FOURIER_GROUP)
        o_ref[c, rows, off:off + FOURIER_GROUP] = mr.astype(BF16)
        o_ref[c, rows, FFT_CHUNK + off:FFT_CHUNK + off + FOURIER_GROUP] = mi.astype(BF16)


def _fmix_prep(w_fmix, cc, sc):
    depth = w_fmix.shape[0]
    n_chunk = D_FOURIER // FFT_CHUNK
    return pl.pallas_call(
        _fmix_prep_kernel,
        grid=(depth,),
        in_specs=[
            pl.BlockSpec((None, N_FOURIER_GROUPS, FOURIER_GROUP, FOURIER_GROUP), lambda l: (l, 0, 0, 0)),
            pl.BlockSpec((FOURIER_GROUP, FOURIER_GROUP), lambda l: (0, 0)),
            pl.BlockSpec((FOURIER_GROUP, FOURIER_GROUP), lambda l: (0, 0)),
        ],
        out_specs=pl.BlockSpec((None, n_chunk, D_FOURIER, 2 * FFT_CHUNK), lambda l: (l, 0, 0, 0)),
        out_shape=jax.ShapeDtypeStruct((depth, n_chunk, D_FOURIER, 2 * FFT_CHUNK), BF16),
        compiler_params=_cparams(("arbitrary",)),
        name="fmix_prep",
    )(w_fmix, cc, sc)


def _fourier_kernel(f_ref, m_ref, c0_ref, s0_ref, cb_ref, sb_ref, g_ref, o_ref, v_ref):
    k1 = pl.program_id(1)

    @pl.when(k1 == 0)
    def _():
        for c in range(D_FOURIER // FFT_CHUNK):
            mc = m_ref[c]
            z = [jnp.dot(f_ref[n1 * FFT_N2:(n1 + 1) * FFT_N2, :], mc, preferred_element_type=F32)
                 for n1 in range(FFT_N1)]
            zr = [t[:, :FFT_CHUNK] for t in z]
            zi = [t[:, FFT_CHUNK:] for t in z]
            ar, ai = zr[0] + zr[2], zi[0] + zi[2]
            br, bi = zr[1] + zr[3], zi[1] + zi[3]
            cr, ci = zr[0] - zr[2], zi[0] - zi[2]
            dr, di = zr[1] - zr[3], zi[1] - zi[3]
            u = [(ar + br, ai + bi), (cr + di, ci - dr), (ar - br, ai - bi), (cr - di, ci + dr)]
            for kk in range(FFT_N1):
                ur, ui = u[kk]
                cb, sb = cb_ref[kk], sb_ref[kk]
                v_ref[kk, :, c * FFT_CHUNK:(c + 1) * FFT_CHUNK] = (ur * cb + ui * sb).astype(BF16)
                v_ref[kk, :, D_FOURIER + c * FFT_CHUNK:D_FOURIER + (c + 1) * FFT_CHUNK] = (
                    ui * cb - ur * sb).astype(BF16)

    vr = v_ref[k1, :, :D_FOURIER]
    vi = v_ref[k1, :, D_FOURIER:]
    out = (jnp.dot(c0_ref[...], vr, preferred_element_type=F32)
           + jnp.dot(s0_ref[...], vi, preferred_element_type=F32))
    o_ref[...] = _rms(out, g_ref[...]).astype(BF16)


def _fourier(l, f, mst, c0, s0, cb, sb, g):
    b, s, _ = f.shape
    n_chunk = D_FOURIER // FFT_CHUNK
    out = pl.pallas_call(
        _fourier_kernel,
        grid=(b, FFT_N1),
        in_specs=[
            pl.BlockSpec((None, s, D_FOURIER), lambda i, j: (i, 0, 0)),
            pl.BlockSpec((None, n_chunk, D_FOURIER, 2 * FFT_CHUNK), lambda i, j: (l, 0, 0, 0)),
            pl.BlockSpec((FFT_N2, FFT_N2), lambda i, j: (0, 0)),
            pl.BlockSpec((FFT_N2, FFT_N2), lambda i, j: (0, 0)),
            pl.BlockSpec((FFT_N1, FFT_N2, V7X_LANES), lambda i, j: (0, 0, 0)),
            pl.BlockSpec((FFT_N1, FFT_N2, V7X_LANES), lambda i, j: (0, 0, 0)),
            _layer_vec(D_FOURIER, l),
        ],
        out_specs=pl.BlockSpec((None, FFT_N2, D_FOURIER), lambda i, j: (i, 0, j)),
        out_shape=jax.ShapeDtypeStruct((b, FFT_N2, FFT_N1 * D_FOURIER), BF16),
        scratch_shapes=[pltpu.VMEM((FFT_N1, FFT_N2, 2 * D_FOURIER), BF16)],
        compiler_params=_cparams(("arbitrary", "arbitrary")),
        name="fourier_mix",
    )(f, mst, c0, s0, cb, sb, g)
    return out.reshape(b * FFT_N2, FFT_N1 * D_FOURIER)


def _out_proj_body(a_ref, f_ref, x_ref, w_ref, g_ref, wbf_ref, fs_ref):
    @pl.when(pl.program_id(0) == 0)
    def _():
        wbf_ref[...] = w_ref[...].astype(BF16)

    return _out_proj_tile(a_ref, f_ref, x_ref, g_ref, wbf_ref, fs_ref)


def _out_proj_tile(a_ref, f_ref, x_ref, g_ref, wbf_ref, fs_ref):
    n_blk = D_FOURIER // V7X_LANES
    rows = x_ref.shape[0]
    for k1 in range(FFT_N1):
        for c in range(n_blk):
            col = k1 * D_FOURIER + c * V7X_LANES
            fs_ref[c, pl.ds(k1, rows // FFT_N1, stride=FFT_N1), :] = f_ref[:, col:col + V7X_LANES].astype(F32)
    fo = jnp.concatenate([fs_ref[c] for c in range(n_blk)], axis=1).astype(BF16)
    y = (jnp.dot(a_ref[...], wbf_ref[:D_ATTN, :], preferred_element_type=F32)
         + jnp.dot(fo, wbf_ref[D_ATTN:, :], preferred_element_type=F32))
    xn = x_ref[...] + y
    return xn, _rms(xn, g_ref[...])


def _out_proj_dense_kernel(a_ref, f_ref, x_ref, w_ref, g_ref, xo_ref, h_ref, wbf_ref, fs_ref):
    xn, h = _out_proj_body(a_ref, f_ref, x_ref, w_ref, g_ref, wbf_ref, fs_ref)
    xo_ref[...] = xn
    h_ref[...] = h.astype(BF16)


def _out_proj_moe_kernel(a_ref, f_ref, x_ref, w_ref, g_ref, wr_ref, lt_ref,
                         xo_ref, hrt_ref, gate_ref, rank_ref, cnt_ref,
                         wbf_ref, fs_ref, wr2_ref, hs_ref, carry_ref):
    @pl.when(pl.program_id(0) == 0)
    def _():
        hi, lo = _split_bf16(wr_ref[...])
        wr2_ref[:, :V7X_LANES] = hi
        wr2_ref[:, V7X_LANES:] = lo
        carry_ref[...] = jnp.zeros(carry_ref.shape, F32)

    xn, h = _out_proj_body(a_ref, f_ref, x_ref, w_ref, g_ref, wbf_ref, fs_ref)
    xo_ref[...] = xn
    for s in range(RT):
        hs_ref[pl.ds(s, TM, stride=RT), :] = h[:, s * V7X_LANES:(s + 1) * V7X_LANES]
    hrt_ref[...] = hs_ref[...].astype(BF16)
    h_hi, h_lo = _split_bf16(h)
    both = (jnp.dot(h_hi, wr2_ref[...], preferred_element_type=F32)
            + jnp.dot(h_lo, wr2_ref[...], preferred_element_type=F32))
    logits = both[:, :V7X_LANES] + both[:, V7X_LANES:]
    lane = lax.broadcasted_iota(jnp.int32, logits.shape, 1).astype(F32)
    logits = jnp.where(lane < N_EXPERTS, logits, -jnp.inf)
    m1 = jnp.max(logits, axis=-1, keepdims=True)
    i1 = jnp.min(jnp.where(logits == m1, lane, float(V7X_LANES)), axis=-1, keepdims=True)
    rest = jnp.where(lane == i1, -jnp.inf, logits)
    m2 = jnp.max(rest, axis=-1, keepdims=True)
    i2 = jnp.min(jnp.where(rest == m2, lane, float(V7X_LANES)), axis=-1, keepdims=True)
    e2 = jnp.exp(m2 - m1)
    den = 1.0 + e2
    sel1, sel2 = lane == i1, lane == i2
    sel = sel1 | sel2
    gate_ref[...] = jnp.where(sel1, 1.0 / den, jnp.where(sel2, e2 / den, 0.0))
    self32 = sel.astype(F32)
    rank = jnp.dot(lt_ref[...], self32.astype(BF16), preferred_element_type=F32) + carry_ref[...]
    rank_ref[...] = jnp.where(sel, rank, -1.0)
    carry_ref[...] = carry_ref[...] + jnp.sum(self32, axis=0, keepdims=True)
    cnt_ref[...] = carry_ref[...]


def _out_proj_specs(l):
    row = lambda i: (i, 0)
    return [
        pl.BlockSpec((TM, D_ATTN), row),
        pl.BlockSpec((TM // FFT_N1, FFT_N1 * D_FOURIER), row),
        pl.BlockSpec((TM, D_MODEL), row),
        pl.BlockSpec((None, D_MODEL, D_MODEL), lambda i: (l, 0, 0)),
        _layer_vec(D_MODEL, l),
    ]


def _out_proj(l, a, f, x, w, g):
    t = x.shape[0]
    row = lambda i: (i, 0)
    return pl.pallas_call(
        _out_proj_dense_kernel,
        grid=(t // TM,),
        in_specs=_out_proj_specs(l),
        out_specs=[pl.BlockSpec((TM, D_MODEL), row), pl.BlockSpec((TM, D_MODEL), row)],
        out_shape=[jax.ShapeDtypeStruct((t, D_MODEL), F32), jax.ShapeDtypeStruct((t, D_MODEL), BF16)],
        scratch_shapes=[pltpu.VMEM((D_MODEL, D_MODEL), BF16), pltpu.VMEM((D_FOURIER // V7X_LANES, TM, V7X_LANES), F32)],
        compiler_params=_cparams(("arbitrary",)),
        name="out_proj",
    )(a, f, x, w, g)


def _out_proj_route(l, li, a, f, x, w, g, w_router_pad, ltri):
    t = x.shape[0]
    row = lambda i: (i, 0)
    fixed = lambda i: (0, 0)
    in_specs = _out_proj_specs(l)
    return pl.pallas_call(
        _out_proj_moe_kernel,
        grid=(t // TM,),
        in_specs=in_specs + [pl.BlockSpec((None, D_MODEL, V7X_LANES), lambda i: (li, 0, 0)),
                             pl.BlockSpec((TM, TM), fixed)],
        out_specs=[
            pl.BlockSpec((TM, D_MODEL), row),
            pl.BlockSpec((TM * RT, V7X_LANES), row),
            pl.BlockSpec((TM, V7X_LANES), row),
            pl.BlockSpec((TM, V7X_LANES), row),
            pl.BlockSpec((1, V7X_LANES), fixed),
        ],
        out_shape=[
            jax.ShapeDtypeStruct((t, D_MODEL), F32),
            jax.ShapeDtypeStruct((t * RT, V7X_LANES), BF16),
            jax.ShapeDtypeStruct((t, V7X_LANES), F32),
            jax.ShapeDtypeStruct((t, V7X_LANES), F32),
            jax.ShapeDtypeStruct((1, V7X_LANES), F32),
        ],
        scratch_shapes=[
            pltpu.VMEM((D_MODEL, D_MODEL), BF16),
            pltpu.VMEM((D_FOURIER // V7X_LANES, TM, V7X_LANES), F32),
            pltpu.VMEM((D_MODEL, 2 * V7X_LANES), BF16),
            pltpu.VMEM((TM * RT, V7X_LANES), F32),
            pltpu.VMEM((1, V7X_LANES), F32),
        ],
        compiler_params=_cparams(("arbitrary",)),
        name="out_proj_route",
    )(a, f, x, w, g, w_router_pad, ltri)


def _ffn_kernel(x_ref, h_ref, wg_ref, wu_ref, wd_ref, o_ref, act_ref):
    @pl.when(pl.program_id(1) == 0)
    def _():
        o_ref[...] = x_ref[...]

    h = h_ref[...]
    for c in range(TF_FFN // V7X_MXU):
        cols = slice(c * V7X_MXU, (c + 1) * V7X_MXU)
        g = jnp.dot(h, wg_ref[:, cols].astype(BF16), preferred_element_type=F32)
        u = jnp.dot(h, wu_ref[:, cols].astype(BF16), preferred_element_type=F32)
        act_ref[:, cols] = (g * jax.nn.sigmoid(g) * u).astype(BF16)
    o_ref[...] += jnp.dot(act_ref[...], wd_ref[...].astype(BF16), preferred_element_type=F32)


def _ffn(li, x, h, wg, wu, wd):
    t = x.shape[0]
    return pl.pallas_call(
        _ffn_kernel,
        grid=(t // TM_FFN, D_FF // TF_FFN),
        in_specs=[
            pl.BlockSpec((TM_FFN, D_MODEL), lambda i, j: (i, 0)),
            pl.BlockSpec((TM_FFN, D_MODEL), lambda i, j: (i, 0)),
            pl.BlockSpec((None, D_MODEL, TF_FFN), lambda i, j: (li, 0, j)),
            pl.BlockSpec((None, D_MODEL, TF_FFN), lambda i, j: (li, 0, j)),
            pl.BlockSpec((None, TF_FFN, D_MODEL), lambda i, j: (li, j, 0)),
        ],
        out_specs=pl.BlockSpec((TM_FFN, D_MODEL), lambda i, j: (i, 0)),
        out_shape=jax.ShapeDtypeStruct((t, D_MODEL), F32),
        scratch_shapes=[pltpu.VMEM((TM_FFN, TF_FFN), BF16)],
        compiler_params=_cparams(("arbitrary", "arbitrary")),
        name="ffn_dense",
    )(x, h, wg, wu, wd)


def _plan_kernel(gate_ref, rank_ref, cnt_ref, pos_ref, wgt_ref, meta_ref):
    lane = lax.broadcasted_iota(jnp.int32, (1, V7X_LANES), 1)
    cnt = cnt_ref[...]
    padded = jnp.floor((cnt + (TM_MOE - 1)) * (1.0 / TM_MOE)) * TM_MOE
    base = jnp.zeros_like(padded)
    for j in range(1, N_EXPERTS):
        base = base + jnp.where(lane >= j, pltpu.roll(padded, j, 1), 0.0)
    rank = rank_ref[...]
    pos_t = jnp.where(rank >= 0.0, base + rank, -1.0).T
    gate_t = gate_ref[...].T
    seen = jnp.zeros((1, TM_PLAN), F32)
    pos_lo = jnp.zeros((1, TM_PLAN), F32)
    pos_hi = jnp.zeros((1, TM_PLAN), F32)
    w_lo = jnp.zeros((1, TM_PLAN), F32)
    w_hi = jnp.zeros((1, TM_PLAN), F32)
    for e in range(N_EXPERTS):
        p = pos_t[e:e + 1, :]
        gt = gate_t[e:e + 1, :]
        chosen = p >= 0.0
        first = chosen & (seen == 0.0)
        second = chosen & (seen == 1.0)
        pos_lo = jnp.where(first, p, pos_lo)
        pos_hi = jnp.where(second, p, pos_hi)
        w_lo = jnp.where(first, gt, w_lo)
        w_hi = jnp.where(second, gt, w_hi)
        seen = seen + chosen.astype(F32)
    pos_ref[0:1, :] = pos_lo.astype(jnp.int32)
    pos_ref[1:2, :] = pos_hi.astype(jnp.int32)
    wgt_ref[0:1, :] = w_lo
    wgt_ref[1:2, :] = w_hi
    ends = base + padded
    tile_start = (lane * TM_MOE).astype(F32)
    tile_e = jnp.zeros((1, V7X_LANES), F32)
    for e in range(N_EXPERTS):
        tile_e = tile_e + (ends[:, e:e + 1] <= tile_start).astype(F32)
    tile_e = jnp.minimum(tile_e, float(N_EXPERTS - 1))
    n_used = ends[:, N_EXPERTS - 1:N_EXPERTS] * (1.0 / TM_MOE) + jnp.zeros((1, V7X_LANES), F32)
    meta_ref[0:1, :] = tile_e.astype(jnp.int32)
    meta_ref[1:2, :] = n_used.astype(jnp.int32)
    meta_ref[2:3, :] = cnt.astype(jnp.int32)
    meta_ref[3:4, :] = base.astype(jnp.int32)
    meta_ref[4:8, :] = jnp.zeros((4, V7X_LANES), jnp.int32)


def _plan(gates, rank, counts):
    t = gates.shape[0]
    return pl.pallas_call(
        _plan_kernel,
        grid=(t // TM_PLAN,),
        in_specs=[
            pl.BlockSpec((TM_PLAN, V7X_LANES), lambda i: (i, 0)),
            pl.BlockSpec((TM_PLAN, V7X_LANES), lambda i: (i, 0)),
            pl.BlockSpec((1, V7X_LANES), lambda i: (0, 0)),
        ],
        out_specs=[
            pl.BlockSpec((2, TM_PLAN), lambda i: (0, i)),
            pl.BlockSpec((2, TM_PLAN), lambda i: (0, i)),
            pl.BlockSpec((8, V7X_LANES), lambda i: (0, 0)),
        ],
        out_shape=[
            jax.ShapeDtypeStruct((2, t), jnp.int32),
            jax.ShapeDtypeStruct((2, t), F32),
            jax.ShapeDtypeStruct((8, V7X_LANES), jnp.int32),
        ],
        compiler_params=_cparams(("arbitrary",)),
        name="moe_plan",
    )(gates, rank, counts)


def _rt_rows(ref, row, n=1, per=RT):
    start = row * per if isinstance(row, int) else pl.multiple_of(row * per, per)
    return ref.at[pl.ds(start, n * per)]


def _dispatch_kernel(plo_ref, phi_ref, cnt_ref, base_ref, nt_ref, h_ref, o_ref, z_ref, sem):
    i = pl.program_id(0)
    t0 = i * TM

    def issue(r, carry):
        src = _rt_rows(h_ref, r)
        pltpu.make_async_copy(src, _rt_rows(o_ref, plo_ref[t0 + r]), sem).start(priority=0)
        pltpu.make_async_copy(src, _rt_rows(o_ref, phi_ref[t0 + r]), sem).start(priority=1)
        return carry

    lax.fori_loop(0, TM, issue, 0, unroll=DMA_UNROLL)
    for _ in range(2):
        pltpu.make_async_copy(h_ref, _rt_rows(o_ref, 0, TM), sem).wait()

    @pl.when(i == pl.num_programs(0) - 1)
    def _():
        z_ref[...] = jnp.zeros(z_ref.shape, z_ref.dtype)
        for e in range(N_EXPERTS):
            n = cnt_ref[e]
            n_pad = ((n + (TM_MOE - 1)) // TM_MOE) * TM_MOE - n
            first = base_ref[e] + n

            def pad_issue(r, carry, first=first):
                pltpu.make_async_copy(_rt_rows(z_ref, 0), _rt_rows(o_ref, first + r), sem).start()
                return carry

            def pad_drain(r, carry):
                pltpu.make_async_copy(_rt_rows(z_ref, 0), _rt_rows(o_ref, 0), sem).wait()
                return carry

            lax.fori_loop(0, n_pad, pad_issue, 0)
            lax.fori_loop(0, n_pad, pad_drain, 0)

        def tail(tile, carry):
            cp = pltpu.make_async_copy(z_ref, _rt_rows(o_ref, tile * TM_MOE, TM_MOE), sem)
            cp.start()
            cp.wait()
            return carry

        lax.fori_loop(nt_ref[0], o_ref.shape[0] // (TM_MOE * RT), tail, 0)


def _dispatch(pos_lo, pos_hi, cnt, base, n_used, h_rt, n_rows):
    t = h_rt.shape[0] // RT
    return pl.pallas_call(
        _dispatch_kernel,
        grid_spec=pltpu.PrefetchScalarGridSpec(
            num_scalar_prefetch=5,
            grid=(t // TM,),
            in_specs=[pl.BlockSpec((TM * RT, V7X_LANES), lambda i, *_: (i, 0))],
            out_specs=pl.BlockSpec(memory_space=pl.ANY),
            scratch_shapes=[pltpu.VMEM((TM_MOE * RT, V7X_LANES), BF16), pltpu.SemaphoreType.DMA(())],
        ),
        out_shape=jax.ShapeDtypeStruct((n_rows * RT, V7X_LANES), BF16),
        compiler_params=_cparams(("arbitrary",)),
        name="moe_dispatch",
    )(pos_lo, pos_hi, cnt, base, n_used, h_rt)


def _tile_of(i, nt_ref):
    return jnp.minimum(i, nt_ref[0] - 1)


def _moe_up_kernel(te_ref, nt_ref, hs_ref, wg_ref, wu_ref, o_ref, xs_ref):
    i = pl.program_id(1)

    @pl.when(i < nt_ref[0])
    def _():
        xs_ref[...] = hs_ref[...].astype(F32)
        x = jnp.concatenate([xs_ref[pl.ds(s, TM_MOE, stride=RT), :].astype(BF16) for s in range(RT)],
                            axis=1)
        for c in range(TF_UP // V7X_MXU):
            cols = slice(c * V7X_MXU, (c + 1) * V7X_MXU)
            g = jnp.dot(x, wg_ref[:, cols].astype(BF16), preferred_element_type=F32)
            u = jnp.dot(x, wu_ref[:, cols].astype(BF16), preferred_element_type=F32)
            o_ref[:, cols] = (g * jax.nn.sigmoid(g) * u).astype(BF16)

    @pl.when(i >= nt_ref[0])
    def _():
        o_ref[...] = jnp.zeros(o_ref.shape, o_ref.dtype)


def _moe_up(li, tile_e, n_used, hs, wg, wu):
    n_rows = hs.shape[0] // RT
    nt = n_rows // TM_MOE
    wspec = pl.BlockSpec((None, None, D_MODEL, TF_UP), lambda j, i, te, n: (li, te[_tile_of(i, n)], 0, j))
    return pl.pallas_call(
        _moe_up_kernel,
        grid_spec=pltpu.PrefetchScalarGridSpec(
            num_scalar_prefetch=2,
            grid=(D_FF // TF_UP, nt),
            in_specs=[pl.BlockSpec((TM_MOE * RT, V7X_LANES), lambda j, i, te, n: (_tile_of(i, n), 0)), wspec, wspec],
            out_specs=pl.BlockSpec((TM_MOE, TF_UP), lambda j, i, te, n: (i, j)),
            scratch_shapes=[pltpu.VMEM((TM_MOE * RT, V7X_LANES), F32)],
        ),
        out_shape=jax.ShapeDtypeStruct((n_rows, D_FF), BF16),
        compiler_params=_cparams(("arbitrary", "arbitrary")),
        name="moe_up",
    )(tile_e, n_used, hs, wg, wu)


def _moe_down_kernel(te_ref, nt_ref, a_ref, wd_ref, o_ref):
    i = pl.program_id(0)

    @pl.when(i < nt_ref[0])
    def _():
        a = a_ref[...]
        for c in range(D_MODEL // V7X_MXU):
            y = jnp.dot(a, wd_ref[:, c * V7X_MXU:(c + 1) * V7X_MXU].astype(BF16), preferred_element_type=F32)
            for k in range(V7X_MXU // V7X_LANES):
                s = c * (V7X_MXU // V7X_LANES) + k
                o_ref[pl.ds(s, TM_MOE, stride=RT), :] = y[:, k * V7X_LANES:(k + 1) * V7X_LANES]

    @pl.when(i >= nt_ref[0])
    def _():
        o_ref[...] = jnp.zeros(o_ref.shape, o_ref.dtype)


def _moe_down(li, tile_e, n_used, act, wd):
    n_rows = act.shape[0]
    nt = n_rows // TM_MOE
    return pl.pallas_call(
        _moe_down_kernel,
        grid_spec=pltpu.PrefetchScalarGridSpec(
            num_scalar_prefetch=2,
            grid=(nt,),
            in_specs=[
                pl.BlockSpec((TM_MOE, D_FF), lambda i, te, n: (_tile_of(i, n), 0)),
                pl.BlockSpec((None, None, D_FF, D_MODEL), lambda i, te, n: (li, te[_tile_of(i, n)], 0, 0)),
            ],
            out_specs=pl.BlockSpec((TM_MOE * RT, V7X_LANES), lambda i, te, n: (i, 0)),
        ),
        out_shape=jax.ShapeDtypeStruct((n_rows * RT, V7X_LANES), F32),
        compiler_params=_cparams(("arbitrary",)),
        name="moe_down",
    )(tile_e, n_used, act, wd)


def _combine_kernel(plo_ref, phi_ref, x_ref, wlo_ref, whi_ref, y_ref, o_ref, lo_ref, hi_ref, sem):
    t0 = pl.program_id(0) * TM

    def issue(r, carry):
        pltpu.make_async_copy(_rt_rows(y_ref, plo_ref[t0 + r]), _rt_rows(lo_ref, r), sem).start(priority=0)
        pltpu.make_async_copy(_rt_rows(y_ref, phi_ref[t0 + r]), _rt_rows(hi_ref, r), sem).start(priority=1)
        return carry

    lax.fori_loop(0, TM, issue, 0, unroll=DMA_UNROLL)
    pltpu.make_async_copy(_rt_rows(y_ref, 0, TM), lo_ref, sem).wait()
    pltpu.make_async_copy(_rt_rows(y_ref, 0, TM), hi_ref, sem).wait()
    w_lo, w_hi = wlo_ref[...], whi_ref[...]
    for s in range(RT):
        cols = slice(s * V7X_LANES, (s + 1) * V7X_LANES)
        o_ref[:, cols] = x_ref[:, cols] + (w_lo * lo_ref[pl.ds(s, TM, stride=RT), :]
                                           + w_hi * hi_ref[pl.ds(s, TM, stride=RT), :])


def _combine(pos_lo, pos_hi, x, w_lo, w_hi, ys):
    t = x.shape[0]
    return pl.pallas_call(
        _combine_kernel,
        grid_spec=pltpu.PrefetchScalarGridSpec(
            num_scalar_prefetch=2,
            grid=(t // TM,),
            in_specs=[
                pl.BlockSpec((TM, D_MODEL), lambda i, a, b: (i, 0)),
                pl.BlockSpec((TM, 1), lambda i, a, b: (i, 0)),
                pl.BlockSpec((TM, 1), lambda i, a, b: (i, 0)),
                pl.BlockSpec(memory_space=pl.ANY),
            ],
            out_specs=pl.BlockSpec((TM, D_MODEL), lambda i, a, b: (i, 0)),
            scratch_shapes=[pltpu.VMEM((TM * RT, V7X_LANES), F32), pltpu.VMEM((TM * RT, V7X_LANES), F32),
                            pltpu.SemaphoreType.DMA(())],
        ),
        out_shape=jax.ShapeDtypeStruct((t, D_MODEL), F32),
        compiler_params=_cparams(("arbitrary",)),
        name="moe_combine",
    )(pos_lo, pos_hi, x, w_lo, w_hi, ys)


def _moe(li, x, h_rt, gates, rank, counts, e_gate, e_up, e_down):
    t = x.shape[0]
    n_rows = ((2 * t) // TM_MOE + N_EXPERTS) * TM_MOE
    pos, wgt, meta = _plan(gates, rank, counts)
    pos_lo, pos_hi = pos[0], pos[1]
    tile_e, n_used = meta[0], meta[1, :1]
    cnt, base = meta[2, :N_EXPERTS], meta[3, :N_EXPERTS]
    hs = _dispatch(pos_lo, pos_hi, cnt, base, n_used, h_rt, n_rows)
    act = _moe_up(li, tile_e, n_used, hs, e_gate, e_up)
    ys = _moe_down(li, tile_e, n_used, act, e_down)
    return pos_lo, pos_hi, x, wgt[0].reshape(t, 1), wgt[1].reshape(t, 1), ys


def _rope_tables(positions):
    inv_freq = ROPE_THETA ** (-jnp.arange(0, ROT_DIM, 2, dtype=F32) / ROT_DIM)
    ang = positions.astype(F32).reshape(-1, 1) * inv_freq
    cos, sin = jnp.cos(ang), jnp.sin(ang)
    t = cos.shape[0]
    half = ROT_DIM // 2
    ones = jnp.ones((t, HEAD_DIM - ROT_DIM), F32)
    zeros = jnp.zeros((t, HEAD_DIM - ROT_DIM), F32)
    zh = jnp.zeros((t, half), F32)
    reps = V7X_LANES // HEAD_DIM
    cos_t = jnp.tile(jnp.concatenate([cos, cos, ones], axis=1), (1, reps))
    sin_a = jnp.tile(jnp.concatenate([zh, sin, zeros], axis=1), (1, reps))
    sin_b = jnp.tile(jnp.concatenate([-sin, zh, zeros], axis=1), (1, reps))
    return cos_t, sin_a, sin_b


def _dft_tables(seq):
    n2 = jnp.arange(FFT_N2, dtype=jnp.int32)
    m = (n2[:, None] * n2[None, :]) % FFT_N2
    ang = m.astype(F32) * (2.0 * np.pi / FFT_N2)
    c0, s0 = jnp.cos(ang).astype(BF16), jnp.sin(ang).astype(BF16)
    k1 = jnp.arange(FFT_N1, dtype=jnp.int32)
    beta = (k1[:, None] * n2[None, :]).astype(F32) * (2.0 * np.pi / seq)
    cb = jnp.broadcast_to(jnp.cos(beta)[:, :, None], (FFT_N1, FFT_N2, V7X_LANES))
    sb = jnp.broadcast_to(jnp.sin(beta)[:, :, None], (FFT_N1, FFT_N2, V7X_LANES))
    c = jnp.arange(FOURIER_GROUP, dtype=jnp.int32)
    angc = ((c[:, None] * c[None, :]) % FOURIER_GROUP).astype(F32) * (2.0 * np.pi / FOURIER_GROUP)
    return c0, s0, cb, sb, jnp.cos(angc), jnp.sin(angc)


def kernel(x, positions, norm_mix, w_in, q_norm, k_norm, sink, w_fmix, g_attn_out, g_fourier_out, w_out, norm_ffn,
           ffn_gate, ffn_up, ffn_down, w_router, e_gate, e_up, e_down):
    b, s, d = x.shape
    depth = w_in.shape[0]
    assert (d, s) == (D_MODEL, FFT_N1 * FFT_N2) and (b * s) % TM_PLAN == 0
    t = b * s
    cos_t, sin_a, sin_b = _rope_tables(positions)
    c0, s0, cb, sb, cc, sc = _dft_tables(s)
    mst = _fmix_prep(w_fmix, cc, sc)
    head = jnp.arange(D_QK) // HEAD_DIM
    member = head[:, None] == jnp.arange(V7X_LANES)[None, :]
    head_sum = member.astype(BF16) * (1.0 / HEAD_DIM)
    head_expand = member.T.astype(BF16)
    row_i = jnp.arange(TM)
    ltri = (row_i[None, :] < row_i[:, None]).astype(BF16)
    gain = jnp.concatenate([jnp.tile(q_norm, (1, N_Q_HEADS)) * (HEAD_DIM ** -0.5),
                            jnp.tile(k_norm, (1, N_KV_HEADS))], axis=1).reshape(depth, 1, D_QK)
    vec = lambda p: p.reshape(depth, 1, p.shape[-1])
    norm_mix, norm_ffn, g_attn_out, g_fourier_out = vec(norm_mix), vec(norm_ffn), vec(g_attn_out), vec(g_fourier_out)
    wr = jnp.pad(w_router, ((0, 0), (0, 0), (0, V7X_LANES - N_EXPERTS)))
    xt = x.reshape(t, d)
    pending = None
    for l in range(depth):
        if pending is None:
            q, k, v, f = _in_proj(l, xt, norm_mix, w_in, head_sum, head_expand, gain, cos_t, sin_a, sin_b)
        else:
            xt, q, k, v, f = _combine_in_proj(l, pending, norm_mix, w_in, head_sum, head_expand, gain,
                                              cos_t, sin_a, sin_b)
            pending = None
        a = _attention(l, q.reshape(b, s, D_ATTN), k.reshape(b, s, D_KV), v.reshape(b, s, D_KV), sink, g_attn_out)
        fo = _fourier(l, f.reshape(b, s, D_FOURIER), mst, c0, s0, cb, sb, g_fourier_out)
        a2, f2 = a.reshape(t, D_ATTN), fo
        li = l // 2
        if l % 2 == 0:
            xt, h = _out_proj(l, a2, f2, xt, w_out, norm_ffn)
            xt = _ffn(li, xt, h, ffn_gate, ffn_up, ffn_down)
        else:
            xt, h_rt, gates, rank, counts = _out_proj_route(l, li, a2, f2, xt, w_out, norm_ffn, wr, ltri)
            pending = _moe(li, xt, h_rt, gates, rank, counts, e_gate, e_up, e_down)
    if pending is not None:
        xt = _combine(*pending)
    return xt.reshape(b, s, d)
```

```python
import functools

import jax
import jax.numpy as jnp
import numpy as np
from jax import lax
from jax.experimental import pallas as pl
from jax.experimental.pallas import tpu as pltpu

F32 = jnp.float32
BF16 = jnp.bfloat16

D_MODEL = 1024
HEAD_DIM = 64
D_ATTN = 512
N_Q_HEADS = 8
N_KV_HEADS = 2
Q_PER_KV = 4
D_KV = 128
D_FOURIER = 512
N_FOURIER_GROUPS = 8
FOURIER_GROUP = 64
D_QK = D_ATTN + D_KV
D_IN = D_ATTN + 2 * D_KV + D_FOURIER
WINDOW = 128
ROPE_THETA = 500000.0
ROT_DIM = 16
D_FF = 3584
N_EXPERTS = 8
EPS = 1e-6

V7X_LANES = 128
V7X_SUBLANES = 8
V7X_MXU = 256
V7X_VMEM_LIMIT = 56 * 1024 * 1024
RT = D_MODEL // V7X_LANES
assert RT == V7X_SUBLANES

TM = 1024
TQ = 1024
QB = 128
QB_PER_ITER = 1
KV_WIN = QB + 2 * WINDOW
FFT_N1 = 4
FFT_N2 = 1024
FFT_CHUNK = 128
TM_FFN = 1024
TF_FFN = 512
TM_MOE = 512
TF_UP = D_FF // 2
TM_PLAN = 1024
TM_CMB = 512
N_ISSUE = 2 + D_IN // V7X_MXU + 1
DMA_UNROLL = 8


def _cparams(sem, vmem=V7X_VMEM_LIMIT):
    return pltpu.CompilerParams(dimension_semantics=sem, vmem_limit_bytes=vmem)


def _rms(xf, g):
    ms = jnp.mean(xf * xf, axis=-1, keepdims=True)
    return xf * lax.rsqrt(ms + EPS) * g


def _split_bf16(xf):
    hi = xf.astype(BF16)
    lo = (xf - hi.astype(F32)).astype(BF16)
    return hi, lo


def _layer_vec(n, l):
    return pl.BlockSpec((None, 1, n), lambda *_: (l, 0, 0))


def _in_proj_kernel(x_ref, g_ref, w_ref, hs_ref, he_ref, gain_ref, cos_ref, sa_ref, sb_ref,
                    q_ref, k_ref, v_ref, f_ref, wbf_ref):
    @pl.when(pl.program_id(0) == 0)
    def _():
        wbf_ref[...] = w_ref[...].astype(BF16)

    _in_proj_tile(x_ref[...], g_ref, hs_ref, he_ref, gain_ref, cos_ref, sa_ref, sb_ref,
                  q_ref, k_ref, v_ref, f_ref, wbf_ref)


def _in_proj_tile(x, g_ref, hs_ref, he_ref, gain_ref, cos_ref, sa_ref, sb_ref, q_ref, k_ref, v_ref, f_ref, wbf_ref,
                  between=None):
    h = _rms(x, g_ref[...]).astype(BF16)
    if between is None:
        z = jnp.dot(h, wbf_ref[...], preferred_element_type=F32)
    else:
        between(0)
        parts = []
        for c in range(D_IN // V7X_MXU):
            parts.append(jnp.dot(h, wbf_ref[:, c * V7X_MXU:(c + 1) * V7X_MXU], preferred_element_type=F32))
            between(1 + c)
        z = jnp.concatenate(parts, axis=1)
    qk = z[:, :D_QK]
    hi, lo = _split_bf16(qk * qk)
    ms = (jnp.dot(hi, hs_ref[...], preferred_element_type=F32)
          + jnp.dot(lo, hs_ref[...], preferred_element_type=F32))
    r_hi, r_lo = _split_bf16(lax.rsqrt(ms + EPS))
    inv_rms = (jnp.dot(r_hi, he_ref[...], preferred_element_type=F32)
               + jnp.dot(r_lo, he_ref[...], preferred_element_type=F32))
    qkn = qk * inv_rms * gain_ref[...]
    if between is not None:
        between(1 + D_IN // V7X_MXU)
    cos_t, sin_a, sin_b = cos_ref[...], sa_ref[...], sb_ref[...]
    for c in range(D_QK // V7X_LANES):
        blk = qkn[:, c * V7X_LANES:(c + 1) * V7X_LANES]
        rot = (blk * cos_t + pltpu.roll(blk, ROT_DIM // 2, 1) * sin_a
               + pltpu.roll(blk, V7X_LANES - ROT_DIM // 2, 1) * sin_b).astype(BF16)
        if c < D_ATTN // V7X_LANES:
            q_ref[:, c * V7X_LANES:(c + 1) * V7X_LANES] = rot
        else:
            k_ref[...] = rot
    v_ref[...] = z[:, D_QK:D_QK + D_KV].astype(BF16)
    f_ref[...] = z[:, D_QK + D_KV:].astype(BF16)


def _in_proj_specs(l, tm, row, fixed):
    in_specs = [
        pl.BlockSpec((tm, D_MODEL), row),
        _layer_vec(D_MODEL, l),
        pl.BlockSpec((None, D_MODEL, D_IN), lambda i, *_: (l, 0, 0)),
        pl.BlockSpec((D_QK, V7X_LANES), fixed),
        pl.BlockSpec((V7X_LANES, D_QK), fixed),
        _layer_vec(D_QK, l),
        pl.BlockSpec((tm, V7X_LANES), row),
        pl.BlockSpec((tm, V7X_LANES), row),
        pl.BlockSpec((tm, V7X_LANES), row),
    ]
    out_specs = [
        pl.BlockSpec((tm, D_ATTN), row),
        pl.BlockSpec((tm, D_KV), row),
        pl.BlockSpec((tm, D_KV), row),
        pl.BlockSpec((tm, D_FOURIER), row),
    ]
    return in_specs, out_specs


def _in_proj_out_shapes(t):
    return [
        jax.ShapeDtypeStruct((t, D_ATTN), BF16),
        jax.ShapeDtypeStruct((t, D_KV), BF16),
        jax.ShapeDtypeStruct((t, D_KV), BF16),
        jax.ShapeDtypeStruct((t, D_FOURIER), BF16),
    ]


def _in_proj(l, x, g, w, head_sum, head_expand, gain, cos_t, sin_a, sin_b):
    t = x.shape[0]
    in_specs, out_specs = _in_proj_specs(l, TM, lambda i: (i, 0), lambda i: (0, 0))
    return pl.pallas_call(
        _in_proj_kernel,
        grid=(t // TM,),
        in_specs=in_specs,
        out_specs=out_specs,
        out_shape=_in_proj_out_shapes(t),
        scratch_shapes=[pltpu.VMEM((D_MODEL, D_IN), BF16)],
        compiler_params=_cparams(("arbitrary",)),
        name="in_proj",
    )(x, g, w, head_sum, head_expand, gain, cos_t, sin_a, sin_b)


def _combine_in_proj_kernel(plo_ref, phi_ref, x_ref, g_ref, w_ref, hs_ref, he_ref, gain_ref, cos_ref, sa_ref, sb_ref,
                            wlo_ref, whi_ref, y_ref, xo_ref, q_ref, k_ref, v_ref, f_ref,
                            wbf_ref, lo_ref, hi_ref, sems):
    i = pl.program_id(0)
    n = pl.num_programs(0)

    def gather_chunk(tile, slot, chunk):
        t0 = tile * TM_CMB
        per = TM_CMB // N_ISSUE
        for r in range(chunk * per, (chunk + 1) * per):
            pltpu.make_async_copy(_rt_rows(y_ref, plo_ref[t0 + r]), _rt_rows(lo_ref.at[slot], r),
                                  sems.at[slot]).start(priority=0)
            pltpu.make_async_copy(_rt_rows(y_ref, phi_ref[t0 + r]), _rt_rows(hi_ref.at[slot], r),
                                  sems.at[slot]).start(priority=1)

    def gather_wait(slot):
        pltpu.make_async_copy(_rt_rows(y_ref, 0, TM_CMB), lo_ref.at[slot], sems.at[slot]).wait()
        pltpu.make_async_copy(_rt_rows(y_ref, 0, TM_CMB), hi_ref.at[slot], sems.at[slot]).wait()

    @pl.when(i == 0)
    def _():
        wbf_ref[...] = w_ref[...].astype(BF16)

        def first(r, carry):
            pltpu.make_async_copy(_rt_rows(y_ref, plo_ref[r]), _rt_rows(lo_ref.at[0], r), sems.at[0]).start(priority=0)
            pltpu.make_async_copy(_rt_rows(y_ref, phi_ref[r]), _rt_rows(hi_ref.at[0], r), sems.at[0]).start(priority=1)
            return carry

        lax.fori_loop(0, TM_CMB, first, 0, unroll=DMA_UNROLL)

    slot = i % 2
    nxt = jnp.minimum(i + 1, n - 1)
    gather_wait(slot)
    w_lo, w_hi = wlo_ref[...], whi_ref[...]
    for s in range(RT):
        cols = slice(s * V7X_LANES, (s + 1) * V7X_LANES)
        xo_ref[:, cols] = x_ref[:, cols] + (w_lo * lo_ref[slot, pl.ds(s, TM_CMB, stride=RT), :]
                                             + w_hi * hi_ref[slot, pl.ds(s, TM_CMB, stride=RT), :])
    gather_chunk(nxt, 1 - slot, 0)
    _in_proj_tile(xo_ref[...], g_ref, hs_ref, he_ref, gain_ref, cos_ref, sa_ref, sb_ref,
                  q_ref, k_ref, v_ref, f_ref, wbf_ref,
                  between=lambda k: gather_chunk(nxt, 1 - slot, 1 + k))

    @pl.when(i == n - 1)
    def _():
        gather_wait(1 - slot)


def _combine_in_proj(l, pending, g, w, head_sum, head_expand, gain, cos_t, sin_a, sin_b):
    pos_lo, pos_hi, x, w_lo, w_hi, ys = pending
    t = x.shape[0]
    row = lambda i, a, b: (i, 0)
    in_specs, out_specs = _in_proj_specs(l, TM_CMB, row, lambda i, a, b: (0, 0))
    return pl.pallas_call(
        _combine_in_proj_kernel,
        grid_spec=pltpu.PrefetchScalarGridSpec(
            num_scalar_prefetch=2,
            grid=(t // TM_CMB,),
            in_specs=in_specs + [pl.BlockSpec((TM_CMB, 1), row), pl.BlockSpec((TM_CMB, 1), row),
                                 pl.BlockSpec(memory_space=pl.ANY)],
            out_specs=[pl.BlockSpec((TM_CMB, D_MODEL), row)] + out_specs,
            scratch_shapes=[
                pltpu.VMEM((D_MODEL, D_IN), BF16),
                pltpu.VMEM((2, TM_CMB * RT, V7X_LANES), F32),
                pltpu.VMEM((2, TM_CMB * RT, V7X_LANES), F32),
                pltpu.SemaphoreType.DMA((2,)),
            ],
        ),
        out_shape=[jax.ShapeDtypeStruct((t, D_MODEL), F32)] + _in_proj_out_shapes(t),
        compiler_params=_cparams(("arbitrary",)),
        name="combine_in_proj",
    )(pos_lo, pos_hi, x, g, w, head_sum, head_expand, gain, cos_t, sin_a, sin_b, w_lo, w_hi, ys)


def _attn_kernel(sink_ref, q_ref, k_ref, v_ref, g_ref, o_ref, s_ref, p_ref, *, seq, layer):
    q0 = pl.program_id(1) * TQ
    gain = g_ref[...]
    heads = range(N_Q_HEADS)

    def blocks(j, carry):
        units = [(b, h) for b in range(QB_PER_ITER) for h in heads]
        r0, kw, vw, mask, q = [], [], [], [], []
        for b in range(QB_PER_ITER):
            r = pl.multiple_of((j * QB_PER_ITER + b) * QB, QB)
            qs = q0 + r
            ws = pl.multiple_of(jnp.clip(qs - WINDOW, 0, seq - KV_WIN), QB)
            r0.append(r)
            kw.append(k_ref[pl.ds(ws, KV_WIN), :])
            vw.append(v_ref[pl.ds(ws, KV_WIN), :])
            q_abs = qs + lax.broadcasted_iota(jnp.int32, (QB, KV_WIN), 0)
            k_abs = ws + lax.broadcasted_iota(jnp.int32, (QB, KV_WIN), 1)
            mask.append(jnp.abs(q_abs - k_abs) <= WINDOW)
            q.append(q_ref[pl.ds(r, QB), :])
        for b, h in units:
            kvh = h // Q_PER_KV
            qh = q[b][:, h * HEAD_DIM:(h + 1) * HEAD_DIM]
            kh = kw[b][:, kvh * HEAD_DIM:(kvh + 1) * HEAD_DIM]
            s = lax.dot_general(qh, kh, (((1,), (1,)), ((), ())), preferred_element_type=F32)
            s_ref[b, h] = jnp.where(mask[b], s, -jnp.inf)
        m = {(b, h): jnp.maximum(jnp.max(s_ref[b, h], axis=-1, keepdims=True), sink_ref[layer, h])
             for b, h in units}
        inv = {}
        for b, h in units:
            p = jnp.exp(s_ref[b, h] - m[b, h])
            inv[b, h] = 1.0 / (jnp.sum(p, axis=-1, keepdims=True) + jnp.exp(sink_ref[layer, h] - m[b, h]))
            p_ref[b, h] = p.astype(BF16)
        for b in range(QB_PER_ITER):
            outs = []
            for h in heads:
                kvh = h // Q_PER_KV
                vh = vw[b][:, kvh * HEAD_DIM:(kvh + 1) * HEAD_DIM]
                outs.append(jnp.dot(p_ref[b, h], vh, preferred_element_type=F32) * inv[b, h])
            a = jnp.concatenate(outs, axis=-1)
            o_ref[pl.ds(r0[b], QB), :] = _rms(a, gain).astype(BF16)
        return carry

    lax.fori_loop(0, TQ // (QB * QB_PER_ITER), blocks, 0)


def _attention(l, q, k, v, sink, g):
    b, s, _ = q.shape
    return pl.pallas_call(
        functools.partial(_attn_kernel, seq=s, layer=l),
        grid=(b, s // TQ),
        in_specs=[
            pl.BlockSpec(memory_space=pltpu.SMEM),
            pl.BlockSpec((None, TQ, D_ATTN), lambda i, j: (i, j, 0)),
            pl.BlockSpec((None, s, D_KV), lambda i, j: (i, 0, 0)),
            pl.BlockSpec((None, s, D_KV), lambda i, j: (i, 0, 0)),
            _layer_vec(D_ATTN, l),
        ],
        out_specs=pl.BlockSpec((None, TQ, D_ATTN), lambda i, j: (i, j, 0)),
        out_shape=jax.ShapeDtypeStruct((b, s, D_ATTN), BF16),
        scratch_shapes=[pltpu.VMEM((QB_PER_ITER, N_Q_HEADS, QB, KV_WIN), F32),
                        pltpu.VMEM((QB_PER_ITER, N_Q_HEADS, QB, KV_WIN), BF16)],
        compiler_params=_cparams(("arbitrary", "arbitrary")),
        name="band_attn",
    )(sink, q, k, v, g)


def _fmix_prep_kernel(w_ref, cc_ref, sc_ref, o_ref):
    o_ref[...] = jnp.zeros(o_ref.shape, o_ref.dtype)
    scale = float(FFT_N1 * FFT_N2 * FOURIER_GROUP) ** -0.5
    for g in range(N_FOURIER_GROUPS):
        w = w_ref[g]
        mr = jnp.dot(cc_ref[...], w, preferred_element_type=F32, precision=lax.Precision.HIGHEST) * scale
        mi = jnp.dot(sc_ref[...], w, preferred_element_type=F32, precision=lax.Precision.HIGHEST) * (-scale)
        c = (g * FOURIER_GROUP) // FFT_CHUNK
        off = (g * FOURIER_GROUP) % FFT_CHUNK
        rows = slice(g * FOURIER_GROUP, (g + 1) * FOURIER_GROUP)
        o_ref[c, rows, off:off + FOURIER_GROUP] = mr.astype(BF16)
        o_ref[c, rows, FFT_CHUNK + off:FFT_CHUNK + off + FOURIER_GROUP] = mi.astype(BF16)


def _fmix_prep(w_fmix, cc, sc):
    depth = w_fmix.shape[0]
    n_chunk = D_FOURIER // FFT_CHUNK
    return pl.pallas_call(
        _fmix_prep_kernel,
        grid=(depth,),
        in_specs=[
            pl.BlockSpec((None, N_FOURIER_GROUPS, FOURIER_GROUP, FOURIER_GROUP), lambda l: (l, 0, 0, 0)),
            pl.BlockSpec((FOURIER_GROUP, FOURIER_GROUP), lambda l: (0, 0)),
            pl.BlockSpec((FOURIER_GROUP, FOURIER_GROUP), lambda l: (0, 0)),
        ],
        out_specs=pl.BlockSpec((None, n_chunk, D_FOURIER, 2 * FFT_CHUNK), lambda l: (l, 0, 0, 0)),
        out_shape=jax.ShapeDtypeStruct((depth, n_chunk, D_FOURIER, 2 * FFT_CHUNK), BF16),
        compiler_params=_cparams(("arbitrary",)),
        name="fmix_prep",
    )(w_fmix, cc, sc)


def _fourier_kernel(f_ref, m_ref, c0_ref, s0_ref, cb_ref, sb_ref, g_ref, o_ref, v_ref):
    k1 = pl.program_id(1)

    @pl.when(k1 == 0)
    def _():
        for c in range(D_FOURIER // FFT_CHUNK):
            mc = m_ref[c]
            z = [jnp.dot(f_ref[n1 * FFT_N2:(n1 + 1) * FFT_N2, :], mc, preferred_element_type=F32)
                 for n1 in range(FFT_N1)]
            zr = [t[:, :FFT_CHUNK] for t in z]
            zi = [t[:, FFT_CHUNK:] for t in z]
            ar, ai = zr[0] + zr[2], zi[0] + zi[2]
            br, bi = zr[1] + zr[3], zi[1] + zi[3]
            cr, ci = zr[0] - zr[2], zi[0] - zi[2]
            dr, di = zr[1] - zr[3], zi[1] - zi[3]
            u = [(ar + br, ai + bi), (cr + di, ci - dr), (ar - br, ai - bi), (cr - di, ci + dr)]
            for kk in range(FFT_N1):
                ur, ui = u[kk]
                cb, sb = cb_ref[kk], sb_ref[kk]
                v_ref[kk, :, c * FFT_CHUNK:(c + 1) * FFT_CHUNK] = (ur * cb + ui * sb).astype(BF16)
                v_ref[kk, :, D_FOURIER + c * FFT_CHUNK:D_FOURIER + (c + 1) * FFT_CHUNK] = (
                    ui * cb - ur * sb).astype(BF16)

    vr = v_ref[k1, :, :D_FOURIER]
    vi = v_ref[k1, :, D_FOURIER:]
    out = (jnp.dot(c0_ref[...], vr, preferred_element_type=F32)
           + jnp.dot(s0_ref[...], vi, preferred_element_type=F32))
    o_ref[...] = _rms(out, g_ref[...]).astype(BF16)


def _fourier(l, f, mst, c0, s0, cb, sb, g):
    b, s, _ = f.shape
    n_chunk = D_FOURIER // FFT_CHUNK
    out = pl.pallas_call(
        _fourier_kernel,
        grid=(b, FFT_N1),
        in_specs=[
            pl.BlockSpec((None, s, D_FOURIER), lambda i, j: (i, 0, 0)),
            pl.BlockSpec((None, n_chunk, D_FOURIER, 2 * FFT_CHUNK), lambda i, j: (l, 0, 0, 0)),
            pl.BlockSpec((FFT_N2, FFT_N2), lambda i, j: (0, 0)),
            pl.BlockSpec((FFT_N2, FFT_N2), lambda i, j: (0, 0)),
            pl.BlockSpec((FFT_N1, FFT_N2, V7X_LANES), lambda i, j: (0, 0, 0)),
            pl.BlockSpec((FFT_N1, FFT_N2, V7X_LANES), lambda i, j: (0, 0, 0)),
            _layer_vec(D_FOURIER, l),
        ],
        out_specs=pl.BlockSpec((None, FFT_N2, D_FOURIER), lambda i, j: (i, 0, j)),
        out_shape=jax.ShapeDtypeStruct((b, FFT_N2, FFT_N1 * D_FOURIER), BF16),
        scratch_shapes=[pltpu.VMEM((FFT_N1, FFT_N2, 2 * D_FOURIER), BF16)],
        compiler_params=_cparams(("arbitrary", "arbitrary")),
        name="fourier_mix",
    )(f, mst, c0, s0, cb, sb, g)
    return out.reshape(b * FFT_N2, FFT_N1 * D_FOURIER)


def _out_proj_body(a_ref, f_ref, x_ref, w_ref, g_ref, wbf_ref, fs_ref):
    @pl.when(pl.program_id(0) == 0)
    def _():
        wbf_ref[...] = w_ref[...].astype(BF16)

    return _out_proj_tile(a_ref, f_ref, x_ref, g_ref, wbf_ref, fs_ref)


def _out_proj_tile(a_ref, f_ref, x_ref, g_ref, wbf_ref, fs_ref):
    n_blk = D_FOURIER // V7X_LANES
    rows = x_ref.shape[0]
    for k1 in range(FFT_N1):
        for c in range(n_blk):
            col = k1 * D_FOURIER + c * V7X_LANES
            fs_ref[c, pl.ds(k1, rows // FFT_N1, stride=FFT_N1), :] = f_ref[:, col:col + V7X_LANES].astype(F32)
    fo = jnp.concatenate([fs_ref[c] for c in range(n_blk)], axis=1).astype(BF16)
    y = (jnp.dot(a_ref[...], wbf_ref[:D_ATTN, :], preferred_element_type=F32)
         + jnp.dot(fo, wbf_ref[D_ATTN:, :], preferred_element_type=F32))
    xn = x_ref[...] + y
    return xn, _rms(xn, g_ref[...])


def _out_proj_dense_kernel(a_ref, f_ref, x_ref, w_ref, g_ref, xo_ref, h_ref, wbf_ref, fs_ref):
    xn, h = _out_proj_body(a_ref, f_ref, x_ref, w_ref, g_ref, wbf_ref, fs_ref)
    xo_ref[...] = xn
    h_ref[...] = h.astype(BF16)


def _out_proj_moe_kernel(a_ref, f_ref, x_ref, w_ref, g_ref, wr_ref, lt_ref,
                         xo_ref, hrt_ref, gate_ref, rank_ref, cnt_ref,
                         wbf_ref, fs_ref, wr2_ref, hs_ref, carry_ref):
    @pl.when(pl.program_id(0) == 0)
    def _():
        hi, lo = _split_bf16(wr_ref[...])
        wr2_ref[:, :V7X_LANES] = hi
        wr2_ref[:, V7X_LANES:] = lo
        carry_ref[...] = jnp.zeros(carry_ref.shape, F32)

    xn, h = _out_proj_body(a_ref, f_ref, x_ref, w_ref, g_ref, wbf_ref, fs_ref)
    xo_ref[...] = xn
    for s in range(RT):
        hs_ref[pl.ds(s, TM, stride=RT), :] = h[:, s * V7X_LANES:(s + 1) * V7X_LANES]
    hrt_ref[...] = hs_ref[...].astype(BF16)
    h_hi, h_lo = _split_bf16(h)
    both = (jnp.dot(h_hi, wr2_ref[...], preferred_element_type=F32)
            + jnp.dot(h_lo, wr2_ref[...], preferred_element_type=F32))
    logits = both[:, :V7X_LANES] + both[:, V7X_LANES:]
    lane = lax.broadcasted_iota(jnp.int32, logits.shape, 1).astype(F32)
    logits = jnp.where(lane < N_EXPERTS, logits, -jnp.inf)
    m1 = jnp.max(logits, axis=-1, keepdims=True)
    i1 = jnp.min(jnp.where(logits == m1, lane, float(V7X_LANES)), axis=-1, keepdims=True)
    rest = jnp.where(lane == i1, -jnp.inf, logits)
    m2 = jnp.max(rest, axis=-1, keepdims=True)
    i2 = jnp.min(jnp.where(rest == m2, lane, float(V7X_LANES)), axis=-1, keepdims=True)
    e2 = jnp.exp(m2 - m1)
    den = 1.0 + e2
    sel1, sel2 = lane == i1, lane == i2
    sel = sel1 | sel2
    gate_ref[...] = jnp.where(sel1, 1.0 / den, jnp.where(sel2, e2 / den, 0.0))
    self32 = sel.astype(F32)
    rank = jnp.dot(lt_ref[...], self32.astype(BF16), preferred_element_type=F32) + carry_ref[...]
    rank_ref[...] = jnp.where(sel, rank, -1.0)
    carry_ref[...] = carry_ref[...] + jnp.sum(self32, axis=0, keepdims=True)
    cnt_ref[...] = carry_ref[...]


def _out_proj_specs(l):
    row = lambda i: (i, 0)
    return [
        pl.BlockSpec((TM, D_ATTN), row),
        pl.BlockSpec((TM // FFT_N1, FFT_N1 * D_FOURIER), row),
        pl.BlockSpec((TM, D_MODEL), row),
        pl.BlockSpec((None, D_MODEL, D_MODEL), lambda i: (l, 0, 0)),
        _layer_vec(D_MODEL, l),
    ]


def _out_proj(l, a, f, x, w, g):
    t = x.shape[0]
    row = lambda i: (i, 0)
    return pl.pallas_call(
        _out_proj_dense_kernel,
        grid=(t // TM,),
        in_specs=_out_proj_specs(l),
        out_specs=[pl.BlockSpec((TM, D_MODEL), row), pl.BlockSpec((TM, D_MODEL), row)],
        out_shape=[jax.ShapeDtypeStruct((t, D_MODEL), F32), jax.ShapeDtypeStruct((t, D_MODEL), BF16)],
        scratch_shapes=[pltpu.VMEM((D_MODEL, D_MODEL), BF16), pltpu.VMEM((D_FOURIER // V7X_LANES, TM, V7X_LANES), F32)],
        compiler_params=_cparams(("arbitrary",)),
        name="out_proj",
    )(a, f, x, w, g)


def _out_proj_route(l, li, a, f, x, w, g, w_router_pad, ltri):
    t = x.shape[0]
    row = lambda i: (i, 0)
    fixed = lambda i: (0, 0)
    in_specs = _out_proj_specs(l)
    return pl.pallas_call(
        _out_proj_moe_kernel,
        grid=(t // TM,),
        in_specs=in_specs + [pl.BlockSpec((None, D_MODEL, V7X_LANES), lambda i: (li, 0, 0)),
                             pl.BlockSpec((TM, TM), fixed)],
        out_specs=[
            pl.BlockSpec((TM, D_MODEL), row),
            pl.BlockSpec((TM * RT, V7X_LANES), row),
            pl.BlockSpec((TM, V7X_LANES), row),
            pl.BlockSpec((TM, V7X_LANES), row),
            pl.BlockSpec((1, V7X_LANES), fixed),
        ],
        out_shape=[
            jax.ShapeDtypeStruct((t, D_MODEL), F32),
            jax.ShapeDtypeStruct((t * RT, V7X_LANES), BF16),
            jax.ShapeDtypeStruct((t, V7X_LANES), F32),
            jax.ShapeDtypeStruct((t, V7X_LANES), F32),
            jax.ShapeDtypeStruct((1, V7X_LANES), F32),
        ],
        scratch_shapes=[
            pltpu.VMEM((D_MODEL, D_MODEL), BF16),
            pltpu.VMEM((D_FOURIER // V7X_LANES, TM, V7X_LANES), F32),
            pltpu.VMEM((D_MODEL, 2 * V7X_LANES), BF16),
            pltpu.VMEM((TM * RT, V7X_LANES), F32),
            pltpu.VMEM((1, V7X_LANES), F32),
        ],
        compiler_params=_cparams(("arbitrary",)),
        name="out_proj_route",
    )(a, f, x, w, g, w_router_pad, ltri)


def _ffn_kernel(x_ref, h_ref, wg_ref, wu_ref, wd_ref, o_ref, act_ref):
    @pl.when(pl.program_id(1) == 0)
    def _():
        o_ref[...] = x_ref[...]

    h = h_ref[...]
    for c in range(TF_FFN // V7X_MXU):
        cols = slice(c * V7X_MXU, (c + 1) * V7X_MXU)
        g = jnp.dot(h, wg_ref[:, cols].astype(BF16), preferred_element_type=F32)
        u = jnp.dot(h, wu_ref[:, cols].astype(BF16), preferred_element_type=F32)
        act_ref[:, cols] = (g * jax.nn.sigmoid(g) * u).astype(BF16)
    o_ref[...] += jnp.dot(act_ref[...], wd_ref[...].astype(BF16), preferred_element_type=F32)


def _ffn(li, x, h, wg, wu, wd):
    t = x.shape[0]
    return pl.pallas_call(
        _ffn_kernel,
        grid=(t // TM_FFN, D_FF // TF_FFN),
        in_specs=[
            pl.BlockSpec((TM_FFN, D_MODEL), lambda i, j: (i, 0)),
            pl.BlockSpec((TM_FFN, D_MODEL), lambda i, j: (i, 0)),
            pl.BlockSpec((None, D_MODEL, TF_FFN), lambda i, j: (li, 0, j)),
            pl.BlockSpec((None, D_MODEL, TF_FFN), lambda i, j: (li, 0, j)),
            pl.BlockSpec((None, TF_FFN, D_MODEL), lambda i, j: (li, j, 0)),
        ],
        out_specs=pl.BlockSpec((TM_FFN, D_MODEL), lambda i, j: (i, 0)),
        out_shape=jax.ShapeDtypeStruct((t, D_MODEL), F32),
        scratch_shapes=[pltpu.VMEM((TM_FFN, TF_FFN), BF16)],
        compiler_params=_cparams(("arbitrary", "arbitrary")),
        name="ffn_dense",
    )(x, h, wg, wu, wd)


def _plan_kernel(gate_ref, rank_ref, cnt_ref, pos_ref, wgt_ref, meta_ref):
    lane = lax.broadcasted_iota(jnp.int32, (1, V7X_LANES), 1)
    cnt = cnt_ref[...]
    padded = jnp.floor((cnt + (TM_MOE - 1)) * (1.0 / TM_MOE)) * TM_MOE
    base = jnp.zeros_like(padded)
    for j in range(1, N_EXPERTS):
        base = base + jnp.where(lane >= j, pltpu.roll(padded, j, 1), 0.0)
    rank = rank_ref[...]
    pos_t = jnp.where(rank >= 0.0, base + rank, -1.0).T
    gate_t = gate_ref[...].T
    seen = jnp.zeros((1, TM_PLAN), F32)
    pos_lo = jnp.zeros((1, TM_PLAN), F32)
    pos_hi = jnp.zeros((1, TM_PLAN), F32)
    w_lo = jnp.zeros((1, TM_PLAN), F32)
    w_hi = jnp.zeros((1, TM_PLAN), F32)
    for e in range(N_EXPERTS):
        p = pos_t[e:e + 1, :]
        gt = gate_t[e:e + 1, :]
        chosen = p >= 0.0
        first = chosen & (seen == 0.0)
        second = chosen & (seen == 1.0)
        pos_lo = jnp.where(first, p, pos_lo)
        pos_hi = jnp.where(second, p, pos_hi)
        w_lo = jnp.where(first, gt, w_lo)
        w_hi = jnp.where(second, gt, w_hi)
        seen = seen + chosen.astype(F32)
    pos_ref[0:1, :] = pos_lo.astype(jnp.int32)
    pos_ref[1:2, :] = pos_hi.astype(jnp.int32)
    wgt_ref[0:1, :] = w_lo
    wgt_ref[1:2, :] = w_hi
    ends = base + padded
    tile_start = (lane * TM_MOE).astype(F32)
    tile_e = jnp.zeros((1, V7X_LANES), F32)
    for e in range(N_EXPERTS):
        tile_e = tile_e + (ends[:, e:e + 1] <= tile_start).astype(F32)
    tile_e = jnp.minimum(tile_e, float(N_EXPERTS - 1))
    n_used = ends[:, N_EXPERTS - 1:N_EXPERTS] * (1.0 / TM_MOE) + jnp.zeros((1, V7X_LANES), F32)
    meta_ref[0:1, :] = tile_e.astype(jnp.int32)
    meta_ref[1:2, :] = n_used.astype(jnp.int32)
    meta_ref[2:3, :] = cnt.astype(jnp.int32)
    meta_ref[3:4, :] = base.astype(jnp.int32)
    meta_ref[4:8, :] = jnp.zeros((4, V7X_LANES), jnp.int32)


def _plan(gates, rank, counts):
    t = gates.shape[0]
    return pl.pallas_call(
        _plan_kernel,
        grid=(t // TM_PLAN,),
        in_specs=[
            pl.BlockSpec((TM_PLAN, V7X_LANES), lambda i: (i, 0)),
            pl.BlockSpec((TM_PLAN, V7X_LANES), lambda i: (i, 0)),
            pl.BlockSpec((1, V7X_LANES), lambda i: (0, 0)),
        ],
        out_specs=[
            pl.BlockSpec((2, TM_PLAN), lambda i: (0, i)),
            pl.BlockSpec((2, TM_PLAN), lambda i: (0, i)),
            pl.BlockSpec((8, V7X_LANES), lambda i: (0, 0)),
        ],
        out_shape=[
            jax.ShapeDtypeStruct((2, t), jnp.int32),
            jax.ShapeDtypeStruct((2, t), F32),
            jax.ShapeDtypeStruct((8, V7X_LANES), jnp.int32),
        ],
        compiler_params=_cparams(("arbitrary",)),
        name="moe_plan",
    )(gates, rank, counts)


def _rt_rows(ref, row, n=1, per=RT):
    start = row * per if isinstance(row, int) else pl.multiple_of(row * per, per)
    return ref.at[pl.ds(start, n * per)]


def _dispatch_kernel(plo_ref, phi_ref, cnt_ref, base_ref, nt_ref, h_ref, o_ref, z_ref, sem):
    i = pl.program_id(0)
    t0 = i * TM

    def issue(r, carry):
        src = _rt_rows(h_ref, r)
        pltpu.make_async_copy(src, _rt_rows(o_ref, plo_ref[t0 + r]), sem).start(priority=0)
        pltpu.make_async_copy(src, _rt_rows(o_ref, phi_ref[t0 + r]), sem).start(priority=1)
        return carry

    lax.fori_loop(0, TM, issue, 0, unroll=DMA_UNROLL)
    for _ in range(2):
        pltpu.make_async_copy(h_ref, _rt_rows(o_ref, 0, TM), sem).wait()

    @pl.when(i == pl.num_programs(0) - 1)
    def _():
        z_ref[...] = jnp.zeros(z_ref.shape, z_ref.dtype)
        for e in range(N_EXPERTS):
            n = cnt_ref[e]
            n_pad = ((n + (TM_MOE - 1)) // TM_MOE) * TM_MOE - n
            first = base_ref[e] + n

            def pad_issue(r, carry, first=first):
                pltpu.make_async_copy(_rt_rows(z_ref, 0), _rt_rows(o_ref, first + r), sem).start()
                return carry

            def pad_drain(r, carry):
                pltpu.make_async_copy(_rt_rows(z_ref, 0), _rt_rows(o_ref, 0), sem).wait()
                return carry

            lax.fori_loop(0, n_pad, pad_issue, 0)
            lax.fori_loop(0, n_pad, pad_drain, 0)

        def tail(tile, carry):
            cp = pltpu.make_async_copy(z_ref, _rt_rows(o_ref, tile * TM_MOE, TM_MOE), sem)
            cp.start()
            cp.wait()
            return carry

        lax.fori_loop(nt_ref[0], o_ref.shape[0] // (TM_MOE * RT), tail, 0)


def _dispatch(pos_lo, pos_hi, cnt, base, n_used, h_rt, n_rows):
    t = h_rt.shape[0] // RT
    return pl.pallas_call(
        _dispatch_kernel,
        grid_spec=pltpu.PrefetchScalarGridSpec(
            num_scalar_prefetch=5,
            grid=(t // TM,),
            in_specs=[pl.BlockSpec((TM * RT, V7X_LANES), lambda i, *_: (i, 0))],
            out_specs=pl.BlockSpec(memory_space=pl.ANY),
            scratch_shapes=[pltpu.VMEM((TM_MOE * RT, V7X_LANES), BF16), pltpu.SemaphoreType.DMA(())],
        ),
        out_shape=jax.ShapeDtypeStruct((n_rows * RT, V7X_LANES), BF16),
        compiler_params=_cparams(("arbitrary",)),
        name="moe_dispatch",
    )(pos_lo, pos_hi, cnt, base, n_used, h_rt)


def _tile_of(i, nt_ref):
    return jnp.minimum(i, nt_ref[0] - 1)


def _moe_up_kernel(te_ref, nt_ref, hs_ref, wg_ref, wu_ref, o_ref, xs_ref):
    i = pl.program_id(1)

    @pl.when(i < nt_ref[0])
    def _():
        xs_ref[...] = hs_ref[...].astype(F32)
        x = jnp.concatenate([xs_ref[pl.ds(s, TM_MOE, stride=RT), :].astype(BF16) for s in range(RT)],
                            axis=1)
        for c in range(TF_UP // V7X_MXU):
            cols = slice(c * V7X_MXU, (c + 1) * V7X_MXU)
            g = jnp.dot(x, wg_ref[:, cols].astype(BF16), preferred_element_type=F32)
            u = jnp.dot(x, wu_ref[:, cols].astype(BF16), preferred_element_type=F32)
            o_ref[:, cols] = (g * jax.nn.sigmoid(g) * u).astype(BF16)

    @pl.when(i >= nt_ref[0])
    def _():
        o_ref[...] = jnp.zeros(o_ref.shape, o_ref.dtype)


def _moe_up(li, tile_e, n_used, hs, wg, wu):
    n_rows = hs.shape[0] // RT
    nt = n_rows // TM_MOE
    wspec = pl.BlockSpec((None, None, D_MODEL, TF_UP), lambda j, i, te, n: (li, te[_tile_of(i, n)], 0, j))
    return pl.pallas_call(
        _moe_up_kernel,
        grid_spec=pltpu.PrefetchScalarGridSpec(
            num_scalar_prefetch=2,
            grid=(D_FF // TF_UP, nt),
            in_specs=[pl.BlockSpec((TM_MOE * RT, V7X_LANES), lambda j, i, te, n: (_tile_of(i, n), 0)), wspec, wspec],
            out_specs=pl.BlockSpec((TM_MOE, TF_UP), lambda j, i, te, n: (i, j)),
            scratch_shapes=[pltpu.VMEM((TM_MOE * RT, V7X_LANES), F32)],
        ),
        out_shape=jax.ShapeDtypeStruct((n_rows, D_FF), BF16),
        compiler_params=_cparams(("arbitrary", "arbitrary")),
        name="moe_up",
    )(tile_e, n_used, hs, wg, wu)


def _moe_down_kernel(te_ref, nt_ref, a_ref, wd_ref, o_ref):
    i = pl.program_id(0)

    @pl.when(i < nt_ref[0])
    def _():
        a = a_ref[...]
        for c in range(D_MODEL // V7X_MXU):
            y = jnp.dot(a, wd_ref[:, c * V7X_MXU:(c + 1) * V7X_MXU].astype(BF16), preferred_element_type=F32)
            for k in range(V7X_MXU // V7X_LANES):
                s = c * (V7X_MXU // V7X_LANES) + k
                o_ref[pl.ds(s, TM_MOE, stride=RT), :] = y[:, k * V7X_LANES:(k + 1) * V7X_LANES]

    @pl.when(i >= nt_ref[0])
    def _():
        o_ref[...] = jnp.zeros(o_ref.shape, o_ref.dtype)


def _moe_down(li, tile_e, n_used, act, wd):
    n_rows = act.shape[0]
    nt = n_rows // TM_MOE
    return pl.pallas_call(
        _moe_down_kernel,
        grid_spec=pltpu.PrefetchScalarGridSpec(
            num_scalar_prefetch=2,
            grid=(nt,),
            in_specs=[
                pl.BlockSpec((TM_MOE, D_FF), lambda i, te, n: (_tile_of(i, n), 0)),
                pl.BlockSpec((None, None, D_FF, D_MODEL), lambda i, te, n: (li, te[_tile_of(i, n)], 0, 0)),
            ],
            out_specs=pl.BlockSpec((TM_MOE * RT, V7X_LANES), lambda i, te, n: (i, 0)),
        ),
        out_shape=jax.ShapeDtypeStruct((n_rows * RT, V7X_LANES), F32),
        compiler_params=_cparams(("arbitrary",)),
        name="moe_down",
    )(tile_e, n_used, act, wd)


def _combine_kernel(plo_ref, phi_ref, x_ref, wlo_ref, whi_ref, y_ref, o_ref, lo_ref, hi_ref, sem):
    t0 = pl.program_id(0) * TM

    def issue(r, carry):
        pltpu.make_async_copy(_rt_rows(y_ref, plo_ref[t0 + r]), _rt_rows(lo_ref, r), sem).start(priority=0)
        pltpu.make_async_copy(_rt_rows(y_ref, phi_ref[t0 + r]), _rt_rows(hi_ref, r), sem).start(priority=1)
        return carry

    lax.fori_loop(0, TM, issue, 0, unroll=DMA_UNROLL)
    pltpu.make_async_copy(_rt_rows(y_ref, 0, TM), lo_ref, sem).wait()
    pltpu.make_async_copy(_rt_rows(y_ref, 0, TM), hi_ref, sem).wait()
    w_lo, w_hi = wlo_ref[...], whi_ref[...]
    for s in range(RT):
        cols = slice(s * V7X_LANES, (s + 1) * V7X_LANES)
        o_ref[:, cols] = x_ref[:, cols] + (w_lo * lo_ref[pl.ds(s, TM, stride=RT), :]
                                           + w_hi * hi_ref[pl.ds(s, TM, stride=RT), :])


def _combine(pos_lo, pos_hi, x, w_lo, w_hi, ys):
    t = x.shape[0]
    return pl.pallas_call(
        _combine_kernel,
        grid_spec=pltpu.PrefetchScalarGridSpec(
            num_scalar_prefetch=2,
            grid=(t // TM,),
            in_specs=[
                pl.BlockSpec((TM, D_MODEL), lambda i, a, b: (i, 0)),
                pl.BlockSpec((TM, 1), lambda i, a, b: (i, 0)),
                pl.BlockSpec((TM, 1), lambda i, a, b: (i, 0)),
                pl.BlockSpec(memory_space=pl.ANY),
            ],
            out_specs=pl.BlockSpec((TM, D_MODEL), lambda i, a, b: (i, 0)),
            scratch_shapes=[pltpu.VMEM((TM * RT, V7X_LANES), F32), pltpu.VMEM((TM * RT, V7X_LANES), F32),
                            pltpu.SemaphoreType.DMA(())],
        ),
        out_shape=jax.ShapeDtypeStruct((t, D_MODEL), F32),
        compiler_params=_cparams(("arbitrary",)),
        name="moe_combine",
    )(pos_lo, pos_hi, x, w_lo, w_hi, ys)


def _moe(li, x, h_rt, gates, rank, counts, e_gate, e_up, e_down):
    t = x.shape[0]
    n_rows = ((2 * t) // TM_MOE + N_EXPERTS) * TM_MOE
    pos, wgt, meta = _plan(gates, rank, counts)
    pos_lo, pos_hi = pos[0], pos[1]
    tile_e, n_used = meta[0], meta[1, :1]
    cnt, base = meta[2, :N_EXPERTS], meta[3, :N_EXPERTS]
    hs = _dispatch(pos_lo, pos_hi, cnt, base, n_used, h_rt, n_rows)
    act = _moe_up(li, tile_e, n_used, hs, e_gate, e_up)
    ys = _moe_down(li, tile_e, n_used, act, e_down)
    return pos_lo, pos_hi, x, wgt[0].reshape(t, 1), wgt[1].reshape(t, 1), ys


def _rope_tables(positions):
    inv_freq = ROPE_THETA ** (-jnp.arange(0, ROT_DIM, 2, dtype=F32) / ROT_DIM)
    ang = positions.astype(F32).reshape(-1, 1) * inv_freq
    cos, sin = jnp.cos(ang), jnp.sin(ang)
    t = cos.shape[0]
    half = ROT_DIM // 2
    ones = jnp.ones((t, HEAD_DIM - ROT_DIM), F32)
    zeros = jnp.zeros((t, HEAD_DIM - ROT_DIM), F32)
    zh = jnp.zeros((t, half), F32)
    reps = V7X_LANES // HEAD_DIM
    cos_t = jnp.tile(jnp.concatenate([cos, cos, ones], axis=1), (1, reps))
    sin_a = jnp.tile(jnp.concatenate([zh, sin, zeros], axis=1), (1, reps))
    sin_b = jnp.tile(jnp.concatenate([-sin, zh, zeros], axis=1), (1, reps))
    return cos_t, sin_a, sin_b


def _dft_tables(seq):
    n2 = jnp.arange(FFT_N2, dtype=jnp.int32)
    m = (n2[:, None] * n2[None, :]) % FFT_N2
    ang = m.astype(F32) * (2.0 * np.pi / FFT_N2)
    c0, s0 = jnp.cos(ang).astype(BF16), jnp.sin(ang).astype(BF16)
    k1 = jnp.arange(FFT_N1, dtype=jnp.int32)
    beta = (k1[:, None] * n2[None, :]).astype(F32) * (2.0 * np.pi / seq)
    cb = jnp.broadcast_to(jnp.cos(beta)[:, :, None], (FFT_N1, FFT_N2, V7X_LANES))
    sb = jnp.broadcast_to(jnp.sin(beta)[:, :, None], (FFT_N1, FFT_N2, V7X_LANES))
    c = jnp.arange(FOURIER_GROUP, dtype=jnp.int32)
    angc = ((c[:, None] * c[None, :]) % FOURIER_GROUP).astype(F32) * (2.0 * np.pi / FOURIER_GROUP)
    return c0, s0, cb, sb, jnp.cos(angc), jnp.sin(angc)


def kernel(x, positions, norm_mix, w_in, q_norm, k_norm, sink, w_fmix, g_attn_out, g_fourier_out, w_out, norm_ffn,
           ffn_gate, ffn_up, ffn_down, w_router, e_gate, e_up, e_down):
    b, s, d = x.shape
    depth = w_in.shape[0]
    assert (d, s) == (D_MODEL, FFT_N1 * FFT_N2) and (b * s) % TM_PLAN == 0
    t = b * s
    cos_t, sin_a, sin_b = _rope_tables(positions)
    c0, s0, cb, sb, cc, sc = _dft_tables(s)
    mst = _fmix_prep(w_fmix, cc, sc)
    head = jnp.arange(D_QK) // HEAD_DIM
    member = head[:, None] == jnp.arange(V7X_LANES)[None, :]
    head_sum = member.astype(BF16) * (1.0 / HEAD_DIM)
    head_expand = member.T.astype(BF16)
    row_i = jnp.arange(TM)
    ltri = (row_i[None, :] < row_i[:, None]).astype(BF16)
    gain = jnp.concatenate([jnp.tile(q_norm, (1, N_Q_HEADS)) * (HEAD_DIM ** -0.5),
                            jnp.tile(k_norm, (1, N_KV_HEADS))], axis=1).reshape(depth, 1, D_QK)
    vec = lambda p: p.reshape(depth, 1, p.shape[-1])
    norm_mix, norm_ffn, g_attn_out, g_fourier_out = vec(norm_mix), vec(norm_ffn), vec(g_attn_out), vec(g_fourier_out)
    wr = jnp.pad(w_router, ((0, 0), (0, 0), (0, V7X_LANES - N_EXPERTS)))
    xt = x.reshape(t, d)
    pending = None
    for l in range(depth):
        if pending is None:
            q, k, v, f = _in_proj(l, xt, norm_mix, w_in, head_sum, head_expand, gain, cos_t, sin_a, sin_b)
        else:
            xt, q, k, v, f = _combine_in_proj(l, pending, norm_mix, w_in, head_sum, head_expand, gain,
                                              cos_t, sin_a, sin_b)
            pending = None
        a = _attention(l, q.reshape(b, s, D_ATTN), k.reshape(b, s, D_KV), v.reshape(b, s, D_KV), sink, g_attn_out)
        fo = _fourier(l, f.reshape(b, s, D_FOURIER), mst, c0, s0, cb, sb, g_fourier_out)
        a2, f2 = a.reshape(t, D_ATTN), fo
        li = l // 2
        if l % 2 == 0:
            xt, h = _out_proj(l, a2, f2, xt, w_out, norm_ffn)
            xt = _ffn(li, xt, h, ffn_gate, ffn_up, ffn_down)
        else:
            xt, h_rt, gates, rank, counts = _out_proj_route(l, li, a2, f2, xt, w_out, norm_ffn, wr, ltri)
            pending = _moe(li, xt, h_rt, gates, rank, counts, e_gate, e_up, e_down)
    if pending is not None:
        xt = _combine(*pending)
    return xt.reshape(b, s, d)
```

```python
import functools

import jax
import jax.numpy as jnp
import numpy as np
from jax import lax
from jax.experimental import pallas as pl
from jax.experimental.pallas import tpu as pltpu

F32 = jnp.float32
BF16 = jnp.bfloat16

D_MODEL = 1024
HEAD_DIM = 64
D_ATTN = 512
N_Q_HEADS = 8
N_KV_HEADS = 2
Q_PER_KV = 4
D_KV = 128
D_FOURIER = 512
N_FOURIER_GROUPS = 8
FOURIER_GROUP = 64
D_QK = D_ATTN + D_KV
D_IN = D_ATTN + 2 * D_KV + D_FOURIER
WINDOW = 128
ROPE_THETA = 500000.0
ROT_DIM = 16
D_FF = 3584
N_EXPERTS = 8
EPS = 1e-6

V7X_LANES = 128
V7X_SUBLANES = 8
V7X_MXU = 256
V7X_VMEM_LIMIT = 56 * 1024 * 1024
RT = D_MODEL // V7X_LANES
assert RT == V7X_SUBLANES

TM = 1024
TQ = 1024
QB = 128
QB_PER_ITER = 1
KV_WIN = QB + 2 * WINDOW
FFT_N1 = 4
FFT_N2 = 1024
FFT_CHUNK = 128
TM_FFN = 1024
TF_FFN = 512
FFN_SLOTS = 3
TM_MOE = 512
TF_UP = D_FF // 2
TM_PLAN = 1024
TM_CMB = 512
DMA_UNROLL = 8


def _cparams(sem, vmem=V7X_VMEM_LIMIT):
    return pltpu.CompilerParams(dimension_semantics=sem, vmem_limit_bytes=vmem)


def _rms(xf, g):
    ms = jnp.mean(xf * xf, axis=-1, keepdims=True)
    return xf * lax.rsqrt(ms + EPS) * g


def _split_bf16(xf):
    hi = xf.astype(BF16)
    lo = (xf - hi.astype(F32)).astype(BF16)
    return hi, lo


def _layer_vec(n, l):
    return pl.BlockSpec((None, 1, n), lambda *_: (l, 0, 0))


def _in_proj_kernel(x_ref, g_ref, w_ref, hs_ref, he_ref, gain_ref, cos_ref, sa_ref, sb_ref,
                    q_ref, k_ref, v_ref, f_ref, wbf_ref):
    @pl.when(pl.program_id(0) == 0)
    def _():
        wbf_ref[...] = w_ref[...].astype(BF16)

    _in_proj_tile(x_ref[...], g_ref, hs_ref, he_ref, gain_ref, cos_ref, sa_ref, sb_ref,
                  q_ref, k_ref, v_ref, f_ref, wbf_ref)


def _in_proj_tile(x, g_ref, hs_ref, he_ref, gain_ref, cos_ref, sa_ref, sb_ref, q_ref, k_ref, v_ref, f_ref, wbf_ref):
    h = _rms(x, g_ref[...]).astype(BF16)
    z = jnp.dot(h, wbf_ref[...], preferred_element_type=F32)
    qk = z[:, :D_QK]
    hi, lo = _split_bf16(qk * qk)
    ms = (jnp.dot(hi, hs_ref[...], preferred_element_type=F32)
          + jnp.dot(lo, hs_ref[...], preferred_element_type=F32))
    r_hi, r_lo = _split_bf16(lax.rsqrt(ms + EPS))
    inv_rms = (jnp.dot(r_hi, he_ref[...], preferred_element_type=F32)
               + jnp.dot(r_lo, he_ref[...], preferred_element_type=F32))
    qkn = qk * inv_rms * gain_ref[...]
    cos_t, sin_a, sin_b = cos_ref[...], sa_ref[...], sb_ref[...]
    for c in range(D_QK // V7X_LANES):
        blk = qkn[:, c * V7X_LANES:(c + 1) * V7X_LANES]
        rot = (blk * cos_t + pltpu.roll(blk, ROT_DIM // 2, 1) * sin_a
               + pltpu.roll(blk, V7X_LANES - ROT_DIM // 2, 1) * sin_b).astype(BF16)
        if c < D_ATTN // V7X_LANES:
            q_ref[:, c * V7X_LANES:(c + 1) * V7X_LANES] = rot
        else:
            k_ref[...] = rot
    v_ref[...] = z[:, D_QK:D_QK + D_KV].astype(BF16)
    f_ref[...] = z[:, D_QK + D_KV:].astype(BF16)


def _in_proj_specs(l, tm, row, fixed):
    in_specs = [
        pl.BlockSpec((tm, D_MODEL), row),
        _layer_vec(D_MODEL, l),
        pl.BlockSpec((None, D_MODEL, D_IN), lambda i, *_: (l, 0, 0)),
        pl.BlockSpec((D_QK, V7X_LANES), fixed),
        pl.BlockSpec((V7X_LANES, D_QK), fixed),
        _layer_vec(D_QK, l),
        pl.BlockSpec((tm, V7X_LANES), row),
        pl.BlockSpec((tm, V7X_LANES), row),
        pl.BlockSpec((tm, V7X_LANES), row),
    ]
    out_specs = [
        pl.BlockSpec((tm, D_ATTN), row),
        pl.BlockSpec((tm, D_KV), row),
        pl.BlockSpec((tm, D_KV), row),
        pl.BlockSpec((tm, D_FOURIER), row),
    ]
    return in_specs, out_specs


def _in_proj_out_shapes(t):
    return [
        jax.ShapeDtypeStruct((t, D_ATTN), BF16),
        jax.ShapeDtypeStruct((t, D_KV), BF16),
        jax.ShapeDtypeStruct((t, D_KV), BF16),
        jax.ShapeDtypeStruct((t, D_FOURIER), BF16),
    ]


def _in_proj(l, x, g, w, head_sum, head_expand, gain, cos_t, sin_a, sin_b):
    t = x.shape[0]
    in_specs, out_specs = _in_proj_specs(l, TM, lambda i: (i, 0), lambda i: (0, 0))
    return pl.pallas_call(
        _in_proj_kernel,
        grid=(t // TM,),
        in_specs=in_specs,
        out_specs=out_specs,
        out_shape=_in_proj_out_shapes(t),
        scratch_shapes=[pltpu.VMEM((D_MODEL, D_IN), BF16)],
        compiler_params=_cparams(("arbitrary",)),
        name="in_proj",
    )(x, g, w, head_sum, head_expand, gain, cos_t, sin_a, sin_b)


def _combine_in_proj_kernel(plo_ref, phi_ref, x_ref, g_ref, w_ref, hs_ref, he_ref, gain_ref, cos_ref, sa_ref, sb_ref,
                            wlo_ref, whi_ref, y_ref, xo_ref, q_ref, k_ref, v_ref, f_ref,
                            wbf_ref, lo_ref, hi_ref, sems):
    i = pl.program_id(0)
    n = pl.num_programs(0)

    def gather(tile, slot):
        t0 = tile * TM_CMB
        for r in range(TM_CMB):
            pltpu.make_async_copy(_rt_rows(y_ref, plo_ref[t0 + r]), _rt_rows(lo_ref.at[slot], r),
                                  sems.at[slot]).start(priority=0)
            pltpu.make_async_copy(_rt_rows(y_ref, phi_ref[t0 + r]), _rt_rows(hi_ref.at[slot], r),
                                  sems.at[slot]).start(priority=1)

    def gather_wait(slot):
        pltpu.make_async_copy(_rt_rows(y_ref, 0, TM_CMB), lo_ref.at[slot], sems.at[slot]).wait()
        pltpu.make_async_copy(_rt_rows(y_ref, 0, TM_CMB), hi_ref.at[slot], sems.at[slot]).wait()

    @pl.when(i == 0)
    def _():
        wbf_ref[...] = w_ref[...].astype(BF16)

        def first(r, carry):
            pltpu.make_async_copy(_rt_rows(y_ref, plo_ref[r]), _rt_rows(lo_ref.at[0], r), sems.at[0]).start(priority=0)
            pltpu.make_async_copy(_rt_rows(y_ref, phi_ref[r]), _rt_rows(hi_ref.at[0], r), sems.at[0]).start(priority=1)
            return carry

        lax.fori_loop(0, TM_CMB, first, 0, unroll=DMA_UNROLL)

    slot = i % 2
    gather(jnp.minimum(i + 1, n - 1), 1 - slot)
    gather_wait(slot)
    w_lo, w_hi = wlo_ref[...], whi_ref[...]
    for s in range(RT):
        cols = slice(s * V7X_LANES, (s + 1) * V7X_LANES)
        xo_ref[:, cols] = x_ref[:, cols] + (w_lo * lo_ref[slot, pl.ds(s, TM_CMB, stride=RT), :]
                                             + w_hi * hi_ref[slot, pl.ds(s, TM_CMB, stride=RT), :])
    _in_proj_tile(xo_ref[...], g_ref, hs_ref, he_ref, gain_ref, cos_ref, sa_ref, sb_ref,
                  q_ref, k_ref, v_ref, f_ref, wbf_ref)

    @pl.when(i == n - 1)
    def _():
        gather_wait(1 - slot)


def _combine_in_proj(l, pending, g, w, head_sum, head_expand, gain, cos_t, sin_a, sin_b):
    pos_lo, pos_hi, x, w_lo, w_hi, ys = pending
    t = x.shape[0]
    row = lambda i, a, b: (i, 0)
    in_specs, out_specs = _in_proj_specs(l, TM_CMB, row, lambda i, a, b: (0, 0))
    return pl.pallas_call(
        _combine_in_proj_kernel,
        grid_spec=pltpu.PrefetchScalarGridSpec(
            num_scalar_prefetch=2,
            grid=(t // TM_CMB,),
            in_specs=in_specs + [pl.BlockSpec((TM_CMB, 1), row), pl.BlockSpec((TM_CMB, 1), row),
                                 pl.BlockSpec(memory_space=pl.ANY)],
            out_specs=[pl.BlockSpec((TM_CMB, D_MODEL), row)] + out_specs,
            scratch_shapes=[
                pltpu.VMEM((D_MODEL, D_IN), BF16),
                pltpu.VMEM((2, TM_CMB * RT, V7X_LANES), F32),
                pltpu.VMEM((2, TM_CMB * RT, V7X_LANES), F32),
                pltpu.SemaphoreType.DMA((2,)),
            ],
        ),
        out_shape=[jax.ShapeDtypeStruct((t, D_MODEL), F32)] + _in_proj_out_shapes(t),
        compiler_params=_cparams(("arbitrary",)),
        name="combine_in_proj",
    )(pos_lo, pos_hi, x, g, w, head_sum, head_expand, gain, cos_t, sin_a, sin_b, w_lo, w_hi, ys)


def _attn_kernel(sink_ref, q_ref, k_ref, v_ref, g_ref, o_ref, s_ref, p_ref, *, seq, layer):
    q0 = pl.program_id(1) * TQ
    gain = g_ref[...]
    heads = range(N_Q_HEADS)

    def blocks(j, carry):
        units = [(b, h) for b in range(QB_PER_ITER) for h in heads]
        r0, kw, vw, mask, q = [], [], [], [], []
        for b in range(QB_PER_ITER):
            r = pl.multiple_of((j * QB_PER_ITER + b) * QB, QB)
            qs = q0 + r
            ws = pl.multiple_of(jnp.clip(qs - WINDOW, 0, seq - KV_WIN), QB)
            r0.append(r)
            kw.append(k_ref[pl.ds(ws, KV_WIN), :])
            vw.append(v_ref[pl.ds(ws, KV_WIN), :])
            q_abs = qs + lax.broadcasted_iota(jnp.int32, (QB, KV_WIN), 0)
            k_abs = ws + lax.broadcasted_iota(jnp.int32, (QB, KV_WIN), 1)
            mask.append(jnp.abs(q_abs - k_abs) <= WINDOW)
            q.append(q_ref[pl.ds(r, QB), :])
        for b, h in units:
            kvh = h // Q_PER_KV
            qh = q[b][:, h * HEAD_DIM:(h + 1) * HEAD_DIM]
            kh = kw[b][:, kvh * HEAD_DIM:(kvh + 1) * HEAD_DIM]
            s = lax.dot_general(qh, kh, (((1,), (1,)), ((), ())), preferred_element_type=F32)
            s_ref[b, h] = jnp.where(mask[b], s, -jnp.inf)
        m = {(b, h): jnp.maximum(jnp.max(s_ref[b, h], axis=-1, keepdims=True), sink_ref[layer, h])
             for b, h in units}
        inv = {}
        for b, h in units:
            p = jnp.exp(s_ref[b, h] - m[b, h])
            inv[b, h] = 1.0 / (jnp.sum(p, axis=-1, keepdims=True) + jnp.exp(sink_ref[layer, h] - m[b, h]))
            p_ref[b, h] = p.astype(BF16)
        for b in range(QB_PER_ITER):
            outs = []
            for h in heads:
                kvh = h // Q_PER_KV
                vh = vw[b][:, kvh * HEAD_DIM:(kvh + 1) * HEAD_DIM]
                outs.append(jnp.dot(p_ref[b, h], vh, preferred_element_type=F32) * inv[b, h])
            a = jnp.concatenate(outs, axis=-1)
            o_ref[pl.ds(r0[b], QB), :] = _rms(a, gain).astype(BF16)
        return carry

    lax.fori_loop(0, TQ // (QB * QB_PER_ITER), blocks, 0)


def _attention(l, q, k, v, sink, g):
    b, s, _ = q.shape
    return pl.pallas_call(
        functools.partial(_attn_kernel, seq=s, layer=l),
        grid=(b, s // TQ),
        in_specs=[
            pl.BlockSpec(memory_space=pltpu.SMEM),
            pl.BlockSpec((None, TQ, D_ATTN), lambda i, j: (i, j, 0)),
            pl.BlockSpec((None, s, D_KV), lambda i, j: (i, 0, 0)),
            pl.BlockSpec((None, s, D_KV), lambda i, j: (i, 0, 0)),
            _layer_vec(D_ATTN, l),
        ],
        out_specs=pl.BlockSpec((None, TQ, D_ATTN), lambda i, j: (i, j, 0)),
        out_shape=jax.ShapeDtypeStruct((b, s, D_ATTN), BF16),
        scratch_shapes=[pltpu.VMEM((QB_PER_ITER, N_Q_HEADS, QB, KV_WIN), F32),
                        pltpu.VMEM((QB_PER_ITER, N_Q_HEADS, QB, KV_WIN), BF16)],
        compiler_params=_cparams(("arbitrary", "arbitrary")),
        name="band_attn",
    )(sink, q, k, v, g)


def _fmix_prep_kernel(w_ref, cc_ref, sc_ref, o_ref):
    o_ref[...] = jnp.zeros(o_ref.shape, o_ref.dtype)
    scale = float(FFT_N1 * FFT_N2 * FOURIER_GROUP) ** -0.5
    for g in range(N_FOURIER_GROUPS):
        w = w_ref[g]
        mr = jnp.dot(cc_ref[...], w, preferred_element_type=F32, precision=lax.Precision.HIGHEST) * scale
        mi = jnp.dot(sc_ref[...], w, preferred_element_type=F32, precision=lax.Precision.HIGHEST) * (-scale)
        c = (g * FOURIER_GROUP) // FFT_CHUNK
        off = (g * FOURIER_GROUP) % FFT_CHUNK
        rows = slice(g * FOURIER_GROUP, (g + 1) * FOURIER_GROUP)
        o_ref[c, rows, off:off + FOURIER_GROUP] = mr.astype(BF16)
        o_ref[c, rows, FFT_CHUNK + off:FFT_CHUNK + off + FOURIER_GROUP] = mi.astype(BF16)


def _fmix_prep(w_fmix, cc, sc):
    depth = w_fmix.shape[0]
    n_chunk = D_FOURIER // FFT_CHUNK
    return pl.pallas_call(
        _fmix_prep_kernel,
        grid=(depth,),
        in_specs=[
            pl.BlockSpec((None, N_FOURIER_GROUPS, FOURIER_GROUP, FOURIER_GROUP), lambda l: (l, 0, 0, 0)),
            pl.BlockSpec((FOURIER_GROUP, FOURIER_GROUP), lambda l: (0, 0)),
            pl.BlockSpec((FOURIER_GROUP, FOURIER_GROUP), lambda l: (0, 0)),
        ],
        out_specs=pl.BlockSpec((None, n_chunk, D_FOURIER, 2 * FFT_CHUNK), lambda l: (l, 0, 0, 0)),
        out_shape=jax.ShapeDtypeStruct((depth, n_chunk, D_FOURIER, 2 * FFT_CHUNK), BF16),
        compiler_params=_cparams(("arbitrary",)),
        name="fmix_prep",
    )(w_fmix, cc, sc)


def _fourier_kernel(f_ref, m_ref, c0_ref, s0_ref, cb_ref, sb_ref, g_ref, o_ref, v_ref):
    k1 = pl.program_id(1)

    @pl.when(k1 == 0)
    def _():
        for c in range(D_FOURIER // FFT_CHUNK):
            mc = m_ref[c]
            z = [jnp.dot(f_ref[n1 * FFT_N2:(n1 + 1) * FFT_N2, :], mc, preferred_element_type=F32)
                 for n1 in range(FFT_N1)]
            zr = [t[:, :FFT_CHUNK] for t in z]
            zi = [t[:, FFT_CHUNK:] for t in z]
            ar, ai = zr[0] + zr[2], zi[0] + zi[2]
            br, bi = zr[1] + zr[3], zi[1] + zi[3]
            cr, ci = zr[0] - zr[2], zi[0] - zi[2]
            dr, di = zr[1] - zr[3], zi[1] - zi[3]
            u = [(ar + br, ai + bi), (cr + di, ci - dr), (ar - br, ai - bi), (cr - di, ci + dr)]
            for kk in range(FFT_N1):
                ur, ui = u[kk]
                cb, sb = cb_ref[kk], sb_ref[kk]
                v_ref[kk, :, c * FFT_CHUNK:(c + 1) * FFT_CHUNK] = (ur * cb + ui * sb).astype(BF16)
                v_ref[kk, :, D_FOURIER + c * FFT_CHUNK:D_FOURIER + (c + 1) * FFT_CHUNK] = (
                    ui * cb - ur * sb).astype(BF16)

    vr = v_ref[k1, :, :D_FOURIER]
    vi = v_ref[k1, :, D_FOURIER:]
    out = (jnp.dot(c0_ref[...], vr, preferred_element_type=F32)
           + jnp.dot(s0_ref[...], vi, preferred_element_type=F32))
    o_ref[...] = _rms(out, g_ref[...]).astype(BF16)


def _fourier(l, f, mst, c0, s0, cb, sb, g):
    b, s, _ = f.shape
    n_chunk = D_FOURIER // FFT_CHUNK
    out = pl.pallas_call(
        _fourier_kernel,
        grid=(b, FFT_N1),
        in_specs=[
            pl.BlockSpec((None, s, D_FOURIER), lambda i, j: (i, 0, 0)),
            pl.BlockSpec((None, n_chunk, D_FOURIER, 2 * FFT_CHUNK), lambda i, j: (l, 0, 0, 0)),
            pl.BlockSpec((FFT_N2, FFT_N2), lambda i, j: (0, 0)),
            pl.BlockSpec((FFT_N2, FFT_N2), lambda i, j: (0, 0)),
            pl.BlockSpec((FFT_N1, FFT_N2, V7X_LANES), lambda i, j: (0, 0, 0)),
            pl.BlockSpec((FFT_N1, FFT_N2, V7X_LANES), lambda i, j: (0, 0, 0)),
            _layer_vec(D_FOURIER, l),
        ],
        out_specs=pl.BlockSpec((None, FFT_N2, D_FOURIER), lambda i, j: (i, 0, j)),
        out_shape=jax.ShapeDtypeStruct((b, FFT_N2, FFT_N1 * D_FOURIER), BF16),
        scratch_shapes=[pltpu.VMEM((FFT_N1, FFT_N2, 2 * D_FOURIER), BF16)],
        compiler_params=_cparams(("arbitrary", "arbitrary")),
        name="fourier_mix",
    )(f, mst, c0, s0, cb, sb, g)
    return out.reshape(b * FFT_N2, FFT_N1 * D_FOURIER)


def _out_proj_body(a_ref, f_ref, x_ref, w_ref, g_ref, wbf_ref, fs_ref):
    @pl.when(pl.program_id(0) == 0)
    def _():
        wbf_ref[...] = w_ref[...].astype(BF16)

    return _out_proj_tile(a_ref, f_ref, x_ref, g_ref, wbf_ref, fs_ref)


def _out_proj_tile(a_ref, f_ref, x_ref, g_ref, wbf_ref, fs_ref):
    n_blk = D_FOURIER // V7X_LANES
    rows = x_ref.shape[0]
    for k1 in range(FFT_N1):
        for c in range(n_blk):
            col = k1 * D_FOURIER + c * V7X_LANES
            fs_ref[c, pl.ds(k1, rows // FFT_N1, stride=FFT_N1), :] = f_ref[:, col:col + V7X_LANES].astype(F32)
    fo = jnp.concatenate([fs_ref[c] for c in range(n_blk)], axis=1).astype(BF16)
    y = (jnp.dot(a_ref[...], wbf_ref[:D_ATTN, :], preferred_element_type=F32)
         + jnp.dot(fo, wbf_ref[D_ATTN:, :], preferred_element_type=F32))
    xn = x_ref[...] + y
    return xn, _rms(xn, g_ref[...])


def _out_proj_dense_kernel(a_ref, f_ref, x_ref, w_ref, g_ref, xo_ref, h_ref, wbf_ref, fs_ref):
    xn, h = _out_proj_body(a_ref, f_ref, x_ref, w_ref, g_ref, wbf_ref, fs_ref)
    xo_ref[...] = xn
    h_ref[...] = h.astype(BF16)


def _out_proj_moe_kernel(a_ref, f_ref, x_ref, w_ref, g_ref, wr_ref, lt_ref,
                         xo_ref, hrt_ref, gate_ref, rank_ref, cnt_ref,
                         wbf_ref, fs_ref, wr2_ref, hs_ref, carry_ref):
    @pl.when(pl.program_id(0) == 0)
    def _():
        hi, lo = _split_bf16(wr_ref[...])
        wr2_ref[:, :V7X_LANES] = hi
        wr2_ref[:, V7X_LANES:] = lo
        carry_ref[...] = jnp.zeros(carry_ref.shape, F32)

    xn, h = _out_proj_body(a_ref, f_ref, x_ref, w_ref, g_ref, wbf_ref, fs_ref)
    xo_ref[...] = xn
    for s in range(RT):
        hs_ref[pl.ds(s, TM, stride=RT), :] = h[:, s * V7X_LANES:(s + 1) * V7X_LANES]
    hrt_ref[...] = hs_ref[...].astype(BF16)
    h_hi, h_lo = _split_bf16(h)
    both = (jnp.dot(h_hi, wr2_ref[...], preferred_element_type=F32)
            + jnp.dot(h_lo, wr2_ref[...], preferred_element_type=F32))
    logits = both[:, :V7X_LANES] + both[:, V7X_LANES:]
    lane = lax.broadcasted_iota(jnp.int32, logits.shape, 1).astype(F32)
    logits = jnp.where(lane < N_EXPERTS, logits, -jnp.inf)
    m1 = jnp.max(logits, axis=-1, keepdims=True)
    i1 = jnp.min(jnp.where(logits == m1, lane, float(V7X_LANES)), axis=-1, keepdims=True)
    rest = jnp.where(lane == i1, -jnp.inf, logits)
    m2 = jnp.max(rest, axis=-1, keepdims=True)
    i2 = jnp.min(jnp.where(rest == m2, lane, float(V7X_LANES)), axis=-1, keepdims=True)
    e2 = jnp.exp(m2 - m1)
    den = 1.0 + e2
    sel1, sel2 = lane == i1, lane == i2
    sel = sel1 | sel2
    gate_ref[...] = jnp.where(sel1, 1.0 / den, jnp.where(sel2, e2 / den, 0.0))
    self32 = sel.astype(F32)
    rank = jnp.dot(lt_ref[...], self32.astype(BF16), preferred_element_type=F32) + carry_ref[...]
    rank_ref[...] = jnp.where(sel, rank, -1.0)
    carry_ref[...] = carry_ref[...] + jnp.sum(self32, axis=0, keepdims=True)
    cnt_ref[...] = carry_ref[...]


def _out_proj_specs(l):
    row = lambda i: (i, 0)
    return [
        pl.BlockSpec((TM, D_ATTN), row),
        pl.BlockSpec((TM // FFT_N1, FFT_N1 * D_FOURIER), row),
        pl.BlockSpec((TM, D_MODEL), row),
        pl.BlockSpec((None, D_MODEL, D_MODEL), lambda i: (l, 0, 0)),
        _layer_vec(D_MODEL, l),
    ]


def _out_proj(l, a, f, x, w, g):
    t = x.shape[0]
    row = lambda i: (i, 0)
    return pl.pallas_call(
        _out_proj_dense_kernel,
        grid=(t // TM,),
        in_specs=_out_proj_specs(l),
        out_specs=[pl.BlockSpec((TM, D_MODEL), row), pl.BlockSpec((TM, D_MODEL), row)],
        out_shape=[jax.ShapeDtypeStruct((t, D_MODEL), F32), jax.ShapeDtypeStruct((t, D_MODEL), BF16)],
        scratch_shapes=[pltpu.VMEM((D_MODEL, D_MODEL), BF16), pltpu.VMEM((D_FOURIER // V7X_LANES, TM, V7X_LANES), F32)],
        compiler_params=_cparams(("arbitrary",)),
        name="out_proj",
    )(a, f, x, w, g)


def _out_proj_route(l, li, a, f, x, w, g, w_router_pad, ltri):
    t = x.shape[0]
    row = lambda i: (i, 0)
    fixed = lambda i: (0, 0)
    in_specs = _out_proj_specs(l)
    return pl.pallas_call(
        _out_proj_moe_kernel,
        grid=(t // TM,),
        in_specs=in_specs + [pl.BlockSpec((None, D_MODEL, V7X_LANES), lambda i: (li, 0, 0)),
                             pl.BlockSpec((TM, TM), fixed)],
        out_specs=[
            pl.BlockSpec((TM, D_MODEL), row),
            pl.BlockSpec((TM * RT, V7X_LANES), row),
            pl.BlockSpec((TM, V7X_LANES), row),
            pl.BlockSpec((TM, V7X_LANES), row),
            pl.BlockSpec((1, V7X_LANES), fixed),
        ],
        out_shape=[
            jax.ShapeDtypeStruct((t, D_MODEL), F32),
            jax.ShapeDtypeStruct((t * RT, V7X_LANES), BF16),
            jax.ShapeDtypeStruct((t, V7X_LANES), F32),
            jax.ShapeDtypeStruct((t, V7X_LANES), F32),
            jax.ShapeDtypeStruct((1, V7X_LANES), F32),
        ],
        scratch_shapes=[
            pltpu.VMEM((D_MODEL, D_MODEL), BF16),
            pltpu.VMEM((D_FOURIER // V7X_LANES, TM, V7X_LANES), F32),
            pltpu.VMEM((D_MODEL, 2 * V7X_LANES), BF16),
            pltpu.VMEM((TM * RT, V7X_LANES), F32),
            pltpu.VMEM((1, V7X_LANES), F32),
        ],
        compiler_params=_cparams(("arbitrary",)),
        name="out_proj_route",
    )(a, f, x, w, g, w_router_pad, ltri)


def _ffn_kernel(x_ref, h_ref, wg_hbm, wu_hbm, wd_hbm, o_ref, act_ref, wg_buf, wu_buf, wd_buf, sems, *, layer):
    n_j = pl.num_programs(1)
    total = pl.num_programs(0) * n_j
    step = pl.program_id(0) * n_j + pl.program_id(1)

    def copies(s):
        j = s % n_j
        slot = s % FFN_SLOTS
        cols = pl.ds(pl.multiple_of(j * TF_FFN, TF_FFN), TF_FFN)
        return (pltpu.make_async_copy(wg_hbm.at[layer, :, cols], wg_buf.at[slot], sems.at[slot]),
                pltpu.make_async_copy(wu_hbm.at[layer, :, cols], wu_buf.at[slot], sems.at[slot]),
                pltpu.make_async_copy(wd_hbm.at[layer, cols, :], wd_buf.at[slot], sems.at[slot]))

    @pl.when(step == 0)
    def _():
        for s in range(FFN_SLOTS - 1):
            for cp in copies(s):
                cp.start()

    @pl.when(step + (FFN_SLOTS - 1) < total)
    def _():
        for cp in copies(step + (FFN_SLOTS - 1)):
            cp.start()

    @pl.when(pl.program_id(1) == 0)
    def _():
        o_ref[...] = x_ref[...]

    for cp in copies(step):
        cp.wait()
    slot = step % FFN_SLOTS
    h = h_ref[...]
    for c in range(TF_FFN // V7X_MXU):
        cols = slice(c * V7X_MXU, (c + 1) * V7X_MXU)
        g = jnp.dot(h, wg_buf[slot, :, cols].astype(BF16), preferred_element_type=F32)
        u = jnp.dot(h, wu_buf[slot, :, cols].astype(BF16), preferred_element_type=F32)
        act_ref[:, cols] = (g * jax.nn.sigmoid(g) * u).astype(BF16)
    o_ref[...] += jnp.dot(act_ref[...], wd_buf[slot].astype(BF16), preferred_element_type=F32)


def _ffn(li, x, h, wg, wu, wd):
    t = x.shape[0]
    hbm = pl.BlockSpec(memory_space=pl.ANY)
    return pl.pallas_call(
        functools.partial(_ffn_kernel, layer=li),
        grid=(t // TM_FFN, D_FF // TF_FFN),
        in_specs=[
            pl.BlockSpec((TM_FFN, D_MODEL), lambda i, j: (i, 0)),
            pl.BlockSpec((TM_FFN, D_MODEL), lambda i, j: (i, 0)),
            hbm, hbm, hbm,
        ],
        out_specs=pl.BlockSpec((TM_FFN, D_MODEL), lambda i, j: (i, 0)),
        out_shape=jax.ShapeDtypeStruct((t, D_MODEL), F32),
        scratch_shapes=[
            pltpu.VMEM((TM_FFN, TF_FFN), BF16),
            pltpu.VMEM((FFN_SLOTS, D_MODEL, TF_FFN), F32),
            pltpu.VMEM((FFN_SLOTS, D_MODEL, TF_FFN), F32),
            pltpu.VMEM((FFN_SLOTS, TF_FFN, D_MODEL), F32),
            pltpu.SemaphoreType.DMA((FFN_SLOTS,)),
        ],
        compiler_params=_cparams(("arbitrary", "arbitrary")),
        name="ffn_dense",
    )(x, h, wg, wu, wd)


def _plan_kernel(gate_ref, rank_ref, cnt_ref, pos_ref, wgt_ref, meta_ref):
    lane = lax.broadcasted_iota(jnp.int32, (1, V7X_LANES), 1)
    cnt = cnt_ref[...]
    padded = jnp.floor((cnt + (TM_MOE - 1)) * (1.0 / TM_MOE)) * TM_MOE
    base = jnp.zeros_like(padded)
    for j in range(1, N_EXPERTS):
        base = base + jnp.where(lane >= j, pltpu.roll(padded, j, 1), 0.0)
    rank = rank_ref[...]
    pos_t = jnp.where(rank >= 0.0, base + rank, -1.0).T
    gate_t = gate_ref[...].T
    seen = jnp.zeros((1, TM_PLAN), F32)
    pos_lo = jnp.zeros((1, TM_PLAN), F32)
    pos_hi = jnp.zeros((1, TM_PLAN), F32)
    w_lo = jnp.zeros((1, TM_PLAN), F32)
    w_hi = jnp.zeros((1, TM_PLAN), F32)
    for e in range(N_EXPERTS):
        p = pos_t[e:e + 1, :]
        gt = gate_t[e:e + 1, :]
        chosen = p >= 0.0
        first = chosen & (seen == 0.0)
        second = chosen & (seen == 1.0)
        pos_lo = jnp.where(first, p, pos_lo)
        pos_hi = jnp.where(second, p, pos_hi)
        w_lo = jnp.where(first, gt, w_lo)
        w_hi = jnp.where(second, gt, w_hi)
        seen = seen + chosen.astype(F32)
    pos_ref[0:1, :] = pos_lo.astype(jnp.int32)
    pos_ref[1:2, :] = pos_hi.astype(jnp.int32)
    wgt_ref[0:1, :] = w_lo
    wgt_ref[1:2, :] = w_hi
    ends = base + padded
    tile_start = (lane * TM_MOE).astype(F32)
    tile_e = jnp.zeros((1, V7X_LANES), F32)
    for e in range(N_EXPERTS):
        tile_e = tile_e + (ends[:, e:e + 1] <= tile_start).astype(F32)
    tile_e = jnp.minimum(tile_e, float(N_EXPERTS - 1))
    n_used = ends[:, N_EXPERTS - 1:N_EXPERTS] * (1.0 / TM_MOE) + jnp.zeros((1, V7X_LANES), F32)
    meta_ref[0:1, :] = tile_e.astype(jnp.int32)
    meta_ref[1:2, :] = n_used.astype(jnp.int32)
    meta_ref[2:3, :] = cnt.astype(jnp.int32)
    meta_ref[3:4, :] = base.astype(jnp.int32)
    meta_ref[4:8, :] = jnp.zeros((4, V7X_LANES), jnp.int32)


def _plan(gates, rank, counts):
    t = gates.shape[0]
    return pl.pallas_call(
        _plan_kernel,
        grid=(t // TM_PLAN,),
        in_specs=[
            pl.BlockSpec((TM_PLAN, V7X_LANES), lambda i: (i, 0)),
            pl.BlockSpec((TM_PLAN, V7X_LANES), lambda i: (i, 0)),
            pl.BlockSpec((1, V7X_LANES), lambda i: (0, 0)),
        ],
        out_specs=[
            pl.BlockSpec((2, TM_PLAN), lambda i: (0, i)),
            pl.BlockSpec((2, TM_PLAN), lambda i: (0, i)),
            pl.BlockSpec((8, V7X_LANES), lambda i: (0, 0)),
        ],
        out_shape=[
            jax.ShapeDtypeStruct((2, t), jnp.int32),
            jax.ShapeDtypeStruct((2, t), F32),
            jax.ShapeDtypeStruct((8, V7X_LANES), jnp.int32),
        ],
        compiler_params=_cparams(("arbitrary",)),
        name="moe_plan",
    )(gates, rank, counts)


def _rt_rows(ref, row, n=1, per=RT):
    start = row * per if isinstance(row, int) else pl.multiple_of(row * per, per)
    return ref.at[pl.ds(start, n * per)]


def _dispatch_kernel(plo_ref, phi_ref, cnt_ref, base_ref, nt_ref, h_ref, o_ref, z_ref, sem):
    i = pl.program_id(0)
    t0 = i * TM

    def issue(r, carry):
        src = _rt_rows(h_ref, r)
        pltpu.make_async_copy(src, _rt_rows(o_ref, plo_ref[t0 + r]), sem).start(priority=0)
        pltpu.make_async_copy(src, _rt_rows(o_ref, phi_ref[t0 + r]), sem).start(priority=1)
        return carry

    lax.fori_loop(0, TM, issue, 0, unroll=DMA_UNROLL)
    for _ in range(2):
        pltpu.make_async_copy(h_ref, _rt_rows(o_ref, 0, TM), sem).wait()

    @pl.when(i == pl.num_programs(0) - 1)
    def _():
        z_ref[...] = jnp.zeros(z_ref.shape, z_ref.dtype)
        for e in range(N_EXPERTS):
            n = cnt_ref[e]
            n_pad = ((n + (TM_MOE - 1)) // TM_MOE) * TM_MOE - n
            first = base_ref[e] + n

            def pad_issue(r, carry, first=first):
                pltpu.make_async_copy(_rt_rows(z_ref, 0), _rt_rows(o_ref, first + r), sem).start()
                return carry

            def pad_drain(r, carry):
                pltpu.make_async_copy(_rt_rows(z_ref, 0), _rt_rows(o_ref, 0), sem).wait()
                return carry

            lax.fori_loop(0, n_pad, pad_issue, 0)
            lax.fori_loop(0, n_pad, pad_drain, 0)

        def tail(tile, carry):
            cp = pltpu.make_async_copy(z_ref, _rt_rows(o_ref, tile * TM_MOE, TM_MOE), sem)
            cp.start()
            cp.wait()
            return carry

        lax.fori_loop(nt_ref[0], o_ref.shape[0] // (TM_MOE * RT), tail, 0)


def _dispatch(pos_lo, pos_hi, cnt, base, n_used, h_rt, n_rows):
    t = h_rt.shape[0] // RT
    return pl.pallas_call(
        _dispatch_kernel,
        grid_spec=pltpu.PrefetchScalarGridSpec(
            num_scalar_prefetch=5,
            grid=(t // TM,),
            in_specs=[pl.BlockSpec((TM * RT, V7X_LANES), lambda i, *_: (i, 0))],
            out_specs=pl.BlockSpec(memory_space=pl.ANY),
            scratch_shapes=[pltpu.VMEM((TM_MOE * RT, V7X_LANES), BF16), pltpu.SemaphoreType.DMA(())],
        ),
        out_shape=jax.ShapeDtypeStruct((n_rows * RT, V7X_LANES), BF16),
        compiler_params=_cparams(("arbitrary",)),
        name="moe_dispatch",
    )(pos_lo, pos_hi, cnt, base, n_used, h_rt)


def _tile_of(i, nt_ref):
    return jnp.minimum(i, nt_ref[0] - 1)


def _moe_up_kernel(te_ref, nt_ref, hs_ref, wg_ref, wu_ref, o_ref, xs_ref):
    i = pl.program_id(1)

    @pl.when(i < nt_ref[0])
    def _():
        xs_ref[...] = hs_ref[...].astype(F32)
        x = jnp.concatenate([xs_ref[pl.ds(s, TM_MOE, stride=RT), :].astype(BF16) for s in range(RT)],
                            axis=1)
        for c in range(TF_UP // V7X_MXU):
            cols = slice(c * V7X_MXU, (c + 1) * V7X_MXU)
            g = jnp.dot(x, wg_ref[:, cols].astype(BF16), preferred_element_type=F32)
            u = jnp.dot(x, wu_ref[:, cols].astype(BF16), preferred_element_type=F32)
            o_ref[:, cols] = (g * jax.nn.sigmoid(g) * u).astype(BF16)

    @pl.when(i >= nt_ref[0])
    def _():
        o_ref[...] = jnp.zeros(o_ref.shape, o_ref.dtype)


def _moe_up(li, tile_e, n_used, hs, wg, wu):
    n_rows = hs.shape[0] // RT
    nt = n_rows // TM_MOE
    wspec = pl.BlockSpec((None, None, D_MODEL, TF_UP), lambda j, i, te, n: (li, te[_tile_of(i, n)], 0, j))
    return pl.pallas_call(
        _moe_up_kernel,
        grid_spec=pltpu.PrefetchScalarGridSpec(
            num_scalar_prefetch=2,
            grid=(D_FF // TF_UP, nt),
            in_specs=[pl.BlockSpec((TM_MOE * RT, V7X_LANES), lambda j, i, te, n: (_tile_of(i, n), 0)), wspec, wspec],
            out_specs=pl.BlockSpec((TM_MOE, TF_UP), lambda j, i, te, n: (i, j)),
            scratch_shapes=[pltpu.VMEM((TM_MOE * RT, V7X_LANES), F32)],
        ),
        out_shape=jax.ShapeDtypeStruct((n_rows, D_FF), BF16),
        compiler_params=_cparams(("arbitrary", "arbitrary")),
        name="moe_up",
    )(tile_e, n_used, hs, wg, wu)


def _moe_down_kernel(te_ref, nt_ref, a_ref, wd_ref, o_ref):
    i = pl.program_id(0)

    @pl.when(i < nt_ref[0])
    def _():
        a = a_ref[...]
        for c in range(D_MODEL // V7X_MXU):
            y = jnp.dot(a, wd_ref[:, c * V7X_MXU:(c + 1) * V7X_MXU].astype(BF16), preferred_element_type=F32)
            for k in range(V7X_MXU // V7X_LANES):
                s = c * (V7X_MXU // V7X_LANES) + k
                o_ref[pl.ds(s, TM_MOE, stride=RT), :] = y[:, k * V7X_LANES:(k + 1) * V7X_LANES]

    @pl.when(i >= nt_ref[0])
    def _():
        o_ref[...] = jnp.zeros(o_ref.shape, o_ref.dtype)


def _moe_down(li, tile_e, n_used, act, wd):
    n_rows = act.shape[0]
    nt = n_rows // TM_MOE
    return pl.pallas_call(
        _moe_down_kernel,
        grid_spec=pltpu.PrefetchScalarGridSpec(
            num_scalar_prefetch=2,
            grid=(nt,),
            in_specs=[
                pl.BlockSpec((TM_MOE, D_FF), lambda i, te, n: (_tile_of(i, n), 0)),
                pl.BlockSpec((None, None, D_FF, D_MODEL), lambda i, te, n: (li, te[_tile_of(i, n)], 0, 0)),
            ],
            out_specs=pl.BlockSpec((TM_MOE * RT, V7X_LANES), lambda i, te, n: (i, 0)),
        ),
        out_shape=jax.ShapeDtypeStruct((n_rows * RT, V7X_LANES), F32),
        compiler_params=_cparams(("arbitrary",)),
        name="moe_down",
    )(tile_e, n_used, act, wd)


def _combine_kernel(plo_ref, phi_ref, x_ref, wlo_ref, whi_ref, y_ref, o_ref, lo_ref, hi_ref, sem):
    t0 = pl.program_id(0) * TM

    def issue(r, carry):
        pltpu.make_async_copy(_rt_rows(y_ref, plo_ref[t0 + r]), _rt_rows(lo_ref, r), sem).start(priority=0)
        pltpu.make_async_copy(_rt_rows(y_ref, phi_ref[t0 + r]), _rt_rows(hi_ref, r), sem).start(priority=1)
        return carry

    lax.fori_loop(0, TM, issue, 0, unroll=DMA_UNROLL)
    pltpu.make_async_copy(_rt_rows(y_ref, 0, TM), lo_ref, sem).wait()
    pltpu.make_async_copy(_rt_rows(y_ref, 0, TM), hi_ref, sem).wait()
    w_lo, w_hi = wlo_ref[...], whi_ref[...]
    for s in range(RT):
        cols = slice(s * V7X_LANES, (s + 1) * V7X_LANES)
        o_ref[:, cols] = x_ref[:, cols] + (w_lo * lo_ref[pl.ds(s, TM, stride=RT), :]
                                           + w_hi * hi_ref[pl.ds(s, TM, stride=RT), :])


def _combine(pos_lo, pos_hi, x, w_lo, w_hi, ys):
    t = x.shape[0]
    return pl.pallas_call(
        _combine_kernel,
        grid_spec=pltpu.PrefetchScalarGridSpec(
            num_scalar_prefetch=2,
            grid=(t // TM,),
            in_specs=[
                pl.BlockSpec((TM, D_MODEL), lambda i, a, b: (i, 0)),
                pl.BlockSpec((TM, 1), lambda i, a, b: (i, 0)),
                pl.BlockSpec((TM, 1), lambda i, a, b: (i, 0)),
                pl.BlockSpec(memory_space=pl.ANY),
            ],
            out_specs=pl.BlockSpec((TM, D_MODEL), lambda i, a, b: (i, 0)),
            scratch_shapes=[pltpu.VMEM((TM * RT, V7X_LANES), F32), pltpu.VMEM((TM * RT, V7X_LANES), F32),
                            pltpu.SemaphoreType.DMA(())],
        ),
        out_shape=jax.ShapeDtypeStruct((t, D_MODEL), F32),
        compiler_params=_cparams(("arbitrary",)),
        name="moe_combine",
    )(pos_lo, pos_hi, x, w_lo, w_hi, ys)


def _moe(li, x, h_rt, gates, rank, counts, e_gate, e_up, e_down):
    t = x.shape[0]
    n_rows = ((2 * t) // TM_MOE + N_EXPERTS) * TM_MOE
    pos, wgt, meta = _plan(gates, rank, counts)
    pos_lo, pos_hi = pos[0], pos[1]
    tile_e, n_used = meta[0], meta[1, :1]
    cnt, base = meta[2, :N_EXPERTS], meta[3, :N_EXPERTS]
    hs = _dispatch(pos_lo, pos_hi, cnt, base, n_used, h_rt, n_rows)
    act = _moe_up(li, tile_e, n_used, hs, e_gate, e_up)
    ys = _moe_down(li, tile_e, n_used, act, e_down)
    return pos_lo, pos_hi, x, wgt[0].reshape(t, 1), wgt[1].reshape(t, 1), ys


def _rope_tables(positions):
    inv_freq = ROPE_THETA ** (-jnp.arange(0, ROT_DIM, 2, dtype=F32) / ROT_DIM)
    ang = positions.astype(F32).reshape(-1, 1) * inv_freq
    cos, sin = jnp.cos(ang), jnp.sin(ang)
    t = cos.shape[0]
    half = ROT_DIM // 2
    ones = jnp.ones((t, HEAD_DIM - ROT_DIM), F32)
    zeros = jnp.zeros((t, HEAD_DIM - ROT_DIM), F32)
    zh = jnp.zeros((t, half), F32)
    reps = V7X_LANES // HEAD_DIM
    cos_t = jnp.tile(jnp.concatenate([cos, cos, ones], axis=1), (1, reps))
    sin_a = jnp.tile(jnp.concatenate([zh, sin, zeros], axis=1), (1, reps))
    sin_b = jnp.tile(jnp.concatenate([-sin, zh, zeros], axis=1), (1, reps))
    return cos_t, sin_a, sin_b


def _dft_tables(seq):
    n2 = jnp.arange(FFT_N2, dtype=jnp.int32)
    m = (n2[:, None] * n2[None, :]) % FFT_N2
    ang = m.astype(F32) * (2.0 * np.pi / FFT_N2)
    c0, s0 = jnp.cos(ang).astype(BF16), jnp.sin(ang).astype(BF16)
    k1 = jnp.arange(FFT_N1, dtype=jnp.int32)
    beta = (k1[:, None] * n2[None, :]).astype(F32) * (2.0 * np.pi / seq)
    cb = jnp.broadcast_to(jnp.cos(beta)[:, :, None], (FFT_N1, FFT_N2, V7X_LANES))
    sb = jnp.broadcast_to(jnp.sin(beta)[:, :, None], (FFT_N1, FFT_N2, V7X_LANES))
    c = jnp.arange(FOURIER_GROUP, dtype=jnp.int32)
    angc = ((c[:, None] * c[None, :]) % FOURIER_GROUP).astype(F32) * (2.0 * np.pi / FOURIER_GROUP)
    return c0, s0, cb, sb, jnp.cos(angc), jnp.sin(angc)


def kernel(x, positions, norm_mix, w_in, q_norm, k_norm, sink, w_fmix, g_attn_out, g_fourier_out, w_out, norm_ffn,
           ffn_gate, ffn_up, ffn_down, w_router, e_gate, e_up, e_down):
    b, s, d = x.shape
    depth = w_in.shape[0]
    assert (d, s) == (D_MODEL, FFT_N1 * FFT_N2) and (b * s) % TM_PLAN == 0
    t = b * s
    cos_t, sin_a, sin_b = _rope_tables(positions)
    c0, s0, cb, sb, cc, sc = _dft_tables(s)
    mst = _fmix_prep(w_fmix, cc, sc)
    head = jnp.arange(D_QK) // HEAD_DIM
    member = head[:, None] == jnp.arange(V7X_LANES)[None, :]
    head_sum = member.astype(BF16) * (1.0 / HEAD_DIM)
    head_expand = member.T.astype(BF16)
    row_i = jnp.arange(TM)
    ltri = (row_i[None, :] < row_i[:, None]).astype(BF16)
    gain = jnp.concatenate([jnp.tile(q_norm, (1, N_Q_HEADS)) * (HEAD_DIM ** -0.5),
                            jnp.tile(k_norm, (1, N_KV_HEADS))], axis=1).reshape(depth, 1, D_QK)
    vec = lambda p: p.reshape(depth, 1, p.shape[-1])
    norm_mix, norm_ffn, g_attn_out, g_fourier_out = vec(norm_mix), vec(norm_ffn), vec(g_attn_out), vec(g_fourier_out)
    wr = jnp.pad(w_router, ((0, 0), (0, 0), (0, V7X_LANES - N_EXPERTS)))
    xt = x.reshape(t, d)
    pending = None
    for l in range(depth):
        if pending is None:
            q, k, v, f = _in_proj(l, xt, norm_mix, w_in, head_sum, head_expand, gain, cos_t, sin_a, sin_b)
        else:
            xt, q, k, v, f = _combine_in_proj(l, pending, norm_mix, w_in, head_sum, head_expand, gain,
                                              cos_t, sin_a, sin_b)
            pending = None
        a = _attention(l, q.reshape(b, s, D_ATTN), k.reshape(b, s, D_KV), v.reshape(b, s, D_KV), sink, g_attn_out)
        fo = _fourier(l, f.reshape(b, s, D_FOURIER), mst, c0, s0, cb, sb, g_fourier_out)
        a2, f2 = a.reshape(t, D_ATTN), fo
        li = l // 2
        if l % 2 == 0:
            xt, h = _out_proj(l, a2, f2, xt, w_out, norm_ffn)
            xt = _ffn(li, xt, h, ffn_gate, ffn_up, ffn_down)
        else:
            xt, h_rt, gates, rank, counts = _out_proj_route(l, li, a2, f2, xt, w_out, norm_ffn, wr, ltri)
            pending = _moe(li, xt, h_rt, gates, rank, counts, e_gate, e_up, e_down)
    if pending is not None:
        xt = _combine(*pending)
    return xt.reshape(b, s, d)
```

```python
import functools

import jax
import jax.numpy as jnp
import numpy as np
from jax import lax
from jax.experimental import pallas as pl
from jax.experimental.pallas import tpu as pltpu

F32 = jnp.float32
BF16 = jnp.bfloat16

D_MODEL = 1024
HEAD_DIM = 64
D_ATTN = 512
N_Q_HEADS = 8
N_KV_HEADS = 2
Q_PER_KV = 4
D_KV = 128
D_FOURIER = 512
N_FOURIER_GROUPS = 8
FOURIER_GROUP = 64
D_QK = D_ATTN + D_KV
D_IN = D_ATTN + 2 * D_KV + D_FOURIER
WINDOW = 128
ROPE_THETA = 500000.0
ROT_DIM = 16
D_FF = 3584
N_EXPERTS = 8
EPS = 1e-6

V7X_LANES = 128
V7X_SUBLANES = 8
V7X_MXU = 256
V7X_VMEM_LIMIT = 56 * 1024 * 1024
RT = D_MODEL // V7X_LANES
assert RT == V7X_SUBLANES

TM = 1024
TQ = 1024
QB = 128
QB_PER_ITER = 1
KV_WIN = QB + 2 * WINDOW
FFT_N1 = 4
FFT_N2 = 1024
FFT_CHUNK = 128
TM_FFN = 1024
TF_FFN = 512
FFN_SLOTS = 4
TM_MOE = 512
TF_UP = D_FF // 2
TM_PLAN = 1024
TM_CMB = 512
DMA_UNROLL = 8


def _cparams(sem, vmem=V7X_VMEM_LIMIT):
    return pltpu.CompilerParams(dimension_semantics=sem, vmem_limit_bytes=vmem)


def _rms(xf, g):
    ms = jnp.mean(xf * xf, axis=-1, keepdims=True)
    return xf * lax.rsqrt(ms + EPS) * g


def _split_bf16(xf):
    hi = xf.astype(BF16)
    lo = (xf - hi.astype(F32)).astype(BF16)
    return hi, lo


def _layer_vec(n, l):
    return pl.BlockSpec((None, 1, n), lambda *_: (l, 0, 0))


def _in_proj_kernel(x_ref, g_ref, w_ref, hs_ref, he_ref, gain_ref, cos_ref, sa_ref, sb_ref,
                    q_ref, k_ref, v_ref, f_ref, wbf_ref):
    @pl.when(pl.program_id(0) == 0)
    def _():
        wbf_ref[...] = w_ref[...].astype(BF16)

    _in_proj_tile(x_ref[...], g_ref, hs_ref, he_ref, gain_ref, cos_ref, sa_ref, sb_ref,
                  q_ref, k_ref, v_ref, f_ref, wbf_ref)


def _in_proj_tile(x, g_ref, hs_ref, he_ref, gain_ref, cos_ref, sa_ref, sb_ref, q_ref, k_ref, v_ref, f_ref, wbf_ref):
    h = _rms(x, g_ref[...]).astype(BF16)
    z = jnp.dot(h, wbf_ref[...], preferred_element_type=F32)
    qk = z[:, :D_QK]
    hi, lo = _split_bf16(qk * qk)
    ms = (jnp.dot(hi, hs_ref[...], preferred_element_type=F32)
          + jnp.dot(lo, hs_ref[...], preferred_element_type=F32))
    r_hi, r_lo = _split_bf16(lax.rsqrt(ms + EPS))
    inv_rms = (jnp.dot(r_hi, he_ref[...], preferred_element_type=F32)
               + jnp.dot(r_lo, he_ref[...], preferred_element_type=F32))
    qkn = qk * inv_rms * gain_ref[...]
    cos_t, sin_a, sin_b = cos_ref[...], sa_ref[...], sb_ref[...]
    for c in range(D_QK // V7X_LANES):
        blk = qkn[:, c * V7X_LANES:(c + 1) * V7X_LANES]
        rot = (blk * cos_t + pltpu.roll(blk, ROT_DIM // 2, 1) * sin_a
               + pltpu.roll(blk, V7X_LANES - ROT_DIM // 2, 1) * sin_b).astype(BF16)
        if c < D_ATTN // V7X_LANES:
            q_ref[:, c * V7X_LANES:(c + 1) * V7X_LANES] = rot
        else:
            k_ref[...] = rot
    v_ref[...] = z[:, D_QK:D_QK + D_KV].astype(BF16)
    f_ref[...] = z[:, D_QK + D_KV:].astype(BF16)


def _in_proj_specs(l, tm, row, fixed):
    in_specs = [
        pl.BlockSpec((tm, D_MODEL), row),
        _layer_vec(D_MODEL, l),
        pl.BlockSpec((None, D_MODEL, D_IN), lambda i, *_: (l, 0, 0)),
        pl.BlockSpec((D_QK, V7X_LANES), fixed),
        pl.BlockSpec((V7X_LANES, D_QK), fixed),
        _layer_vec(D_QK, l),
        pl.BlockSpec((tm, V7X_LANES), row),
        pl.BlockSpec((tm, V7X_LANES), row),
        pl.BlockSpec((tm, V7X_LANES), row),
    ]
    out_specs = [
        pl.BlockSpec((tm, D_ATTN), row),
        pl.BlockSpec((tm, D_KV), row),
        pl.BlockSpec((tm, D_KV), row),
        pl.BlockSpec((tm, D_FOURIER), row),
    ]
    return in_specs, out_specs


def _in_proj_out_shapes(t):
    return [
        jax.ShapeDtypeStruct((t, D_ATTN), BF16),
        jax.ShapeDtypeStruct((t, D_KV), BF16),
        jax.ShapeDtypeStruct((t, D_KV), BF16),
        jax.ShapeDtypeStruct((t, D_FOURIER), BF16),
    ]


def _in_proj(l, x, g, w, head_sum, head_expand, gain, cos_t, sin_a, sin_b):
    t = x.shape[0]
    in_specs, out_specs = _in_proj_specs(l, TM, lambda i: (i, 0), lambda i: (0, 0))
    return pl.pallas_call(
        _in_proj_kernel,
        grid=(t // TM,),
        in_specs=in_specs,
        out_specs=out_specs,
        out_shape=_in_proj_out_shapes(t),
        scratch_shapes=[pltpu.VMEM((D_MODEL, D_IN), BF16)],
        compiler_params=_cparams(("arbitrary",)),
        name="in_proj",
    )(x, g, w, head_sum, head_expand, gain, cos_t, sin_a, sin_b)


def _combine_in_proj_kernel(plo_ref, phi_ref, x_ref, g_ref, w_ref, hs_ref, he_ref, gain_ref, cos_ref, sa_ref, sb_ref,
                            wlo_ref, whi_ref, y_ref, xo_ref, q_ref, k_ref, v_ref, f_ref,
                            wbf_ref, lo_ref, hi_ref, sems):
    i = pl.program_id(0)
    n = pl.num_programs(0)

    def gather(tile, slot):
        t0 = tile * TM_CMB
        for r in range(TM_CMB):
            pltpu.make_async_copy(_rt_rows(y_ref, plo_ref[t0 + r]), _rt_rows(lo_ref.at[slot], r),
                                  sems.at[slot]).start(priority=0)
            pltpu.make_async_copy(_rt_rows(y_ref, phi_ref[t0 + r]), _rt_rows(hi_ref.at[slot], r),
                                  sems.at[slot]).start(priority=1)

    def gather_wait(slot):
        pltpu.make_async_copy(_rt_rows(y_ref, 0, TM_CMB), lo_ref.at[slot], sems.at[slot]).wait()
        pltpu.make_async_copy(_rt_rows(y_ref, 0, TM_CMB), hi_ref.at[slot], sems.at[slot]).wait()

    @pl.when(i == 0)
    def _():
        wbf_ref[...] = w_ref[...].astype(BF16)

        def first(r, carry):
            pltpu.make_async_copy(_rt_rows(y_ref, plo_ref[r]), _rt_rows(lo_ref.at[0], r), sems.at[0]).start(priority=0)
            pltpu.make_async_copy(_rt_rows(y_ref, phi_ref[r]), _rt_rows(hi_ref.at[0], r), sems.at[0]).start(priority=1)
            return carry

        lax.fori_loop(0, TM_CMB, first, 0, unroll=DMA_UNROLL)

    slot = i % 2
    gather(jnp.minimum(i + 1, n - 1), 1 - slot)
    gather_wait(slot)
    w_lo, w_hi = wlo_ref[...], whi_ref[...]
    for s in range(RT):
        cols = slice(s * V7X_LANES, (s + 1) * V7X_LANES)
        xo_ref[:, cols] = x_ref[:, cols] + (w_lo * lo_ref[slot, pl.ds(s, TM_CMB, stride=RT), :]
                                             + w_hi * hi_ref[slot, pl.ds(s, TM_CMB, stride=RT), :])
    _in_proj_tile(xo_ref[...], g_ref, hs_ref, he_ref, gain_ref, cos_ref, sa_ref, sb_ref,
                  q_ref, k_ref, v_ref, f_ref, wbf_ref)

    @pl.when(i == n - 1)
    def _():
        gather_wait(1 - slot)


def _combine_in_proj(l, pending, g, w, head_sum, head_expand, gain, cos_t, sin_a, sin_b):
    pos_lo, pos_hi, x, w_lo, w_hi, ys = pending
    t = x.shape[0]
    row = lambda i, a, b: (i, 0)
    in_specs, out_specs = _in_proj_specs(l, TM_CMB, row, lambda i, a, b: (0, 0))
    return pl.pallas_call(
        _combine_in_proj_kernel,
        grid_spec=pltpu.PrefetchScalarGridSpec(
            num_scalar_prefetch=2,
            grid=(t // TM_CMB,),
            in_specs=in_specs + [pl.BlockSpec((TM_CMB, 1), row), pl.BlockSpec((TM_CMB, 1), row),
                                 pl.BlockSpec(memory_space=pl.ANY)],
            out_specs=[pl.BlockSpec((TM_CMB, D_MODEL), row)] + out_specs,
            scratch_shapes=[
                pltpu.VMEM((D_MODEL, D_IN), BF16),
                pltpu.VMEM((2, TM_CMB * RT, V7X_LANES), F32),
                pltpu.VMEM((2, TM_CMB * RT, V7X_LANES), F32),
                pltpu.SemaphoreType.DMA((2,)),
            ],
        ),
        out_shape=[jax.ShapeDtypeStruct((t, D_MODEL), F32)] + _in_proj_out_shapes(t),
        compiler_params=_cparams(("arbitrary",)),
        name="combine_in_proj",
    )(pos_lo, pos_hi, x, g, w, head_sum, head_expand, gain, cos_t, sin_a, sin_b, w_lo, w_hi, ys)


def _attn_kernel(sink_ref, q_ref, k_ref, v_ref, g_ref, o_ref, s_ref, p_ref, *, seq, layer):
    q0 = pl.program_id(1) * TQ
    gain = g_ref[...]
    heads = range(N_Q_HEADS)

    def blocks(j, carry):
        units = [(b, h) for b in range(QB_PER_ITER) for h in heads]
        r0, kw, vw, mask, q = [], [], [], [], []
        for b in range(QB_PER_ITER):
            r = pl.multiple_of((j * QB_PER_ITER + b) * QB, QB)
            qs = q0 + r
            ws = pl.multiple_of(jnp.clip(qs - WINDOW, 0, seq - KV_WIN), QB)
            r0.append(r)
            kw.append(k_ref[pl.ds(ws, KV_WIN), :])
            vw.append(v_ref[pl.ds(ws, KV_WIN), :])
            q_abs = qs + lax.broadcasted_iota(jnp.int32, (QB, KV_WIN), 0)
            k_abs = ws + lax.broadcasted_iota(jnp.int32, (QB, KV_WIN), 1)
            mask.append(jnp.abs(q_abs - k_abs) <= WINDOW)
            q.append(q_ref[pl.ds(r, QB), :])
        for b, h in units:
            kvh = h // Q_PER_KV
            qh = q[b][:, h * HEAD_DIM:(h + 1) * HEAD_DIM]
            kh = kw[b][:, kvh * HEAD_DIM:(kvh + 1) * HEAD_DIM]
            s = lax.dot_general(qh, kh, (((1,), (1,)), ((), ())), preferred_element_type=F32)
            s_ref[b, h] = jnp.where(mask[b], s, -jnp.inf)
        m = {(b, h): jnp.maximum(jnp.max(s_ref[b, h], axis=-1, keepdims=True), sink_ref[layer, h])
             for b, h in units}
        inv = {}
        for b, h in units:
            p = jnp.exp(s_ref[b, h] - m[b, h])
            inv[b, h] = 1.0 / (jnp.sum(p, axis=-1, keepdims=True) + jnp.exp(sink_ref[layer, h] - m[b, h]))
            p_ref[b, h] = p.astype(BF16)
        for b in range(QB_PER_ITER):
            outs = []
            for h in heads:
                kvh = h // Q_PER_KV
                vh = vw[b][:, kvh * HEAD_DIM:(kvh + 1) * HEAD_DIM]
                outs.append(jnp.dot(p_ref[b, h], vh, preferred_element_type=F32) * inv[b, h])
            a = jnp.concatenate(outs, axis=-1)
            o_ref[pl.ds(r0[b], QB), :] = _rms(a, gain).astype(BF16)
        return carry

    lax.fori_loop(0, TQ // (QB * QB_PER_ITER), blocks, 0)


def _attention(l, q, k, v, sink, g):
    b, s, _ = q.shape
    return pl.pallas_call(
        functools.partial(_attn_kernel, seq=s, layer=l),
        grid=(b, s // TQ),
        in_specs=[
            pl.BlockSpec(memory_space=pltpu.SMEM),
            pl.BlockSpec((None, TQ, D_ATTN), lambda i, j: (i, j, 0)),
            pl.BlockSpec((None, s, D_KV), lambda i, j: (i, 0, 0)),
            pl.BlockSpec((None, s, D_KV), lambda i, j: (i, 0, 0)),
            _layer_vec(D_ATTN, l),
        ],
        out_specs=pl.BlockSpec((None, TQ, D_ATTN), lambda i, j: (i, j, 0)),
        out_shape=jax.ShapeDtypeStruct((b, s, D_ATTN), BF16),
        scratch_shapes=[pltpu.VMEM((QB_PER_ITER, N_Q_HEADS, QB, KV_WIN), F32),
                        pltpu.VMEM((QB_PER_ITER, N_Q_HEADS, QB, KV_WIN), BF16)],
        compiler_params=_cparams(("arbitrary", "arbitrary")),
        name="band_attn",
    )(sink, q, k, v, g)


def _fmix_prep_kernel(w_ref, cc_ref, sc_ref, o_ref):
    o_ref[...] = jnp.zeros(o_ref.shape, o_ref.dtype)
    scale = float(FFT_N1 * FFT_N2 * FOURIER_GROUP) ** -0.5
    for g in range(N_FOURIER_GROUPS):
        w = w_ref[g]
        mr = jnp.dot(cc_ref[...], w, preferred_element_type=F32, precision=lax.Precision.HIGHEST) * scale
        mi = jnp.dot(sc_ref[...], w, preferred_element_type=F32, precision=lax.Precision.HIGHEST) * (-scale)
        c = (g * FOURIER_GROUP) // FFT_CHUNK
        off = (g * FOURIER_GROUP) % FFT_CHUNK
        rows = slice(g * FOURIER_GROUP, (g + 1) * FOURIER_GROUP)
        o_ref[c, rows, off:off + FOURIER_GROUP] = mr.astype(BF16)
        o_ref[c, rows, FFT_CHUNK + off:FFT_CHUNK + off + FOURIER_GROUP] = mi.astype(BF16)


def _fmix_prep(w_fmix, cc, sc):
    depth = w_fmix.shape[0]
    n_chunk = D_FOURIER // FFT_CHUNK
    return pl.pallas_call(
        _fmix_prep_kernel,
        grid=(depth,),
        in_specs=[
            pl.BlockSpec((None, N_FOURIER_GROUPS, FOURIER_GROUP, FOURIER_GROUP), lambda l: (l, 0, 0, 0)),
            pl.BlockSpec((FOURIER_GROUP, FOURIER_GROUP), lambda l: (0, 0)),
            pl.BlockSpec((FOURIER_GROUP, FOURIER_GROUP), lambda l: (0, 0)),
        ],
        out_specs=pl.BlockSpec((None, n_chunk, D_FOURIER, 2 * FFT_CHUNK), lambda l: (l, 0, 0, 0)),
        out_shape=jax.ShapeDtypeStruct((depth, n_chunk, D_FOURIER, 2 * FFT_CHUNK), BF16),
        compiler_params=_cparams(("arbitrary",)),
        name="fmix_prep",
    )(w_fmix, cc, sc)


def _fourier_kernel(f_ref, m_ref, c0_ref, s0_ref, cb_ref, sb_ref, g_ref, o_ref, v_ref):
    k1 = pl.program_id(1)

    @pl.when(k1 == 0)
    def _():
        for c in range(D_FOURIER // FFT_CHUNK):
            mc = m_ref[c]
            z = [jnp.dot(f_ref[n1 * FFT_N2:(n1 + 1) * FFT_N2, :], mc, preferred_element_type=F32)
                 for n1 in range(FFT_N1)]
            zr = [t[:, :FFT_CHUNK] for t in z]
            zi = [t[:, FFT_CHUNK:] for t in z]
            ar, ai = zr[0] + zr[2], zi[0] + zi[2]
            br, bi = zr[1] + zr[3], zi[1] + zi[3]
            cr, ci = zr[0] - zr[2], zi[0] - zi[2]
            dr, di = zr[1] - zr[3], zi[1] - zi[3]
            u = [(ar + br, ai + bi), (cr + di, ci - dr), (ar - br, ai - bi), (cr - di, ci + dr)]
            for kk in range(FFT_N1):
                ur, ui = u[kk]
                cb, sb = cb_ref[kk], sb_ref[kk]
                v_ref[kk, :, c * FFT_CHUNK:(c + 1) * FFT_CHUNK] = (ur * cb + ui * sb).astype(BF16)
                v_ref[kk, :, D_FOURIER + c * FFT_CHUNK:D_FOURIER + (c + 1) * FFT_CHUNK] = (
                    ui * cb - ur * sb).astype(BF16)

    vr = v_ref[k1, :, :D_FOURIER]
    vi = v_ref[k1, :, D_FOURIER:]
    out = (jnp.dot(c0_ref[...], vr, preferred_element_type=F32)
           + jnp.dot(s0_ref[...], vi, preferred_element_type=F32))
    o_ref[...] = _rms(out, g_ref[...]).astype(BF16)


def _fourier(l, f, mst, c0, s0, cb, sb, g):
    b, s, _ = f.shape
    n_chunk = D_FOURIER // FFT_CHUNK
    out = pl.pallas_call(
        _fourier_kernel,
        grid=(b, FFT_N1),
        in_specs=[
            pl.BlockSpec((None, s, D_FOURIER), lambda i, j: (i, 0, 0)),
            pl.BlockSpec((None, n_chunk, D_FOURIER, 2 * FFT_CHUNK), lambda i, j: (l, 0, 0, 0)),
            pl.BlockSpec((FFT_N2, FFT_N2), lambda i, j: (0, 0)),
            pl.BlockSpec((FFT_N2, FFT_N2), lambda i, j: (0, 0)),
            pl.BlockSpec((FFT_N1, FFT_N2, V7X_LANES), lambda i, j: (0, 0, 0)),
            pl.BlockSpec((FFT_N1, FFT_N2, V7X_LANES), lambda i, j: (0, 0, 0)),
            _layer_vec(D_FOURIER, l),
        ],
        out_specs=pl.BlockSpec((None, FFT_N2, D_FOURIER), lambda i, j: (i, 0, j)),
        out_shape=jax.ShapeDtypeStruct((b, FFT_N2, FFT_N1 * D_FOURIER), BF16),
        scratch_shapes=[pltpu.VMEM((FFT_N1, FFT_N2, 2 * D_FOURIER), BF16)],
        compiler_params=_cparams(("arbitrary", "arbitrary")),
        name="fourier_mix",
    )(f, mst, c0, s0, cb, sb, g)
    return out.reshape(b * FFT_N2, FFT_N1 * D_FOURIER)


def _out_proj_body(a_ref, f_ref, x_ref, w_ref, g_ref, wbf_ref, fs_ref):
    @pl.when(pl.program_id(0) == 0)
    def _():
        wbf_ref[...] = w_ref[...].astype(BF16)

    return _out_proj_tile(a_ref, f_ref, x_ref, g_ref, wbf_ref, fs_ref)


def _out_proj_tile(a_ref, f_ref, x_ref, g_ref, wbf_ref, fs_ref):
    n_blk = D_FOURIER // V7X_LANES
    rows = x_ref.shape[0]
    for k1 in range(FFT_N1):
        for c in range(n_blk):
            col = k1 * D_FOURIER + c * V7X_LANES
            fs_ref[c, pl.ds(k1, rows // FFT_N1, stride=FFT_N1), :] = f_ref[:, col:col + V7X_LANES].astype(F32)
    fo = jnp.concatenate([fs_ref[c] for c in range(n_blk)], axis=1).astype(BF16)
    y = (jnp.dot(a_ref[...], wbf_ref[:D_ATTN, :], preferred_element_type=F32)
         + jnp.dot(fo, wbf_ref[D_ATTN:, :], preferred_element_type=F32))
    xn = x_ref[...] + y
    return xn, _rms(xn, g_ref[...])


def _out_proj_dense_kernel(a_ref, f_ref, x_ref, w_ref, g_ref, xo_ref, h_ref, wbf_ref, fs_ref):
    xn, h = _out_proj_body(a_ref, f_ref, x_ref, w_ref, g_ref, wbf_ref, fs_ref)
    xo_ref[...] = xn
    h_ref[...] = h.astype(BF16)


def _out_proj_moe_kernel(a_ref, f_ref, x_ref, w_ref, g_ref, wr_ref, lt_ref,
                         xo_ref, hrt_ref, gate_ref, rank_ref, cnt_ref,
                         wbf_ref, fs_ref, wr2_ref, hs_ref, carry_ref):
    @pl.when(pl.program_id(0) == 0)
    def _():
        hi, lo = _split_bf16(wr_ref[...])
        wr2_ref[:, :V7X_LANES] = hi
        wr2_ref[:, V7X_LANES:] = lo
        carry_ref[...] = jnp.zeros(carry_ref.shape, F32)

    xn, h = _out_proj_body(a_ref, f_ref, x_ref, w_ref, g_ref, wbf_ref, fs_ref)
    xo_ref[...] = xn
    for s in range(RT):
        hs_ref[pl.ds(s, TM, stride=RT), :] = h[:, s * V7X_LANES:(s + 1) * V7X_LANES]
    hrt_ref[...] = hs_ref[...].astype(BF16)
    h_hi, h_lo = _split_bf16(h)
    both = (jnp.dot(h_hi, wr2_ref[...], preferred_element_type=F32)
            + jnp.dot(h_lo, wr2_ref[...], preferred_element_type=F32))
    logits = both[:, :V7X_LANES] + both[:, V7X_LANES:]
    lane = lax.broadcasted_iota(jnp.int32, logits.shape, 1).astype(F32)
    logits = jnp.where(lane < N_EXPERTS, logits, -jnp.inf)
    m1 = jnp.max(logits, axis=-1, keepdims=True)
    i1 = jnp.min(jnp.where(logits == m1, lane, float(V7X_LANES)), axis=-1, keepdims=True)
    rest = jnp.where(lane == i1, -jnp.inf, logits)
    m2 = jnp.max(rest, axis=-1, keepdims=True)
    i2 = jnp.min(jnp.where(rest == m2, lane, float(V7X_LANES)), axis=-1, keepdims=True)
    e2 = jnp.exp(m2 - m1)
    den = 1.0 + e2
    sel1, sel2 = lane == i1, lane == i2
    sel = sel1 | sel2
    gate_ref[...] = jnp.where(sel1, 1.0 / den, jnp.where(sel2, e2 / den, 0.0))
    self32 = sel.astype(F32)
    rank = jnp.dot(lt_ref[...], self32.astype(BF16), preferred_element_type=F32) + carry_ref[...]
    rank_ref[...] = jnp.where(sel, rank, -1.0)
    carry_ref[...] = carry_ref[...] + jnp.sum(self32, axis=0, keepdims=True)
    cnt_ref[...] = carry_ref[...]


def _out_proj_specs(l):
    row = lambda i: (i, 0)
    return [
        pl.BlockSpec((TM, D_ATTN), row),
        pl.BlockSpec((TM // FFT_N1, FFT_N1 * D_FOURIER), row),
        pl.BlockSpec((TM, D_MODEL), row),
        pl.BlockSpec((None, D_MODEL, D_MODEL), lambda i: (l, 0, 0)),
        _layer_vec(D_MODEL, l),
    ]


def _out_proj(l, a, f, x, w, g):
    t = x.shape[0]
    row = lambda i: (i, 0)
    return pl.pallas_call(
        _out_proj_dense_kernel,
        grid=(t // TM,),
        in_specs=_out_proj_specs(l),
        out_specs=[pl.BlockSpec((TM, D_MODEL), row), pl.BlockSpec((TM, D_MODEL), row)],
        out_shape=[jax.ShapeDtypeStruct((t, D_MODEL), F32), jax.ShapeDtypeStruct((t, D_MODEL), BF16)],
        scratch_shapes=[pltpu.VMEM((D_MODEL, D_MODEL), BF16), pltpu.VMEM((D_FOURIER // V7X_LANES, TM, V7X_LANES), F32)],
        compiler_params=_cparams(("arbitrary",)),
        name="out_proj",
    )(a, f, x, w, g)


def _out_proj_route(l, li, a, f, x, w, g, w_router_pad, ltri):
    t = x.shape[0]
    row = lambda i: (i, 0)
    fixed = lambda i: (0, 0)
    in_specs = _out_proj_specs(l)
    return pl.pallas_call(
        _out_proj_moe_kernel,
        grid=(t // TM,),
        in_specs=in_specs + [pl.BlockSpec((None, D_MODEL, V7X_LANES), lambda i: (li, 0, 0)),
                             pl.BlockSpec((TM, TM), fixed)],
        out_specs=[
            pl.BlockSpec((TM, D_MODEL), row),
            pl.BlockSpec((TM * RT, V7X_LANES), row),
            pl.BlockSpec((TM, V7X_LANES), row),
            pl.BlockSpec((TM, V7X_LANES), row),
            pl.BlockSpec((1, V7X_LANES), fixed),
        ],
        out_shape=[
            jax.ShapeDtypeStruct((t, D_MODEL), F32),
            jax.ShapeDtypeStruct((t * RT, V7X_LANES), BF16),
            jax.ShapeDtypeStruct((t, V7X_LANES), F32),
            jax.ShapeDtypeStruct((t, V7X_LANES), F32),
            jax.ShapeDtypeStruct((1, V7X_LANES), F32),
        ],
        scratch_shapes=[
            pltpu.VMEM((D_MODEL, D_MODEL), BF16),
            pltpu.VMEM((D_FOURIER // V7X_LANES, TM, V7X_LANES), F32),
            pltpu.VMEM((D_MODEL, 2 * V7X_LANES), BF16),
            pltpu.VMEM((TM * RT, V7X_LANES), F32),
            pltpu.VMEM((1, V7X_LANES), F32),
        ],
        compiler_params=_cparams(("arbitrary",)),
        name="out_proj_route",
    )(a, f, x, w, g, w_router_pad, ltri)


def _ffn_kernel(x_ref, h_ref, wg_hbm, wu_hbm, wd_hbm, o_ref, act_ref, wg_buf, wu_buf, wd_buf, sems, *, layer):
    n_j = pl.num_programs(1)
    total = pl.num_programs(0) * n_j
    step = pl.program_id(0) * n_j + pl.program_id(1)

    def copies(s):
        j = s % n_j
        slot = s % FFN_SLOTS
        cols = pl.ds(pl.multiple_of(j * TF_FFN, TF_FFN), TF_FFN)
        return (pltpu.make_async_copy(wg_hbm.at[layer, :, cols], wg_buf.at[slot], sems.at[slot]),
                pltpu.make_async_copy(wu_hbm.at[layer, :, cols], wu_buf.at[slot], sems.at[slot]),
                pltpu.make_async_copy(wd_hbm.at[layer, cols, :], wd_buf.at[slot], sems.at[slot]))

    @pl.when(step == 0)
    def _():
        for s in range(FFN_SLOTS - 1):
            for cp in copies(s):
                cp.start()

    @pl.when(step + (FFN_SLOTS - 1) < total)
    def _():
        for cp in copies(step + (FFN_SLOTS - 1)):
            cp.start()

    @pl.when(pl.program_id(1) == 0)
    def _():
        o_ref[...] = x_ref[...]

    for cp in copies(step):
        cp.wait()
    slot = step % FFN_SLOTS
    h = h_ref[...]
    for c in range(TF_FFN // V7X_MXU):
        cols = slice(c * V7X_MXU, (c + 1) * V7X_MXU)
        g = jnp.dot(h, wg_buf[slot, :, cols].astype(BF16), preferred_element_type=F32)
        u = jnp.dot(h, wu_buf[slot, :, cols].astype(BF16), preferred_element_type=F32)
        act_ref[:, cols] = (g * jax.nn.sigmoid(g) * u).astype(BF16)
    o_ref[...] += jnp.dot(act_ref[...], wd_buf[slot].astype(BF16), preferred_element_type=F32)


def _ffn(li, x, h, wg, wu, wd):
    t = x.shape[0]
    hbm = pl.BlockSpec(memory_space=pl.ANY)
    return pl.pallas_call(
        functools.partial(_ffn_kernel, layer=li),
        grid=(t // TM_FFN, D_FF // TF_FFN),
        in_specs=[
            pl.BlockSpec((TM_FFN, D_MODEL), lambda i, j: (i, 0)),
            pl.BlockSpec((TM_FFN, D_MODEL), lambda i, j: (i, 0)),
            hbm, hbm, hbm,
        ],
        out_specs=pl.BlockSpec((TM_FFN, D_MODEL), lambda i, j: (i, 0)),
        out_shape=jax.ShapeDtypeStruct((t, D_MODEL), F32),
        scratch_shapes=[
            pltpu.VMEM((TM_FFN, TF_FFN), BF16),
            pltpu.VMEM((FFN_SLOTS, D_MODEL, TF_FFN), F32),
            pltpu.VMEM((FFN_SLOTS, D_MODEL, TF_FFN), F32),
            pltpu.VMEM((FFN_SLOTS, TF_FFN, D_MODEL), F32),
            pltpu.SemaphoreType.DMA((FFN_SLOTS,)),
        ],
        compiler_params=_cparams(("arbitrary", "arbitrary")),
        name="ffn_dense",
    )(x, h, wg, wu, wd)


def _plan_kernel(gate_ref, rank_ref, cnt_ref, pos_ref, wgt_ref, meta_ref):
    lane = lax.broadcasted_iota(jnp.int32, (1, V7X_LANES), 1)
    cnt = cnt_ref[...]
    padded = jnp.floor((cnt + (TM_MOE - 1)) * (1.0 / TM_MOE)) * TM_MOE
    base = jnp.zeros_like(padded)
    for j in range(1, N_EXPERTS):
        base = base + jnp.where(lane >= j, pltpu.roll(padded, j, 1), 0.0)
    rank = rank_ref[...]
    pos_t = jnp.where(rank >= 0.0, base + rank, -1.0).T
    gate_t = gate_ref[...].T
    seen = jnp.zeros((1, TM_PLAN), F32)
    pos_lo = jnp.zeros((1, TM_PLAN), F32)
    pos_hi = jnp.zeros((1, TM_PLAN), F32)
    w_lo = jnp.zeros((1, TM_PLAN), F32)
    w_hi = jnp.zeros((1, TM_PLAN), F32)
    for e in range(N_EXPERTS):
        p = pos_t[e:e + 1, :]
        gt = gate_t[e:e + 1, :]
        chosen = p >= 0.0
        first = chosen & (seen == 0.0)
        second = chosen & (seen == 1.0)
        pos_lo = jnp.where(first, p, pos_lo)
        pos_hi = jnp.where(second, p, pos_hi)
        w_lo = jnp.where(first, gt, w_lo)
        w_hi = jnp.where(second, gt, w_hi)
        seen = seen + chosen.astype(F32)
    pos_ref[0:1, :] = pos_lo.astype(jnp.int32)
    pos_ref[1:2, :] = pos_hi.astype(jnp.int32)
    wgt_ref[0:1, :] = w_lo
    wgt_ref[1:2, :] = w_hi
    ends = base + padded
    tile_start = (lane * TM_MOE).astype(F32)
    tile_e = jnp.zeros((1, V7X_LANES), F32)
    for e in range(N_EXPERTS):
        tile_e = tile_e + (ends[:, e:e + 1] <= tile_start).astype(F32)
    tile_e = jnp.minimum(tile_e, float(N_EXPERTS - 1))
    n_used = ends[:, N_EXPERTS - 1:N_EXPERTS] * (1.0 / TM_MOE) + jnp.zeros((1, V7X_LANES), F32)
    meta_ref[0:1, :] = tile_e.astype(jnp.int32)
    meta_ref[1:2, :] = n_used.astype(jnp.int32)
    meta_ref[2:3, :] = cnt.astype(jnp.int32)
    meta_ref[3:4, :] = base.astype(jnp.int32)
    meta_ref[4:8, :] = jnp.zeros((4, V7X_LANES), jnp.int32)


def _plan(gates, rank, counts):
    t = gates.shape[0]
    return pl.pallas_call(
        _plan_kernel,
        grid=(t // TM_PLAN,),
        in_specs=[
            pl.BlockSpec((TM_PLAN, V7X_LANES), lambda i: (i, 0)),
            pl.BlockSpec((TM_PLAN, V7X_LANES), lambda i: (i, 0)),
            pl.BlockSpec((1, V7X_LANES), lambda i: (0, 0)),
        ],
        out_specs=[
            pl.BlockSpec((2, TM_PLAN), lambda i: (0, i)),
            pl.BlockSpec((2, TM_PLAN), lambda i: (0, i)),
            pl.BlockSpec((8, V7X_LANES), lambda i: (0, 0)),
        ],
        out_shape=[
            jax.ShapeDtypeStruct((2, t), jnp.int32),
            jax.ShapeDtypeStruct((2, t), F32),
            jax.ShapeDtypeStruct((8, V7X_LANES), jnp.int32),
        ],
        compiler_params=_cparams(("arbitrary",)),
        name="moe_plan",
    )(gates, rank, counts)


def _rt_rows(ref, row, n=1, per=RT):
    start = row * per if isinstance(row, int) else pl.multiple_of(row * per, per)
    return ref.at[pl.ds(start, n * per)]


def _dispatch_kernel(plo_ref, phi_ref, cnt_ref, base_ref, nt_ref, h_ref, o_ref, z_ref, sem):
    i = pl.program_id(0)
    t0 = i * TM

    def issue(r, carry):
        src = _rt_rows(h_ref, r)
        pltpu.make_async_copy(src, _rt_rows(o_ref, plo_ref[t0 + r]), sem).start(priority=0)
        pltpu.make_async_copy(src, _rt_rows(o_ref, phi_ref[t0 + r]), sem).start(priority=1)
        return carry

    lax.fori_loop(0, TM, issue, 0, unroll=DMA_UNROLL)
    for _ in range(2):
        pltpu.make_async_copy(h_ref, _rt_rows(o_ref, 0, TM), sem).wait()

    @pl.when(i == pl.num_programs(0) - 1)
    def _():
        z_ref[...] = jnp.zeros(z_ref.shape, z_ref.dtype)
        for e in range(N_EXPERTS):
            n = cnt_ref[e]
            n_pad = ((n + (TM_MOE - 1)) // TM_MOE) * TM_MOE - n
            first = base_ref[e] + n

            def pad_issue(r, carry, first=first):
                pltpu.make_async_copy(_rt_rows(z_ref, 0), _rt_rows(o_ref, first + r), sem).start()
                return carry

            def pad_drain(r, carry):
                pltpu.make_async_copy(_rt_rows(z_ref, 0), _rt_rows(o_ref, 0), sem).wait()
                return carry

            lax.fori_loop(0, n_pad, pad_issue, 0)
            lax.fori_loop(0, n_pad, pad_drain, 0)

        def tail(tile, carry):
            cp = pltpu.make_async_copy(z_ref, _rt_rows(o_ref, tile * TM_MOE, TM_MOE), sem)
            cp.start()
            cp.wait()
            return carry

        lax.fori_loop(nt_ref[0], o_ref.shape[0] // (TM_MOE * RT), tail, 0)


def _dispatch(pos_lo, pos_hi, cnt, base, n_used, h_rt, n_rows):
    t = h_rt.shape[0] // RT
    return pl.pallas_call(
        _dispatch_kernel,
        grid_spec=pltpu.PrefetchScalarGridSpec(
            num_scalar_prefetch=5,
            grid=(t // TM,),
            in_specs=[pl.BlockSpec((TM * RT, V7X_LANES), lambda i, *_: (i, 0))],
            out_specs=pl.BlockSpec(memory_space=pl.ANY),
            scratch_shapes=[pltpu.VMEM((TM_MOE * RT, V7X_LANES), BF16), pltpu.SemaphoreType.DMA(())],
        ),
        out_shape=jax.ShapeDtypeStruct((n_rows * RT, V7X_LANES), BF16),
        compiler_params=_cparams(("arbitrary",)),
        name="moe_dispatch",
    )(pos_lo, pos_hi, cnt, base, n_used, h_rt)


def _tile_of(i, nt_ref):
    return jnp.minimum(i, nt_ref[0] - 1)


def _moe_up_kernel(te_ref, nt_ref, hs_ref, wg_ref, wu_ref, o_ref, xs_ref):
    i = pl.program_id(1)

    @pl.when(i < nt_ref[0])
    def _():
        xs_ref[...] = hs_ref[...].astype(F32)
        x = jnp.concatenate([xs_ref[pl.ds(s, TM_MOE, stride=RT), :].astype(BF16) for s in range(RT)],
                            axis=1)
        for c in range(TF_UP // V7X_MXU):
            cols = slice(c * V7X_MXU, (c + 1) * V7X_MXU)
            g = jnp.dot(x, wg_ref[:, cols].astype(BF16), preferred_element_type=F32)
            u = jnp.dot(x, wu_ref[:, cols].astype(BF16), preferred_element_type=F32)
            o_ref[:, cols] = (g * jax.nn.sigmoid(g) * u).astype(BF16)

    @pl.when(i >= nt_ref[0])
    def _():
        o_ref[...] = jnp.zeros(o_ref.shape, o_ref.dtype)


def _moe_up(li, tile_e, n_used, hs, wg, wu):
    n_rows = hs.shape[0] // RT
    nt = n_rows // TM_MOE
    wspec = pl.BlockSpec((None, None, D_MODEL, TF_UP), lambda j, i, te, n: (li, te[_tile_of(i, n)], 0, j))
    return pl.pallas_call(
        _moe_up_kernel,
        grid_spec=pltpu.PrefetchScalarGridSpec(
            num_scalar_prefetch=2,
            grid=(D_FF // TF_UP, nt),
            in_specs=[pl.BlockSpec((TM_MOE * RT, V7X_LANES), lambda j, i, te, n: (_tile_of(i, n), 0)), wspec, wspec],
            out_specs=pl.BlockSpec((TM_MOE, TF_UP), lambda j, i, te, n: (i, j)),
            scratch_shapes=[pltpu.VMEM((TM_MOE * RT, V7X_LANES), F32)],
        ),
        out_shape=jax.ShapeDtypeStruct((n_rows, D_FF), BF16),
        compiler_params=_cparams(("arbitrary", "arbitrary")),
        name="moe_up",
    )(tile_e, n_used, hs, wg, wu)


def _moe_down_kernel(te_ref, nt_ref, a_ref, wd_ref, o_ref):
    i = pl.program_id(0)

    @pl.when(i < nt_ref[0])
    def _():
        a = a_ref[...]
        for c in range(D_MODEL // V7X_MXU):
            y = jnp.dot(a, wd_ref[:, c * V7X_MXU:(c + 1) * V7X_MXU].astype(BF16), preferred_element_type=F32)
            for k in range(V7X_MXU // V7X_LANES):
                s = c * (V7X_MXU // V7X_LANES) + k
                o_ref[pl.ds(s, TM_MOE, stride=RT), :] = y[:, k * V7X_LANES:(k + 1) * V7X_LANES]

    @pl.when(i >= nt_ref[0])
    def _():
        o_ref[...] = jnp.zeros(o_ref.shape, o_ref.dtype)


def _moe_down(li, tile_e, n_used, act, wd):
    n_rows = act.shape[0]
    nt = n_rows // TM_MOE
    return pl.pallas_call(
        _moe_down_kernel,
        grid_spec=pltpu.PrefetchScalarGridSpec(
            num_scalar_prefetch=2,
            grid=(nt,),
            in_specs=[
                pl.BlockSpec((TM_MOE, D_FF), lambda i, te, n: (_tile_of(i, n), 0)),
                pl.BlockSpec((None, None, D_FF, D_MODEL), lambda i, te, n: (li, te[_tile_of(i, n)], 0, 0)),
            ],
            out_specs=pl.BlockSpec((TM_MOE * RT, V7X_LANES), lambda i, te, n: (i, 0)),
        ),
        out_shape=jax.ShapeDtypeStruct((n_rows * RT, V7X_LANES), F32),
        compiler_params=_cparams(("arbitrary",)),
        name="moe_down",
    )(tile_e, n_used, act, wd)


def _combine_kernel(plo_ref, phi_ref, x_ref, wlo_ref, whi_ref, y_ref, o_ref, lo_ref, hi_ref, sem):
    t0 = pl.program_id(0) * TM

    def issue(r, carry):
        pltpu.make_async_copy(_rt_rows(y_ref, plo_ref[t0 + r]), _rt_rows(lo_ref, r), sem).start(priority=0)
        pltpu.make_async_copy(_rt_rows(y_ref, phi_ref[t0 + r]), _rt_rows(hi_ref, r), sem).start(priority=1)
        return carry

    lax.fori_loop(0, TM, issue, 0, unroll=DMA_UNROLL)
    pltpu.make_async_copy(_rt_rows(y_ref, 0, TM), lo_ref, sem).wait()
    pltpu.make_async_copy(_rt_rows(y_ref, 0, TM), hi_ref, sem).wait()
    w_lo, w_hi = wlo_ref[...], whi_ref[...]
    for s in range(RT):
        cols = slice(s * V7X_LANES, (s + 1) * V7X_LANES)
        o_ref[:, cols] = x_ref[:, cols] + (w_lo * lo_ref[pl.ds(s, TM, stride=RT), :]
                                           + w_hi * hi_ref[pl.ds(s, TM, stride=RT), :])


def _combine(pos_lo, pos_hi, x, w_lo, w_hi, ys):
    t = x.shape[0]
    return pl.pallas_call(
        _combine_kernel,
        grid_spec=pltpu.PrefetchScalarGridSpec(
            num_scalar_prefetch=2,
            grid=(t // TM,),
            in_specs=[
                pl.BlockSpec((TM, D_MODEL), lambda i, a, b: (i, 0)),
                pl.BlockSpec((TM, 1), lambda i, a, b: (i, 0)),
                pl.BlockSpec((TM, 1), lambda i, a, b: (i, 0)),
                pl.BlockSpec(memory_space=pl.ANY),
            ],
            out_specs=pl.BlockSpec((TM, D_MODEL), lambda i, a, b: (i, 0)),
            scratch_shapes=[pltpu.VMEM((TM * RT, V7X_LANES), F32), pltpu.VMEM((TM * RT, V7X_LANES), F32),
                            pltpu.SemaphoreType.DMA(())],
        ),
        out_shape=jax.ShapeDtypeStruct((t, D_MODEL), F32),
        compiler_params=_cparams(("arbitrary",)),
        name="moe_combine",
    )(pos_lo, pos_hi, x, w_lo, w_hi, ys)


def _moe(li, x, h_rt, gates, rank, counts, e_gate, e_up, e_down):
    t = x.shape[0]
    n_rows = ((2 * t) // TM_MOE + N_EXPERTS) * TM_MOE
    pos, wgt, meta = _plan(gates, rank, counts)
    pos_lo, pos_hi = pos[0], pos[1]
    tile_e, n_used = meta[0], meta[1, :1]
    cnt, base = meta[2, :N_EXPERTS], meta[3, :N_EXPERTS]
    hs = _dispatch(pos_lo, pos_hi, cnt, base, n_used, h_rt, n_rows)
    act = _moe_up(li, tile_e, n_used, hs, e_gate, e_up)
    ys = _moe_down(li, tile_e, n_used, act, e_down)
    return pos_lo, pos_hi, x, wgt[0].reshape(t, 1), wgt[1].reshape(t, 1), ys


def _rope_tables(positions):
    inv_freq = ROPE_THETA ** (-jnp.arange(0, ROT_DIM, 2, dtype=F32) / ROT_DIM)
    ang = positions.astype(F32).reshape(-1, 1) * inv_freq
    cos, sin = jnp.cos(ang), jnp.sin(ang)
    t = cos.shape[0]
    half = ROT_DIM // 2
    ones = jnp.ones((t, HEAD_DIM - ROT_DIM), F32)
    zeros = jnp.zeros((t, HEAD_DIM - ROT_DIM), F32)
    zh = jnp.zeros((t, half), F32)
    reps = V7X_LANES // HEAD_DIM
    cos_t = jnp.tile(jnp.concatenate([cos, cos, ones], axis=1), (1, reps))
    sin_a = jnp.tile(jnp.concatenate([zh, sin, zeros], axis=1), (1, reps))
    sin_b = jnp.tile(jnp.concatenate([-sin, zh, zeros], axis=1), (1, reps))
    return cos_t, sin_a, sin_b


def _dft_tables(seq):
    n2 = jnp.arange(FFT_N2, dtype=jnp.int32)
    m = (n2[:, None] * n2[None, :]) % FFT_N2
    ang = m.astype(F32) * (2.0 * np.pi / FFT_N2)
    c0, s0 = jnp.cos(ang).astype(BF16), jnp.sin(ang).astype(BF16)
    k1 = jnp.arange(FFT_N1, dtype=jnp.int32)
    beta = (k1[:, None] * n2[None, :]).astype(F32) * (2.0 * np.pi / seq)
    cb = jnp.broadcast_to(jnp.cos(beta)[:, :, None], (FFT_N1, FFT_N2, V7X_LANES))
    sb = jnp.broadcast_to(jnp.sin(beta)[:, :, None], (FFT_N1, FFT_N2, V7X_LANES))
    c = jnp.arange(FOURIER_GROUP, dtype=jnp.int32)
    angc = ((c[:, None] * c[None, :]) % FOURIER_GROUP).astype(F32) * (2.0 * np.pi / FOURIER_GROUP)
    return c0, s0, cb, sb, jnp.cos(angc), jnp.sin(angc)


def kernel(x, positions, norm_mix, w_in, q_norm, k_norm, sink, w_fmix, g_attn_out, g_fourier_out, w_out, norm_ffn,
           ffn_gate, ffn_up, ffn_down, w_router, e_gate, e_up, e_down):
    b, s, d = x.shape
    depth = w_in.shape[0]
    assert (d, s) == (D_MODEL, FFT_N1 * FFT_N2) and (b * s) % TM_PLAN == 0
    t = b * s
    cos_t, sin_a, sin_b = _rope_tables(positions)
    c0, s0, cb, sb, cc, sc = _dft_tables(s)
    mst = _fmix_prep(w_fmix, cc, sc)
    head = jnp.arange(D_QK) // HEAD_DIM
    member = head[:, None] == jnp.arange(V7X_LANES)[None, :]
    head_sum = member.astype(BF16) * (1.0 / HEAD_DIM)
    head_expand = member.T.astype(BF16)
    row_i = jnp.arange(TM)
    ltri = (row_i[None, :] < row_i[:, None]).astype(BF16)
    gain = jnp.concatenate([jnp.tile(q_norm, (1, N_Q_HEADS)) * (HEAD_DIM ** -0.5),
                            jnp.tile(k_norm, (1, N_KV_HEADS))], axis=1).reshape(depth, 1, D_QK)
    vec = lambda p: p.reshape(depth, 1, p.shape[-1])
    norm_mix, norm_ffn, g_attn_out, g_fourier_out = vec(norm_mix), vec(norm_ffn), vec(g_attn_out), vec(g_fourier_out)
    wr = jnp.pad(w_router, ((0, 0), (0, 0), (0, V7X_LANES - N_EXPERTS)))
    xt = x.reshape(t, d)
    pending = None
    for l in range(depth):
        if pending is None:
            q, k, v, f = _in_proj(l, xt, norm_mix, w_in, head_sum, head_expand, gain, cos_t, sin_a, sin_b)
        else:
            xt, q, k, v, f = _combine_in_proj(l, pending, norm_mix, w_in, head_sum, head_expand, gain,
                                              cos_t, sin_a, sin_b)
            pending = None
        a = _attention(l, q.reshape(b, s, D_ATTN), k.reshape(b, s, D_KV), v.reshape(b, s, D_KV), sink, g_attn_out)
        fo = _fourier(l, f.reshape(b, s, D_FOURIER), mst, c0, s0, cb, sb, g_fourier_out)
        a2, f2 = a.reshape(t, D_ATTN), fo
        li = l // 2
        if l % 2 == 0:
            xt, h = _out_proj(l, a2, f2, xt, w_out, norm_ffn)
            xt = _ffn(li, xt, h, ffn_gate, ffn_up, ffn_down)
        else:
            xt, h_rt, gates, rank, counts = _out_proj_route(l, li, a2, f2, xt, w_out, norm_ffn, wr, ltri)
            pending = _moe(li, xt, h_rt, gates, rank, counts, e_gate, e_up, e_down)
    if pending is not None:
        xt = _combine(*pending)
    return xt.reshape(b, s, d)
```
